```python
import math
import jax, jax.numpy as jnp
from jax import lax
import numpy as np

D_MODEL = 1024
BATCH = 16
SEQ = 2048
DEPTH = 2

A_GROUPS = 4
A_GROUP_DIM = 128
A_WIDTH = A_GROUPS * A_GROUP_DIM
A_CHUNK = 128
MLA_HEADS = 8
QK_NOPE = 64
QK_ROPE = 32
QK_HEAD = QK_NOPE + QK_ROPE
V_HEAD = 64
Q_LORA = 384
KV_LORA = 256
ROPE_THETA = 10000.0
Q_BLOCK = 128
MAX_POS_OFFSET = 4096
AB_IN = 2 * A_WIDTH + Q_LORA + KV_LORA + QK_ROPE
AB_OUT = A_WIDTH + MLA_HEADS * V_HEAD
S5_GROUP_DIM = 16
S5_GROUPS = D_MODEL // S5_GROUP_DIM
S5_STATE = 64
C_WIDTH = S5_GROUPS * S5_GROUP_DIM
S5_MIN_STEP = 1e-3
S5_MAX_STEP = 1e-1
D_FF = 2816
N_EXPERTS = 8
TOP_K = 2
D_EXPERT = 3584
LN_EPS = 1e-5
RMS_EPS = 1e-6
DEEPNORM_ALPHA = (2.0 * DEPTH) ** 0.25
DEEPNORM_BETA = (8.0 * DEPTH) ** -0.25

kernel_name = 'hybrid_gmlp_mla_s5_moe_deepnorm'


def layernorm(x, g, b):
    xf = x.astype(jnp.float32)
    mu = jnp.mean(xf, axis=-1, keepdims=True)
    var = jnp.mean(jnp.square(xf - mu), axis=-1, keepdims=True)
    return ((xf - mu) * lax.rsqrt(var + LN_EPS)).astype(x.dtype) * g + b


def rmsnorm(x, g):
    xf = x.astype(jnp.float32)
    ms = jnp.mean(jnp.square(xf), axis=-1, keepdims=True)
    return (xf * lax.rsqrt(ms + RMS_EPS)).astype(x.dtype) * g


def rope_cos_sin(positions, dtype):
    inv_freq = 1.0 / (ROPE_THETA ** (jnp.arange(0, QK_ROPE, 2, dtype=jnp.float32) / QK_ROPE))
    ang = positions.astype(jnp.float32)[..., None] * inv_freq
    return jnp.cos(ang).astype(dtype), jnp.sin(ang).astype(dtype)


def apply_rope(x, cos, sin):
    half = QK_ROPE // 2
    x1, x2 = x[..., :half], x[..., half:]
    return jnp.concatenate([x1 * cos - x2 * sin, x1 * sin + x2 * cos], axis=-1)


def gmlp_spatial_gating(z, ln_g, ln_b, w_s, b_s):
    bsz, seq, _ = z.shape
    u, v = jnp.split(z, 2, axis=-1)
    v = layernorm(v, ln_g, ln_b)
    v = v.reshape(bsz, seq // A_CHUNK, A_CHUNK, A_GROUPS, A_GROUP_DIM)
    w = jnp.tril(w_s)
    mixed = jnp.einsum('hts,bnshc->bnthc', w, v) + b_s.T[:, :, None]
    return u * mixed.reshape(bsz, seq, A_WIDTH)


def mla_attention(c_q, c_kv, k_pe, positions, q_norm, w_uq, kv_norm, w_ukv):
    bsz, seq, _ = c_q.shape
    q = (rmsnorm(c_q, q_norm) @ w_uq).reshape(bsz, seq, MLA_HEADS, QK_HEAD)
    q_nope, q_pe = q[..., :QK_NOPE], q[..., QK_NOPE:]
    kv = (rmsnorm(c_kv, kv_norm) @ w_ukv).reshape(bsz, seq, MLA_HEADS, QK_NOPE + V_HEAD)
    k_nope, v = kv[..., :QK_NOPE], kv[..., QK_NOPE:]
    cos, sin = rope_cos_sin(positions, c_q.dtype)
    q_pe = apply_rope(q_pe, cos[:, :, None, :], sin[:, :, None, :])
    k_pe = apply_rope(k_pe, cos, sin)
    q = jnp.concatenate([q_nope, q_pe], axis=-1)
    k = jnp.concatenate([k_nope, jnp.broadcast_to(k_pe[:, :, None, :], (bsz, seq, MLA_HEADS, QK_ROPE))], axis=-1)
    scale = QK_HEAD ** -0.5
    n_blocks = seq // Q_BLOCK
    q_blocks = q.reshape(bsz, n_blocks, Q_BLOCK, MLA_HEADS, QK_HEAD).transpose(1, 0, 2, 3, 4)
    starts = jnp.arange(n_blocks, dtype=jnp.int32) * Q_BLOCK
    key_pos = jnp.arange(seq, dtype=jnp.int32)

    def attend(args):
        qb, start = args
        s = jnp.einsum('bqhd,bkhd->bhqk', qb, k).astype(jnp.float32) * scale
        q_pos = start + jnp.arange(Q_BLOCK, dtype=jnp.int32)
        mask = key_pos[None, :] <= q_pos[:, None]
        s = jnp.where(mask, s, jnp.finfo(jnp.float32).min)
        p = jax.nn.softmax(s, axis=-1).astype(v.dtype)
        return jnp.einsum('bhqk,bkhd->bqhd', p, v)

    out = lax.map(attend, (q_blocks, starts))
    return out.transpose(1, 0, 2, 3, 4).reshape(bsz, seq, MLA_HEADS * V_HEAD)


def ab_mixer(x, positions, w_in, gm_ln_g, gm_ln_b, w_s, b_s, q_norm, w_uq, kv_norm, w_ukv, w_out):
    h = x @ w_in
    a_uv, c_q, c_kv, k_pe = jnp.split(h, [2 * A_WIDTH, 2 * A_WIDTH + Q_LORA, 2 * A_WIDTH + Q_LORA + KV_LORA], axis=-1)
    y_a = gmlp_spatial_gating(jax.nn.gelu(a_uv), gm_ln_g, gm_ln_b, w_s, b_s)
    y_b = mla_attention(c_q, c_kv, k_pe, positions, q_norm, w_uq, kv_norm, w_ukv)
    return jnp.concatenate([y_a, y_b], axis=-1) @ w_out


def _ssm_combine(e1, e2):
    a1r, a1i, b1r, b1i = e1
    a2r, a2i, b2r, b2i = e2
    return (a2r * a1r - a2i * a1i,
            a2r * a1i + a2i * a1r,
            a2r * b1r - a2i * b1i + b2r,
            a2r * b1i + a2i * b1r + b2i)


def s5_mixer(x, w_in, a_re, a_im, log_step, b_re, b_im, c_re, c_im, d_skip, glu_w, glu_b, w_out):
    bsz, seq, _ = x.shape
    f32 = jnp.float32
    u = (x @ w_in).astype(f32)
    ug = u.reshape(bsz, seq, S5_GROUPS, S5_GROUP_DIM)
    a_re, a_im = a_re.astype(f32), a_im.astype(f32)
    delta = jnp.exp(log_step.astype(f32))[:, None]
    mag = jnp.exp(delta * a_re)
    abar_re = mag * jnp.cos(delta * a_im)
    abar_im = mag * jnp.sin(delta * a_im)
    den = a_re * a_re + a_im * a_im
    coef_re = ((abar_re - 1.0) * a_re + abar_im * a_im) / den
    coef_im = (abar_im * a_re - (abar_re - 1.0) * a_im) / den
    br, bi = b_re.astype(f32), b_im.astype(f32)
    bb_re = coef_re[..., None] * br - coef_im[..., None] * bi
    bb_im = coef_re[..., None] * bi + coef_im[..., None] * br
    bu_re = jnp.einsum('bsgc,gpc->sbgp', ug, bb_re)
    bu_im = jnp.einsum('bsgc,gpc->sbgp', ug, bb_im)
    ar = jnp.broadcast_to(abar_re[None, None], (seq, 1, S5_GROUPS, S5_STATE))
    ai = jnp.broadcast_to(abar_im[None, None], (seq, 1, S5_GROUPS, S5_STATE))
    _, _, h_re, h_im = lax.associative_scan(_ssm_combine, (ar, ai, bu_re, bu_im), axis=0)
    y = (jnp.einsum('sbgp,gcp->bsgc', h_re, c_re.astype(f32))
         - jnp.einsum('sbgp,gcp->bsgc', h_im, c_im.astype(f32)))
    y = (y.reshape(bsz, seq, C_WIDTH) + d_skip.astype(f32) * u).astype(x.dtype)
    g = jax.nn.gelu(y)
    z = g * jax.nn.sigmoid(g @ glu_w + glu_b)
    return z @ w_out


def swiglu(x, w_gate, w_up, w_down):
    return (jax.nn.silu(x @ w_gate) * (x @ w_up)) @ w_down


def moe_swiglu(x, router, w_gate, w_up, w_down):
    bsz, seq, d = x.shape
    xt = x.reshape(bsz * seq, d)
    logits = (xt @ router).astype(jnp.float32)
    top_vals, top_idx = lax.top_k(logits, TOP_K)
    gates = jax.nn.softmax(top_vals, axis=-1)
    combine = jnp.sum(gates[..., None] * jax.nn.one_hot(top_idx, N_EXPERTS, dtype=jnp.float32), axis=1).astype(x.dtype)
    out = jnp.zeros_like(xt)
    for e in range(N_EXPERTS):
        out = out + combine[:, e:e + 1] * swiglu(xt, w_gate[e], w_up[e], w_down[e])
    return out.reshape(bsz, seq, d)


def setup_inputs(seed: int = 0) -> dict:
    key = jax.random.key(seed)
    keys = iter(jax.random.split(key, 64))
    ne = (DEPTH + 1) // 2
    no = DEPTH // 2

    def nrm(shape, scale):
        return jax.random.normal(next(keys), shape, jnp.float32) * scale

    def gain(shape):
        return 1.0 + nrm(shape, 0.02)

    x = jax.random.normal(next(keys), (BATCH, SEQ, D_MODEL), jnp.float32)
    offsets = jax.random.randint(next(keys), (BATCH, 1), 0, MAX_POS_OFFSET, dtype=jnp.int32)
    positions = (offsets + jnp.arange(SEQ, dtype=jnp.int32)[None, :]).astype(jnp.int32)
    log_lo, log_hi = math.log(S5_MIN_STEP), math.log(S5_MAX_STEP)
    a_im_init = math.pi * jnp.arange(S5_STATE, dtype=jnp.float32)
    return {
        'x': x,
        'positions': positions,
        'ab_w_in': nrm((ne, D_MODEL, AB_IN), D_MODEL ** -0.5),
        'gm_ln_g': gain((ne, A_WIDTH)),
        'gm_ln_b': nrm((ne, A_WIDTH), 0.02),
        'gm_w_s': nrm((ne, A_GROUPS, A_CHUNK, A_CHUNK), A_CHUNK ** -0.5),
        'gm_b_s': gain((ne, A_GROUPS, A_CHUNK)),
        'mla_q_norm': gain((ne, Q_LORA)),
        'mla_w_uq': nrm((ne, Q_LORA, MLA_HEADS * QK_HEAD), Q_LORA ** -0.5),
        'mla_kv_norm': gain((ne, KV_LORA)),
        'mla_w_ukv': nrm((ne, KV_LORA, MLA_HEADS * (QK_NOPE + V_HEAD)), KV_LORA ** -0.5),
        'ab_w_out': nrm((ne, AB_OUT, D_MODEL), AB_OUT ** -0.5 * DEEPNORM_BETA),
        'ab_ln_g': gain((ne, D_MODEL)),
        'ab_ln_b': nrm((ne, D_MODEL), 0.02),
        'ffd_w_gate': nrm((ne, D_MODEL, D_FF), D_MODEL ** -0.5),
        'ffd_w_up': nrm((ne, D_MODEL, D_FF), D_MODEL ** -0.5),
        'ffd_w_down': nrm((ne, D_FF, D_MODEL), D_FF ** -0.5 * DEEPNORM_BETA),
        'ffd_ln_g': gain((ne, D_MODEL)),
        'ffd_ln_b': nrm((ne, D_MODEL), 0.02),
        'c_w_in': nrm((no, D_MODEL, C_WIDTH), D_MODEL ** -0.5),
        's5_a_re': -0.5 * (1.0 + nrm((no, S5_GROUPS, S5_STATE), 0.02)),
        's5_a_im': a_im_init[None, None, :] + nrm((no, S5_GROUPS, S5_STATE), 0.01),
        's5_log_step': jax.random.uniform(next(keys), (no, S5_GROUPS), jnp.float32, log_lo, log_hi),
        's5_b_re': nrm((no, S5_GROUPS, S5_STATE, S5_GROUP_DIM), (2.0 * S5_GROUP_DIM) ** -0.5),
        's5_b_im': nrm((no, S5_GROUPS, S5_STATE, S5_GROUP_DIM), (2.0 * S5_GROUP_DIM) ** -0.5),
        's5_c_re': nrm((no, S5_GROUPS, S5_GROUP_DIM, S5_STATE), (2.0 * S5_STATE) ** -0.5),
        's5_c_im': nrm((no, S5_GROUPS, S5_GROUP_DIM, S5_STATE), (2.0 * S5_STATE) ** -0.5),
        's5_d': nrm((no, C_WIDTH), 1.0),
        'glu_w': nrm((no, C_WIDTH, C_WIDTH), C_WIDTH ** -0.5),
        'glu_b': nrm((no, C_WIDTH), 0.02),
        'c_w_out': nrm((no, C_WIDTH, D_MODEL), C_WIDTH ** -0.5 * DEEPNORM_BETA),
        'c_ln_g': gain((no, D_MODEL)),
        'c_ln_b': nrm((no, D_MODEL), 0.02),
        'moe_router': nrm((no, D_MODEL, N_EXPERTS), D_MODEL ** -0.5),
        'moe_w_gate': nrm((no, N_EXPERTS, D_MODEL, D_EXPERT), D_MODEL ** -0.5),
        'moe_w_up': nrm((no, N_EXPERTS, D_MODEL, D_EXPERT), D_MODEL ** -0.5),
        'moe_w_down': nrm((no, N_EXPERTS, D_EXPERT, D_MODEL), D_EXPERT ** -0.5 * DEEPNORM_BETA),
        'moe_ln_g': gain((no, D_MODEL)),
        'moe_ln_b': nrm((no, D_MODEL), 0.02),
    }


def reference(x, positions, ab_w_in, gm_ln_g, gm_ln_b, gm_w_s, gm_b_s, mla_q_norm, mla_w_uq, mla_kv_norm, mla_w_ukv, ab_w_out, ab_ln_g, ab_ln_b, ffd_w_gate, ffd_w_up, ffd_w_down, ffd_ln_g, ffd_ln_b, c_w_in, s5_a_re, s5_a_im, s5_log_step, s5_b_re, s5_b_im, s5_c_re, s5_c_im, s5_d, glu_w, glu_b, c_w_out, c_ln_g, c_ln_b, moe_router, moe_w_gate, moe_w_up, moe_w_down, moe_ln_g, moe_ln_b):
    for i in range(DEPTH):
        j = i // 2
        if i % 2 == 0:
            mix = ab_mixer(x, positions, ab_w_in[j], gm_ln_g[j], gm_ln_b[j], gm_w_s[j], gm_b_s[j],
                           mla_q_norm[j], mla_w_uq[j], mla_kv_norm[j], mla_w_ukv[j], ab_w_out[j])
            x = layernorm(DEEPNORM_ALPHA * x + mix, ab_ln_g[j], ab_ln_b[j])
            ffn = swiglu(x, ffd_w_gate[j], ffd_w_up[j], ffd_w_down[j])
            x = layernorm(DEEPNORM_ALPHA * x + ffn, ffd_ln_g[j], ffd_ln_b[j])
        else:
            mix = s5_mixer(x, c_w_in[j], s5_a_re[j], s5_a_im[j], s5_log_step[j], s5_b_re[j], s5_b_im[j],
                           s5_c_re[j], s5_c_im[j], s5_d[j], glu_w[j], glu_b[j], c_w_out[j])
            x = layernorm(DEEPNORM_ALPHA * x + mix, c_ln_g[j], c_ln_b[j])
            ffn = moe_swiglu(x, moe_router[j], moe_w_gate[j], moe_w_up[j], moe_w_down[j])
            x = layernorm(DEEPNORM_ALPHA * x + ffn, moe_ln_g[j], moe_ln_b[j])
    return x
```

```python
import functools
import math

import jax
import jax.numpy as jnp
from jax import lax
from jax.experimental import pallas as pl
from jax.experimental.pallas import tpu as pltpu

F32 = jnp.float32
BF16 = jnp.bfloat16

A_GROUPS = 4
A_GROUP_DIM = 128
A_WIDTH = A_GROUPS * A_GROUP_DIM
A_CHUNK = 128
MLA_HEADS = 8
QK_NOPE = 64
QK_ROPE = 32
QK_HEAD = QK_NOPE + QK_ROPE
V_HEAD = 64
Q_LORA = 384
KV_LORA = 256
ROPE_THETA = 10000.0
S5_GROUP_DIM = 16
S5_STATE = 64
N_EXPERTS = 8
LN_EPS = 1e-5
RMS_EPS = 1e-6
DEPTH = 2
DEEPNORM_ALPHA = (2.0 * DEPTH) ** 0.25

LANES = 128
HEAD_PAD = LANES
S5_CHUNK = 16
VMEM_LIMIT = 56 * 1024 * 1024
NEG_BIG = -1e30


def _params(sem):
    return pltpu.CompilerParams(dimension_semantics=sem, vmem_limit_bytes=VMEM_LIMIT)


def _gelu(x):
    c = math.sqrt(2.0 / math.pi)
    return 0.5 * x * (1.0 + jnp.tanh(c * (x + 0.044715 * (x * x * x))))


def _sigmoid(x):
    return 1.0 / (1.0 + jnp.exp(-x))


def _layernorm(x, g, b):
    mu = jnp.mean(x, axis=-1, keepdims=True)
    xc = x - mu
    var = jnp.mean(xc * xc, axis=-1, keepdims=True)
    return xc * lax.rsqrt(var + LN_EPS) * g + b


def _rmsnorm(x, g):
    ms = jnp.mean(x * x, axis=-1, keepdims=True)
    return x * lax.rsqrt(ms + RMS_EPS) * g


def _dot(a, b):
    return jnp.dot(a, b, preferred_element_type=F32)


def _rope_table_kernel(inv_ref, pos_ref, cos_ref, sin_ref):
    pos = pos_ref[...].astype(F32)
    for j in range(QK_ROPE // 2):
        ang = pos * inv_ref[j]
        cos_ref[j] = jnp.cos(ang)
        sin_ref[j] = jnp.sin(ang)


def _rope_tables(positions):
    n = positions.size
    half = QK_ROPE // 2
    inv_freq = 1.0 / (ROPE_THETA ** (jnp.arange(0, QK_ROPE, 2, dtype=F32) / QK_ROPE))
    pos2d = positions.reshape(n // LANES, LANES)
    cos_t, sin_t = pl.pallas_call(
        _rope_table_kernel,
        out_shape=(jax.ShapeDtypeStruct((half, n // LANES, LANES), F32),) * 2,
        in_specs=[pl.BlockSpec(memory_space=pltpu.SMEM), pl.BlockSpec(memory_space=pltpu.VMEM)],
        out_specs=(pl.BlockSpec(memory_space=pltpu.VMEM),) * 2,
        name="rope_table",
    )(inv_freq, pos2d)
    cos_c = cos_t.reshape(half, n).T
    sin_c = sin_t.reshape(half, n).T
    ones = jnp.ones((n, QK_NOPE), F32)
    zeros_n = jnp.zeros((n, QK_NOPE), F32)
    zeros_p = jnp.zeros((n, HEAD_PAD - QK_HEAD), F32)
    cos_p = jnp.concatenate([ones, cos_c, cos_c, zeros_p], axis=1)
    sin_p = jnp.concatenate([zeros_n, sin_c, sin_c, zeros_p], axis=1)
    return cos_p, sin_p


def _mixer_in_kernel(x_ref, cos_ref, sin_ref, w_in_ref, lng_ref, lnb_ref, ws_ref, bs_ref, qn_ref, wq_ref,
                     kvn_ref, wkv_ref, ya_ref, q_ref, k_ref, v_ref):
    rows = x_ref.shape[0]
    xb = x_ref[...].astype(BF16)
    h = _dot(xb, w_in_ref[...])
    o_q = 2 * A_WIDTH
    o_kv = o_q + Q_LORA
    o_pe = o_kv + KV_LORA
    o_rot = o_pe + HEAD_PAD
    cos_p = cos_ref[...]
    sin_p = sin_ref[...]

    a_u = _gelu(h[:, :A_WIDTH])
    a_v = _gelu(h[:, A_WIDTH:o_q])
    vn = _layernorm(a_v, lng_ref[...], lnb_ref[...]).astype(BF16)
    t_idx = lax.broadcasted_iota(jnp.int32, (A_CHUNK, A_CHUNK), 0)
    s_idx = lax.broadcasted_iota(jnp.int32, (A_CHUNK, A_CHUNK), 1)
    causal = s_idx <= t_idx
    bs = bs_ref[...]
    for g in range(A_GROUPS):
        w_g = jnp.where(causal, ws_ref[g], 0.0).astype(BF16)
        cols = slice(g * A_GROUP_DIM, (g + 1) * A_GROUP_DIM)
        for c in range(rows // A_CHUNK):
            rws = slice(c * A_CHUNK, (c + 1) * A_CHUNK)
            mixed = _dot(w_g, vn[rws, cols]) + bs[:, cols]
            ya_ref[rws, cols] = (a_u[rws, cols] * mixed).astype(BF16)

    cqn = _rmsnorm(h[:, o_q:o_kv], qn_ref[...]).astype(BF16)
    q2 = _dot(cqn, wq_ref[...])
    half = MLA_HEADS * HEAD_PAD
    for hd in range(MLA_HEADS):
        cols = slice(hd * HEAD_PAD, (hd + 1) * HEAD_PAD)
        rot = slice(half + hd * HEAD_PAD, half + (hd + 1) * HEAD_PAD)
        q_ref[:, cols] = (q2[:, cols] * cos_p + q2[:, rot] * sin_p).astype(BF16)

    ckvn = _rmsnorm(h[:, o_kv:o_pe], kvn_ref[...]).astype(BF16)
    kv = _dot(ckvn, wkv_ref[...])
    kpe = h[:, o_pe:o_rot] * cos_p + h[:, o_rot:o_rot + HEAD_PAD] * sin_p
    for hd in range(MLA_HEADS):
        cols = slice(hd * HEAD_PAD, (hd + 1) * HEAD_PAD)
        k_ref[:, cols] = (kv[:, cols] + kpe).astype(BF16)
    v_ref[...] = kv[:, half:].astype(BF16)


def _mixer_in(x2d, cos_p, sin_p, w_in, gm_ln_g, gm_ln_b, w_s, b_s, q_norm, w_uq, kv_norm, w_ukv, *, rows):
    n, d = x2d.shape
    hp = MLA_HEADS * HEAD_PAD
    o_pe = 2 * A_WIDTH + Q_LORA + KV_LORA
    half = QK_ROPE // 2
    w_pe = w_in[:, o_pe:o_pe + QK_ROPE]
    w_pe_rot = jnp.concatenate([-w_pe[:, half:], w_pe[:, :half]], axis=1)
    pad_l = jnp.zeros((d, QK_NOPE), F32)
    pad_r = jnp.zeros((d, HEAD_PAD - QK_HEAD), F32)
    w_in_p = jnp.concatenate([w_in[:, :o_pe], pad_l, w_pe, pad_r, pad_l, w_pe_rot, pad_r], axis=1).astype(BF16)
    wq = w_uq.reshape(Q_LORA, MLA_HEADS, QK_HEAD)
    wq_pe = wq[:, :, QK_NOPE:]
    wq_rot = jnp.concatenate([jnp.zeros((Q_LORA, MLA_HEADS, QK_NOPE), F32), -wq_pe[:, :, half:], wq_pe[:, :, :half]],
                             axis=2)
    padq = ((0, 0), (0, 0), (0, HEAD_PAD - QK_HEAD))
    wq2 = jnp.concatenate([jnp.pad(wq, padq).reshape(Q_LORA, hp), jnp.pad(wq_rot, padq).reshape(Q_LORA, hp)],
                          axis=1).astype(BF16)
    wkv = w_ukv.reshape(KV_LORA, MLA_HEADS, QK_NOPE + V_HEAD)
    wk = jnp.pad(wkv[:, :, :QK_NOPE], ((0, 0), (0, 0), (0, HEAD_PAD - QK_NOPE))).reshape(KV_LORA, hp)
    wv = wkv[:, :, QK_NOPE:].reshape(KV_LORA, MLA_HEADS * V_HEAD)
    wkv2 = jnp.concatenate([wk, wv], axis=1).astype(BF16)
    bs_full = jnp.repeat(b_s.T, A_GROUP_DIM, axis=1)

    full = lambda a: pl.BlockSpec(a.shape, lambda i: (0,) * a.ndim)
    row = lambda w: pl.BlockSpec((rows, w), lambda i: (i, 0))
    args = (x2d, cos_p, sin_p, w_in_p, gm_ln_g.reshape(1, -1), gm_ln_b.reshape(1, -1), w_s, bs_full,
            q_norm.reshape(1, -1), wq2, kv_norm.reshape(1, -1), wkv2)
    in_specs = [row(d), row(HEAD_PAD), row(HEAD_PAD)] + [full(a) for a in args[3:]]
    return pl.pallas_call(
        _mixer_in_kernel,
        grid=(n // rows,),
        in_specs=in_specs,
        out_specs=(row(A_WIDTH), row(hp), row(hp), row(MLA_HEADS * V_HEAD)),
        out_shape=(jax.ShapeDtypeStruct((n, A_WIDTH), BF16), jax.ShapeDtypeStruct((n, hp), BF16),
                   jax.ShapeDtypeStruct((n, hp), BF16), jax.ShapeDtypeStruct((n, MLA_HEADS * V_HEAD), BF16)),
        compiler_params=_params(("parallel",)),
        name="mixer_in",
    )(*args)


def _attn_kernel(q_ref, k_ref, v_ref, o_ref, *, blk):
    seq = q_ref.shape[0]
    scale = QK_HEAD ** -0.5
    row = lax.broadcasted_iota(jnp.int32, (blk, blk), 0)
    col = lax.broadcasted_iota(jnp.int32, (blk, blk), 1)
    diag_mask = col <= row
    first_head_lanes = lax.broadcasted_iota(jnp.int32, (blk, 2 * V_HEAD), 1) < V_HEAD

    def q_body(qi, carry):
        q0 = pl.multiple_of(qi * blk, blk)
        outs = []
        for hh in range(2):
            cols = slice(hh * HEAD_PAD, (hh + 1) * HEAD_PAD)
            q = q_ref[pl.ds(q0, blk), cols]

            def kv_step(k0, st, masked):
                m, l, acc = st
                k = k_ref[pl.ds(k0, blk), cols]
                v = v_ref[pl.ds(k0, blk), :]
                s = lax.dot_general(q, k, (((1,), (1,)), ((), ())), preferred_element_type=F32) * scale
                if masked:
                    s = jnp.where(diag_mask, s, NEG_BIG)
                m_new = jnp.maximum(m, jnp.max(s, axis=-1, keepdims=True))
                a = jnp.exp(m - m_new)
                p = jnp.exp(s - m_new)
                l = a * l + jnp.sum(p, axis=-1, keepdims=True)
                acc = a * acc + _dot(p.astype(BF16), v)
                return m_new, l, acc

            st = (jnp.full((blk, 1), NEG_BIG, F32), jnp.zeros((blk, 1), F32), jnp.zeros((blk, 2 * V_HEAD), F32))
            st = lax.fori_loop(0, qi, lambda j, s_: kv_step(pl.multiple_of(j * blk, blk), s_, False), st)
            m, l, acc = kv_step(q0, st, True)
            outs.append(acc / l)
        o_ref[pl.ds(q0, blk), :] = jnp.where(first_head_lanes, outs[0], outs[1]).astype(BF16)
        return carry

    lax.fori_loop(0, seq // blk, q_body, 0)


def _attention(q, k, v, *, bsz, seq, blk):
    n = bsz * seq
    pair = 2 * HEAD_PAD
    return pl.pallas_call(
        functools.partial(_attn_kernel, blk=blk),
        grid=(bsz, MLA_HEADS // 2),
        in_specs=[pl.BlockSpec((seq, pair), lambda b, h: (b, h)), pl.BlockSpec((seq, pair), lambda b, h: (b, h)),
                  pl.BlockSpec((seq, 2 * V_HEAD), lambda b, h: (b, h))],
        out_specs=pl.BlockSpec((seq, 2 * V_HEAD), lambda b, h: (b, h)),
        out_shape=jax.ShapeDtypeStruct((n, MLA_HEADS * V_HEAD), BF16),
        compiler_params=_params(("parallel", "parallel")),
        name="attention",
    )(q, k, v)


def _proj_ln_kernel(*refs, n_in):
    ins = refs[:n_in]
    ws = refs[n_in:2 * n_in]
    x_ref, g_ref, b_ref, o_ref = refs[2 * n_in:]
    acc = DEEPNORM_ALPHA * x_ref[...]
    for a_ref, w_ref in zip(ins, ws):
        acc = acc + _dot(a_ref[...], w_ref[...])
    o_ref[...] = _layernorm(acc, g_ref[...], b_ref[...])


def _proj_ln(ins, ws, x2d, g, b, *, rows, name):
    n, d = x2d.shape
    full = lambda a: pl.BlockSpec(a.shape, lambda i: (0,) * a.ndim)
    row = lambda w: pl.BlockSpec((rows, w), lambda i: (i, 0))
    g2, b2 = g.reshape(1, -1), b.reshape(1, -1)
    return pl.pallas_call(
        functools.partial(_proj_ln_kernel, n_in=len(ins)),
        grid=(n // rows,),
        in_specs=[row(a.shape[1]) for a in ins] + [full(w) for w in ws] + [row(d), full(g2), full(b2)],
        out_specs=row(d),
        out_shape=jax.ShapeDtypeStruct((n, d), F32),
        compiler_params=_params(("parallel",)),
        name=name,
    )(*ins, *ws, x2d, g2, b2)


def _ffn_kernel(x_ref, comb_ref, wg_ref, wu_ref, wd_ref, g_ref, b_ref, o_ref, xb_ref, acc_ref, *, weighted):
    e = pl.program_id(1)
    f = pl.program_id(2)

    @pl.when((e == 0) & (f == 0))
    def _():
        xb_ref[...] = x_ref[...].astype(BF16)
        acc_ref[...] = jnp.zeros_like(acc_ref)

    xb = xb_ref[...]
    gate = _dot(xb, wg_ref[0])
    up = _dot(xb, wu_ref[0])
    hid = (gate * _sigmoid(gate) * up).astype(BF16)
    contrib = _dot(hid, wd_ref[0])
    if weighted:
        comb = comb_ref[...]
        sel = lax.broadcasted_iota(jnp.int32, comb.shape, 1) == e
        contrib = contrib * jnp.sum(jnp.where(sel, comb, 0.0), axis=-1, keepdims=True)
    acc_ref[...] += contrib

    @pl.when((e == pl.num_programs(1) - 1) & (f == pl.num_programs(2) - 1))
    def _():
        o_ref[...] = _layernorm(DEEPNORM_ALPHA * x_ref[...] + acc_ref[...], g_ref[...], b_ref[...])


def _ffn(x2d, comb, w_gate, w_up, w_down, g, b, *, rows, fchunk, weighted, name):
    n, d = x2d.shape
    n_exp, _, dff = w_gate.shape
    g2, b2 = g.reshape(1, -1), b.reshape(1, -1)
    return pl.pallas_call(
        functools.partial(_ffn_kernel, weighted=weighted),
        grid=(n // rows, n_exp, dff // fchunk),
        in_specs=[pl.BlockSpec((rows, d), lambda i, e, f: (i, 0)),
                  pl.BlockSpec((rows, comb.shape[1]), lambda i, e, f: (i, 0)),
                  pl.BlockSpec((1, d, fchunk), lambda i, e, f: (e, 0, f)),
                  pl.BlockSpec((1, d, fchunk), lambda i, e, f: (e, 0, f)),
                  pl.BlockSpec((1, fchunk, d), lambda i, e, f: (e, f, 0)),
                  pl.BlockSpec((1, d), lambda i, e, f: (0, 0)),
                  pl.BlockSpec((1, d), lambda i, e, f: (0, 0))],
        out_specs=pl.BlockSpec((rows, d), lambda i, e, f: (i, 0)),
        out_shape=jax.ShapeDtypeStruct((n, d), F32),
        scratch_shapes=[pltpu.VMEM((rows, d), BF16), pltpu.VMEM((rows, d), F32)],
        compiler_params=_params(("parallel", "arbitrary", "arbitrary")),
        name=name,
    )(x2d, comb, w_gate, w_up, w_down, g2, b2)


def _matmul_kernel(x_ref, w_ref, o_ref):
    o_ref[...] = _dot(x_ref[...].astype(BF16), w_ref[...])


def _matmul(x2d, w, *, rows, name):
    n, d = x2d.shape
    return pl.pallas_call(
        _matmul_kernel,
        grid=(n // rows,),
        in_specs=[pl.BlockSpec((rows, d), lambda i: (i, 0)), pl.BlockSpec(w.shape, lambda i: (0, 0))],
        out_specs=pl.BlockSpec((rows, w.shape[1]), lambda i: (i, 0)),
        out_shape=jax.ShapeDtypeStruct((n, w.shape[1]), F32),
        compiler_params=_params(("parallel",)),
        name=name,
    )(x2d, w)


def _s5_kernel(u_ref, tw_ref, v_ref, lam_ref, y_ref, sre_ref, sim_ref, st_ref, *, bsz):
    lc = u_ref.shape[2]
    n_chunks = u_ref.shape[1] // bsz
    big = _dot(u_ref[0], tw_ref[0])
    y_ref[0] = big[:, :lc]
    sre_ref[...] = big[:, lc:lc + LANES]
    sim_ref[...] = big[:, lc + LANES:]
    a_re = lam_ref[0, 0:1, :]
    a_im = lam_ref[0, 1:2, :]

    def step(j, st):
        re, im = st
        r0 = pl.multiple_of(j * bsz, bsz)
        st_ref[pl.ds(r0, bsz), :LANES] = re.astype(BF16)
        st_ref[pl.ds(r0, bsz), LANES:] = im.astype(BF16)
        new_re = a_re * re - a_im * im + sre_ref[pl.ds(r0, bsz), :]
        new_im = a_re * im + a_im * re + sim_ref[pl.ds(r0, bsz), :]
        return new_re, new_im

    zero = jnp.zeros((bsz, LANES), F32)
    lax.fori_loop(0, n_chunks, step, (zero, zero))
    y_ref[0] += _dot(st_ref[...], v_ref[0])


def _s5_scan(uc, tw, vv, lam, *, bsz):
    n_grp, rows, lc = uc.shape
    return pl.pallas_call(
        functools.partial(_s5_kernel, bsz=bsz),
        grid=(n_grp,),
        in_specs=[pl.BlockSpec((1, rows, lc), lambda g: (g, 0, 0)),
                  pl.BlockSpec((1,) + tw.shape[1:], lambda g: (g, 0, 0)),
                  pl.BlockSpec((1,) + vv.shape[1:], lambda g: (g, 0, 0)),
                  pl.BlockSpec((1,) + lam.shape[1:], lambda g: (g, 0, 0))],
        out_specs=pl.BlockSpec((1, rows, lc), lambda g: (g, 0, 0)),
        out_shape=jax.ShapeDtypeStruct((n_grp, rows, lc), F32),
        scratch_shapes=[pltpu.VMEM((rows, LANES), F32), pltpu.VMEM((rows, LANES), F32),
                        pltpu.VMEM((rows, 2 * LANES), BF16)],
        compiler_params=_params(("parallel",)),
        name="s5_scan",
    )(uc, tw, vv, lam)


def _s5_weights(a_re, a_im, log_step, b_re, b_im, c_re, c_im):
    lch = S5_CHUNK
    n_grp, n_st = a_re.shape
    delta = jnp.exp(log_step)[:, None]
    mag = jnp.exp(delta * a_re)
    abar_re = mag * jnp.cos(delta * a_im)
    abar_im = mag * jnp.sin(delta * a_im)
    den = a_re * a_re + a_im * a_im
    coef_re = ((abar_re - 1.0) * a_re + abar_im * a_im) / den
    coef_im = (abar_im * a_re - (abar_re - 1.0) * a_im) / den
    bb_re = coef_re[..., None] * b_re - coef_im[..., None] * b_im
    bb_im = coef_re[..., None] * b_im + coef_im[..., None] * b_re
    kk = jnp.arange(lch + 1, dtype=F32)[:, None, None]
    pmag = jnp.exp(kk * (delta * a_re)[None])
    pw_re = pmag * jnp.cos(kk * (delta * a_im)[None])
    pw_im = pmag * jnp.sin(kk * (delta * a_im)[None])
    cl_re = c_re[None] * pw_re[:lch, :, None, :] - c_im[None] * pw_im[:lch, :, None, :]
    cl_im = c_re[None] * pw_im[:lch, :, None, :] + c_im[None] * pw_re[:lch, :, None, :]
    taps = (jnp.sum(cl_re[..., None] * bb_re[None, :, None], axis=3)
            - jnp.sum(cl_im[..., None] * bb_im[None, :, None], axis=3))
    s_i = jnp.arange(lch)[:, None]
    t_i = jnp.arange(lch)[None, :]
    lag = jnp.clip(t_i - s_i, 0, lch - 1)
    toep = jnp.where((t_i >= s_i)[None, :, :, None, None], jnp.transpose(taps, (1, 0, 2, 3))[:, lag], 0.0)
    toep = jnp.transpose(toep, (0, 1, 4, 2, 3)).reshape(n_grp, lch * S5_GROUP_DIM, lch * S5_GROUP_DIM)
    rev_re = pw_re[:lch][::-1]
    rev_im = pw_im[:lch][::-1]
    w_re = rev_re[..., None] * bb_re[None] - rev_im[..., None] * bb_im[None]
    w_im = rev_re[..., None] * bb_im[None] + rev_im[..., None] * bb_re[None]
    lay_w = lambda w: jnp.pad(jnp.transpose(w, (1, 0, 3, 2)).reshape(n_grp, lch * S5_GROUP_DIM, n_st),
                              ((0, 0), (0, 0), (0, LANES - n_st)))
    tw = jnp.concatenate([toep, lay_w(w_re), lay_w(w_im)], axis=2).astype(BF16)
    nx_re = pw_re[1:]
    nx_im = pw_im[1:]
    v_re = c_re[None] * nx_re[:, :, None, :] - c_im[None] * nx_im[:, :, None, :]
    v_im = c_re[None] * nx_im[:, :, None, :] + c_im[None] * nx_re[:, :, None, :]
    lay_v = lambda w: jnp.pad(jnp.transpose(w, (1, 3, 0, 2)).reshape(n_grp, n_st, lch * S5_GROUP_DIM),
                              ((0, 0), (0, LANES - n_st), (0, 0)))
    vv = jnp.concatenate([lay_v(v_re), -lay_v(v_im)], axis=1).astype(BF16)
    lam = jnp.zeros((n_grp, 8, LANES), F32)
    lam = lam.at[:, 0, :n_st].set(pw_re[lch]).at[:, 1, :n_st].set(pw_im[lch])
    return tw, vv, lam


def _s5_out_kernel(y_ref, u_ref, d_ref, gw_ref, gb_ref, wo_ref, x_ref, g_ref, b_ref, r_ref, o_ref, comb_ref):
    y = y_ref[...] + d_ref[...] * u_ref[...]
    gl = _gelu(y)
    z = gl * _sigmoid(_dot(gl.astype(BF16), gw_ref[...]) + gb_ref[...])
    mix = _dot(z.astype(BF16), wo_ref[...])
    xo = _layernorm(DEEPNORM_ALPHA * x_ref[...] + mix, g_ref[...], b_ref[...])
    o_ref[...] = xo
    logits = jnp.dot(xo, r_ref[...], preferred_element_type=F32, precision=lax.Precision.HIGHEST)
    idx = lax.broadcasted_iota(jnp.int32, logits.shape, 1)
    m1 = jnp.max(logits, axis=-1, keepdims=True)
    i1 = jnp.min(jnp.where(logits == m1, idx, N_EXPERTS), axis=-1, keepdims=True)
    rest = jnp.where(idx == i1, -jnp.inf, logits)
    m2 = jnp.max(rest, axis=-1, keepdims=True)
    i2 = jnp.min(jnp.where(rest == m2, idx, N_EXPERTS), axis=-1, keepdims=True)
    e2 = jnp.exp(m2 - m1)
    g1 = 1.0 / (1.0 + e2)
    g2 = e2 * g1
    comb_ref[...] = jnp.where(idx == i1, g1, 0.0) + jnp.where(idx == i2, g2, 0.0)


def _s5_out(y2d, u2d, d_skip, glu_w, glu_b, w_out, x2d, g, b, router, *, rows):
    n, d = x2d.shape
    full = lambda a: pl.BlockSpec(a.shape, lambda i: (0,) * a.ndim)
    row = lambda w: pl.BlockSpec((rows, w), lambda i: (i, 0))
    args = (y2d, u2d, d_skip.reshape(1, -1), glu_w.astype(BF16), glu_b.reshape(1, -1), w_out.astype(BF16), x2d,
            g.reshape(1, -1), b.reshape(1, -1), router)
    in_specs = [row(d), row(d), full(args[2]), full(args[3]), full(args[4]), full(args[5]), row(d), full(args[7]),
                full(args[8]), full(args[9])]
    return pl.pallas_call(
        _s5_out_kernel,
        grid=(n // rows,),
        in_specs=in_specs,
        out_specs=(row(d), row(N_EXPERTS)),
        out_shape=(jax.ShapeDtypeStruct((n, d), F32), jax.ShapeDtypeStruct((n, N_EXPERTS), F32)),
        compiler_params=_params(("parallel",)),
        name="s5_out",
    )(*args)


def _layer_even(x2d, positions, bsz, seq, w_in, gm_ln_g, gm_ln_b, w_s, b_s, q_norm, w_uq, kv_norm, w_ukv, w_out,
                ln_g, ln_b, f_gate, f_up, f_down, f_ln_g, f_ln_b, *, rows, ffn_rows, fchunk, attn_blk):
    cos_p, sin_p = _rope_tables(positions)
    ya, q, k, v = _mixer_in(x2d, cos_p, sin_p, w_in, gm_ln_g, gm_ln_b, w_s, b_s, q_norm, w_uq, kv_norm, w_ukv,
                            rows=rows)
    yb = _attention(q, k, v, bsz=bsz, seq=seq, blk=attn_blk)
    wo = w_out.astype(BF16)
    x2d = _proj_ln([ya, yb], [wo[:A_WIDTH], wo[A_WIDTH:]], x2d, ln_g, ln_b, rows=rows, name="mixer_out")
    ones = jnp.ones((x2d.shape[0], 1), F32)
    return _ffn(x2d, ones, f_gate.astype(BF16)[None], f_up.astype(BF16)[None], f_down.astype(BF16)[None], f_ln_g,
                f_ln_b, rows=ffn_rows, fchunk=fchunk, weighted=False, name="ffn_dense")


def _layer_odd(x2d, bsz, seq, w_in, a_re, a_im, log_step, b_re, b_im, c_re, c_im, d_skip, glu_w, glu_b, w_out, ln_g,
               ln_b, router, m_gate, m_up, m_down, m_ln_g, m_ln_b, *, rows, ffn_rows, fchunk):
    n, d = x2d.shape
    n_grp = d // S5_GROUP_DIM
    lch = S5_CHUNK
    n_chunks = seq // lch
    u2d = _matmul(x2d, w_in.astype(BF16), rows=rows, name="s5_in")
    tw, vv, lam = _s5_weights(a_re, a_im, log_step, b_re, b_im, c_re, c_im)
    uc = u2d.reshape(bsz, n_chunks, lch, n_grp, S5_GROUP_DIM).transpose(3, 1, 0, 2, 4)
    uc = uc.reshape(n_grp, n_chunks * bsz, lch * S5_GROUP_DIM).astype(BF16)
    yc = _s5_scan(uc, tw, vv, lam, bsz=bsz)
    y2d = yc.reshape(n_grp, n_chunks, bsz, lch, S5_GROUP_DIM).transpose(2, 1, 3, 0, 4).reshape(n, d)
    x2d, comb = _s5_out(y2d, u2d, d_skip, glu_w, glu_b, w_out, x2d, ln_g, ln_b, router, rows=rows)
    return _ffn(x2d, comb, m_gate.astype(BF16), m_up.astype(BF16), m_down.astype(BF16), m_ln_g, m_ln_b,
                rows=ffn_rows, fchunk=fchunk, weighted=True, name="ffn_moe")


def kernel(x, positions, ab_w_in, gm_ln_g, gm_ln_b, gm_w_s, gm_b_s, mla_q_norm, mla_w_uq, mla_kv_norm, mla_w_ukv, ab_w_out, ab_ln_g, ab_ln_b, ffd_w_gate, ffd_w_up, ffd_w_down, ffd_ln_g, ffd_ln_b, c_w_in, s5_a_re, s5_a_im, s5_log_step, s5_b_re, s5_b_im, s5_c_re, s5_c_im, s5_d, glu_w, glu_b, c_w_out, c_ln_g, c_ln_b, moe_router, moe_w_gate, moe_w_up, moe_w_down, moe_ln_g, moe_ln_b):
    bsz, seq, d = x.shape
    x2d = x.reshape(bsz * seq, d)
    rows = min(512, seq)
    for i in range(DEPTH):
        j = i // 2
        if i % 2 == 0:
            x2d = _layer_even(x2d, positions, bsz, seq, ab_w_in[j], gm_ln_g[j], gm_ln_b[j], gm_w_s[j], gm_b_s[j],
                              mla_q_norm[j], mla_w_uq[j], mla_kv_norm[j], mla_w_ukv[j], ab_w_out[j], ab_ln_g[j],
                              ab_ln_b[j], ffd_w_gate[j], ffd_w_up[j], ffd_w_down[j], ffd_ln_g[j], ffd_ln_b[j],
                              rows=rows, ffn_rows=rows, fchunk=ffd_w_gate.shape[2] // 2, attn_blk=min(256, seq))
        else:
            x2d = _layer_odd(x2d, bsz, seq, c_w_in[j], s5_a_re[j], s5_a_im[j], s5_log_step[j], s5_b_re[j],
                             s5_b_im[j], s5_c_re[j], s5_c_im[j], s5_d[j], glu_w[j], glu_b[j], c_w_out[j], c_ln_g[j],
                             c_ln_b[j], moe_router[j], moe_w_gate[j], moe_w_up[j], moe_w_down[j], moe_ln_g[j],
                             moe_ln_b[j], rows=rows, ffn_rows=rows, fchunk=moe_w_gate.shape[3] // 4)
    return x2d.reshape(bsz, seq, d)
```

```python
import functools
import math

import jax
import jax.numpy as jnp
from jax import lax
from jax.experimental import pallas as pl
from jax.experimental.pallas import tpu as pltpu

F32 = jnp.float32
BF16 = jnp.bfloat16

A_GROUPS = 4
A_GROUP_DIM = 128
A_WIDTH = A_GROUPS * A_GROUP_DIM
A_CHUNK = 128
MLA_HEADS = 8
QK_NOPE = 64
QK_ROPE = 32
QK_HEAD = QK_NOPE + QK_ROPE
V_HEAD = 64
Q_LORA = 384
KV_LORA = 256
ROPE_THETA = 10000.0
S5_GROUP_DIM = 16
S5_STATE = 64
N_EXPERTS = 8
LN_EPS = 1e-5
RMS_EPS = 1e-6
DEPTH = 2
DEEPNORM_ALPHA = (2.0 * DEPTH) ** 0.25

LANES = 128
HEAD_PAD = LANES
S5_CHUNK = 16
VMEM_LIMIT = 56 * 1024 * 1024
NEG_BIG = -1e30


def _params(sem):
    return pltpu.CompilerParams(dimension_semantics=sem, vmem_limit_bytes=VMEM_LIMIT)


def _gelu(x):
    c = math.sqrt(2.0 / math.pi)
    return 0.5 * x * (1.0 + jnp.tanh(c * (x + 0.044715 * (x * x * x))))


def _sigmoid(x):
    return 1.0 / (1.0 + jnp.exp(-x))


def _layernorm(x, g, b):
    mu = jnp.mean(x, axis=-1, keepdims=True)
    xc = x - mu
    var = jnp.mean(xc * xc, axis=-1, keepdims=True)
    return xc * lax.rsqrt(var + LN_EPS) * g + b


def _rmsnorm(x, g):
    ms = jnp.mean(x * x, axis=-1, keepdims=True)
    return x * lax.rsqrt(ms + RMS_EPS) * g


def _dot(a, b):
    return jnp.dot(a, b, preferred_element_type=F32)


def _rope_table_kernel(inv_ref, pos_ref, cos_ref, sin_ref):
    pos = pos_ref[...].astype(F32)
    for j in range(QK_ROPE // 2):
        ang = pos * inv_ref[j]
        cos_ref[j] = jnp.cos(ang)
        sin_ref[j] = jnp.sin(ang)


def _rope_tables(positions):
    n = positions.size
    half = QK_ROPE // 2
    inv_freq = 1.0 / (ROPE_THETA ** (jnp.arange(0, QK_ROPE, 2, dtype=F32) / QK_ROPE))
    pos2d = positions.reshape(n // LANES, LANES)
    cos_t, sin_t = pl.pallas_call(
        _rope_table_kernel,
        out_shape=(jax.ShapeDtypeStruct((half, n // LANES, LANES), F32),) * 2,
        in_specs=[pl.BlockSpec(memory_space=pltpu.SMEM), pl.BlockSpec(memory_space=pltpu.VMEM)],
        out_specs=(pl.BlockSpec(memory_space=pltpu.VMEM),) * 2,
        name="rope_table",
    )(inv_freq, pos2d)
    cos_c = cos_t.reshape(half, n).T
    sin_c = sin_t.reshape(half, n).T
    ones = jnp.ones((n, QK_NOPE), F32)
    zeros_n = jnp.zeros((n, QK_NOPE), F32)
    zeros_p = jnp.zeros((n, HEAD_PAD - QK_HEAD), F32)
    cos_p = jnp.concatenate([ones, cos_c, cos_c, zeros_p], axis=1)
    sin_p = jnp.concatenate([zeros_n, sin_c, sin_c, zeros_p], axis=1)
    return cos_p, sin_p


def _mixer_in_kernel(x_ref, cos_ref, sin_ref, w_in_ref, lng_ref, lnb_ref, ws_ref, bs_ref, qn_ref, wq_ref,
                     kvn_ref, wkv_ref, ya_ref, q_ref, k_ref, v_ref):
    rows = x_ref.shape[0]
    xb = x_ref[...].astype(BF16)
    h = _dot(xb, w_in_ref[...])
    o_q = 2 * A_WIDTH
    o_kv = o_q + Q_LORA
    o_pe = o_kv + KV_LORA
    o_rot = o_pe + HEAD_PAD
    cos_p = cos_ref[...]
    sin_p = sin_ref[...]

    a_u = _gelu(h[:, :A_WIDTH])
    a_v = _gelu(h[:, A_WIDTH:o_q])
    vn = _layernorm(a_v, lng_ref[...], lnb_ref[...]).astype(BF16)
    t_idx = lax.broadcasted_iota(jnp.int32, (A_CHUNK, A_CHUNK), 0)
    s_idx = lax.broadcasted_iota(jnp.int32, (A_CHUNK, A_CHUNK), 1)
    causal = s_idx <= t_idx
    bs = bs_ref[...]
    for g in range(A_GROUPS):
        w_g = jnp.where(causal, ws_ref[g], 0.0).astype(BF16)
        cols = slice(g * A_GROUP_DIM, (g + 1) * A_GROUP_DIM)
        for c in range(rows // A_CHUNK):
            rws = slice(c * A_CHUNK, (c + 1) * A_CHUNK)
            mixed = _dot(w_g, vn[rws, cols]) + bs[:, cols]
            ya_ref[rws, cols] = (a_u[rws, cols] * mixed).astype(BF16)

    cqn = _rmsnorm(h[:, o_q:o_kv], qn_ref[...]).astype(BF16)
    q2 = _dot(cqn, wq_ref[...])
    half = MLA_HEADS * HEAD_PAD
    for hd in range(MLA_HEADS):
        cols = slice(hd * HEAD_PAD, (hd + 1) * HEAD_PAD)
        rot = slice(half + hd * HEAD_PAD, half + (hd + 1) * HEAD_PAD)
        q_ref[:, cols] = (q2[:, cols] * cos_p + q2[:, rot] * sin_p).astype(BF16)

    ckvn = _rmsnorm(h[:, o_kv:o_pe], kvn_ref[...]).astype(BF16)
    kv = _dot(ckvn, wkv_ref[...])
    kpe = h[:, o_pe:o_rot] * cos_p + h[:, o_rot:o_rot + HEAD_PAD] * sin_p
    for hd in range(MLA_HEADS):
        cols = slice(hd * HEAD_PAD, (hd + 1) * HEAD_PAD)
        k_ref[:, cols] = (kv[:, cols] + kpe).astype(BF16)
    v_ref[...] = kv[:, half:].astype(BF16)


def _mixer_in(x2d, cos_p, sin_p, w_in, gm_ln_g, gm_ln_b, w_s, b_s, q_norm, w_uq, kv_norm, w_ukv, *, rows):
    n, d = x2d.shape
    hp = MLA_HEADS * HEAD_PAD
    o_pe = 2 * A_WIDTH + Q_LORA + KV_LORA
    half = QK_ROPE // 2
    w_pe = w_in[:, o_pe:o_pe + QK_ROPE]
    w_pe_rot = jnp.concatenate([-w_pe[:, half:], w_pe[:, :half]], axis=1)
    pad_l = jnp.zeros((d, QK_NOPE), F32)
    pad_r = jnp.zeros((d, HEAD_PAD - QK_HEAD), F32)
    w_in_p = jnp.concatenate([w_in[:, :o_pe], pad_l, w_pe, pad_r, pad_l, w_pe_rot, pad_r], axis=1).astype(BF16)
    wq = w_uq.reshape(Q_LORA, MLA_HEADS, QK_HEAD)
    wq_pe = wq[:, :, QK_NOPE:]
    wq_rot = jnp.concatenate([jnp.zeros((Q_LORA, MLA_HEADS, QK_NOPE), F32), -wq_pe[:, :, half:], wq_pe[:, :, :half]],
                             axis=2)
    padq = ((0, 0), (0, 0), (0, HEAD_PAD - QK_HEAD))
    wq2 = jnp.concatenate([jnp.pad(wq, padq).reshape(Q_LORA, hp), jnp.pad(wq_rot, padq).reshape(Q_LORA, hp)],
                          axis=1).astype(BF16)
    wkv = w_ukv.reshape(KV_LORA, MLA_HEADS, QK_NOPE + V_HEAD)
    wk = jnp.pad(wkv[:, :, :QK_NOPE], ((0, 0), (0, 0), (0, HEAD_PAD - QK_NOPE))).reshape(KV_LORA, hp)
    wv = wkv[:, :, QK_NOPE:].reshape(KV_LORA, MLA_HEADS * V_HEAD)
    wkv2 = jnp.concatenate([wk, wv], axis=1).astype(BF16)
    bs_full = jnp.repeat(b_s.T, A_GROUP_DIM, axis=1)

    full = lambda a: pl.BlockSpec(a.shape, lambda i: (0,) * a.ndim)
    row = lambda w: pl.BlockSpec((rows, w), lambda i: (i, 0))
    args = (x2d, cos_p, sin_p, w_in_p, gm_ln_g.reshape(1, -1), gm_ln_b.reshape(1, -1), w_s, bs_full,
            q_norm.reshape(1, -1), wq2, kv_norm.reshape(1, -1), wkv2)
    in_specs = [row(d), row(HEAD_PAD), row(HEAD_PAD)] + [full(a) for a in args[3:]]
    return pl.pallas_call(
        _mixer_in_kernel,
        grid=(n // rows,),
        in_specs=in_specs,
        out_specs=(row(A_WIDTH), row(hp), row(hp), row(MLA_HEADS * V_HEAD)),
        out_shape=(jax.ShapeDtypeStruct((n, A_WIDTH), BF16), jax.ShapeDtypeStruct((n, hp), BF16),
                   jax.ShapeDtypeStruct((n, hp), BF16), jax.ShapeDtypeStruct((n, MLA_HEADS * V_HEAD), BF16)),
        compiler_params=_params(("parallel",)),
        name="mixer_in",
    )(*args)


def _attn_kernel(q_ref, k_ref, v_ref, o_ref, *, blk):
    seq = q_ref.shape[0]
    scale = QK_HEAD ** -0.5
    row = lax.broadcasted_iota(jnp.int32, (blk, blk), 0)
    col = lax.broadcasted_iota(jnp.int32, (blk, blk), 1)
    diag_mask = col <= row
    first_head_lanes = lax.broadcasted_iota(jnp.int32, (blk, 2 * V_HEAD), 1) < V_HEAD

    def q_body(qi, carry):
        q0 = pl.multiple_of(qi * blk, blk)
        outs = []
        for hh in range(2):
            cols = slice(hh * HEAD_PAD, (hh + 1) * HEAD_PAD)
            q = q_ref[pl.ds(q0, blk), cols]

            def kv_step(k0, st, masked):
                m, l, acc = st
                k = k_ref[pl.ds(k0, blk), cols]
                v = v_ref[pl.ds(k0, blk), :]
                s = lax.dot_general(q, k, (((1,), (1,)), ((), ())), preferred_element_type=F32) * scale
                if masked:
                    s = jnp.where(diag_mask, s, NEG_BIG)
                m_new = jnp.maximum(m, jnp.max(s, axis=-1, keepdims=True))
                a = jnp.exp(m - m_new)
                p = jnp.exp(s - m_new)
                l = a * l + jnp.sum(p, axis=-1, keepdims=True)
                acc = a * acc + _dot(p.astype(BF16), v)
                return m_new, l, acc

            st = (jnp.full((blk, 1), NEG_BIG, F32), jnp.zeros((blk, 1), F32), jnp.zeros((blk, 2 * V_HEAD), F32))
            st = lax.fori_loop(0, qi, lambda j, s_: kv_step(pl.multiple_of(j * blk, blk), s_, False), st)
            m, l, acc = kv_step(q0, st, True)
            outs.append(acc / l)
        o_ref[pl.ds(q0, blk), :] = jnp.where(first_head_lanes, outs[0], outs[1]).astype(BF16)
        return carry

    lax.fori_loop(0, seq // blk, q_body, 0)


def _attention(q, k, v, *, bsz, seq, blk):
    n = bsz * seq
    pair = 2 * HEAD_PAD
    return pl.pallas_call(
        functools.partial(_attn_kernel, blk=blk),
        grid=(bsz, MLA_HEADS // 2),
        in_specs=[pl.BlockSpec((seq, pair), lambda b, h: (b, h)), pl.BlockSpec((seq, pair), lambda b, h: (b, h)),
                  pl.BlockSpec((seq, 2 * V_HEAD), lambda b, h: (b, h))],
        out_specs=pl.BlockSpec((seq, 2 * V_HEAD), lambda b, h: (b, h)),
        out_shape=jax.ShapeDtypeStruct((n, MLA_HEADS * V_HEAD), BF16),
        compiler_params=_params(("parallel", "parallel")),
        name="attention",
    )(q, k, v)


def _proj_ln_kernel(*refs, n_in):
    ins = refs[:n_in]
    ws = refs[n_in:2 * n_in]
    x_ref, g_ref, b_ref, o_ref = refs[2 * n_in:]
    acc = DEEPNORM_ALPHA * x_ref[...]
    for a_ref, w_ref in zip(ins, ws):
        acc = acc + _dot(a_ref[...], w_ref[...])
    o_ref[...] = _layernorm(acc, g_ref[...], b_ref[...])


def _proj_ln(ins, ws, x2d, g, b, *, rows, name):
    n, d = x2d.shape
    full = lambda a: pl.BlockSpec(a.shape, lambda i: (0,) * a.ndim)
    row = lambda w: pl.BlockSpec((rows, w), lambda i: (i, 0))
    g2, b2 = g.reshape(1, -1), b.reshape(1, -1)
    return pl.pallas_call(
        functools.partial(_proj_ln_kernel, n_in=len(ins)),
        grid=(n // rows,),
        in_specs=[row(a.shape[1]) for a in ins] + [full(w) for w in ws] + [row(d), full(g2), full(b2)],
        out_specs=row(d),
        out_shape=jax.ShapeDtypeStruct((n, d), F32),
        compiler_params=_params(("parallel",)),
        name=name,
    )(*ins, *ws, x2d, g2, b2)


def _ffn_kernel(x_ref, wg_ref, wu_ref, wd_ref, g_ref, b_ref, o_ref, xb_ref, acc_ref):
    f = pl.program_id(1)

    @pl.when(f == 0)
    def _():
        xb_ref[...] = x_ref[...].astype(BF16)
        acc_ref[...] = jnp.zeros_like(acc_ref)

    xb = xb_ref[...]
    gate = _dot(xb, wg_ref[...])
    up = _dot(xb, wu_ref[...])
    hid = (gate * _sigmoid(gate) * up).astype(BF16)
    acc_ref[...] += _dot(hid, wd_ref[...])

    @pl.when(f == pl.num_programs(1) - 1)
    def _():
        o_ref[...] = _layernorm(DEEPNORM_ALPHA * x_ref[...] + acc_ref[...], g_ref[...], b_ref[...])


def _ffn(x2d, w_gate, w_up, w_down, g, b, *, rows, fchunk):
    n, d = x2d.shape
    dff = w_gate.shape[1]
    g2, b2 = g.reshape(1, -1), b.reshape(1, -1)
    return pl.pallas_call(
        _ffn_kernel,
        grid=(n // rows, dff // fchunk),
        in_specs=[pl.BlockSpec((rows, d), lambda i, f: (i, 0)),
                  pl.BlockSpec((d, fchunk), lambda i, f: (0, f)),
                  pl.BlockSpec((d, fchunk), lambda i, f: (0, f)),
                  pl.BlockSpec((fchunk, d), lambda i, f: (f, 0)),
                  pl.BlockSpec((1, d), lambda i, f: (0, 0)),
                  pl.BlockSpec((1, d), lambda i, f: (0, 0))],
        out_specs=pl.BlockSpec((rows, d), lambda i, f: (i, 0)),
        out_shape=jax.ShapeDtypeStruct((n, d), F32),
        scratch_shapes=[pltpu.VMEM((rows, d), BF16), pltpu.VMEM((rows, d), F32)],
        compiler_params=_params(("parallel", "arbitrary")),
        name="ffn_dense",
    )(x2d, w_gate, w_up, w_down, g2, b2)


def _matmul_kernel(x_ref, w_ref, o_ref):
    o_ref[...] = _dot(x_ref[...].astype(BF16), w_ref[...])


def _matmul(x2d, w, *, rows, name):
    n, d = x2d.shape
    return pl.pallas_call(
        _matmul_kernel,
        grid=(n // rows,),
        in_specs=[pl.BlockSpec((rows, d), lambda i: (i, 0)), pl.BlockSpec(w.shape, lambda i: (0, 0))],
        out_specs=pl.BlockSpec((rows, w.shape[1]), lambda i: (i, 0)),
        out_shape=jax.ShapeDtypeStruct((n, w.shape[1]), F32),
        compiler_params=_params(("parallel",)),
        name=name,
    )(x2d, w)


def _s5_kernel(u_ref, tw_ref, v_ref, lam_ref, y_ref, sre_ref, sim_ref, st_ref, *, bsz):
    lc = u_ref.shape[2]
    n_chunks = u_ref.shape[1] // bsz
    big = _dot(u_ref[0], tw_ref[0])
    y_ref[0] = big[:, :lc]
    sre_ref[...] = big[:, lc:lc + LANES]
    sim_ref[...] = big[:, lc + LANES:]
    a_re = lam_ref[0, 0:1, :]
    a_im = lam_ref[0, 1:2, :]

    def step(j, st):
        re, im = st
        r0 = pl.multiple_of(j * bsz, bsz)
        st_ref[pl.ds(r0, bsz), :LANES] = re.astype(BF16)
        st_ref[pl.ds(r0, bsz), LANES:] = im.astype(BF16)
        new_re = a_re * re - a_im * im + sre_ref[pl.ds(r0, bsz), :]
        new_im = a_re * im + a_im * re + sim_ref[pl.ds(r0, bsz), :]
        return new_re, new_im

    zero = jnp.zeros((bsz, LANES), F32)
    lax.fori_loop(0, n_chunks, step, (zero, zero))
    y_ref[0] += _dot(st_ref[...], v_ref[0])


def _s5_scan(uc, tw, vv, lam, *, bsz):
    n_grp, rows, lc = uc.shape
    return pl.pallas_call(
        functools.partial(_s5_kernel, bsz=bsz),
        grid=(n_grp,),
        in_specs=[pl.BlockSpec((1, rows, lc), lambda g: (g, 0, 0)),
                  pl.BlockSpec((1,) + tw.shape[1:], lambda g: (g, 0, 0)),
                  pl.BlockSpec((1,) + vv.shape[1:], lambda g: (g, 0, 0)),
                  pl.BlockSpec((1,) + lam.shape[1:], lambda g: (g, 0, 0))],
        out_specs=pl.BlockSpec((1, rows, lc), lambda g: (g, 0, 0)),
        out_shape=jax.ShapeDtypeStruct((n_grp, rows, lc), F32),
        scratch_shapes=[pltpu.VMEM((rows, LANES), F32), pltpu.VMEM((rows, LANES), F32),
                        pltpu.VMEM((rows, 2 * LANES), BF16)],
        compiler_params=_params(("parallel",)),
        name="s5_scan",
    )(uc, tw, vv, lam)


def _s5_weights(a_re, a_im, log_step, b_re, b_im, c_re, c_im):
    lch = S5_CHUNK
    n_grp, n_st = a_re.shape
    delta = jnp.exp(log_step)[:, None]
    mag = jnp.exp(delta * a_re)
    abar_re = mag * jnp.cos(delta * a_im)
    abar_im = mag * jnp.sin(delta * a_im)
    den = a_re * a_re + a_im * a_im
    coef_re = ((abar_re - 1.0) * a_re + abar_im * a_im) / den
    coef_im = (abar_im * a_re - (abar_re - 1.0) * a_im) / den
    bb_re = coef_re[..., None] * b_re - coef_im[..., None] * b_im
    bb_im = coef_re[..., None] * b_im + coef_im[..., None] * b_re
    kk = jnp.arange(lch + 1, dtype=F32)[:, None, None]
    pmag = jnp.exp(kk * (delta * a_re)[None])
    pw_re = pmag * jnp.cos(kk * (delta * a_im)[None])
    pw_im = pmag * jnp.sin(kk * (delta * a_im)[None])
    cl_re = c_re[None] * pw_re[:lch, :, None, :] - c_im[None] * pw_im[:lch, :, None, :]
    cl_im = c_re[None] * pw_im[:lch, :, None, :] + c_im[None] * pw_re[:lch, :, None, :]
    taps = (jnp.sum(cl_re[..., None] * bb_re[None, :, None], axis=3)
            - jnp.sum(cl_im[..., None] * bb_im[None, :, None], axis=3))
    s_i = jnp.arange(lch)[:, None]
    t_i = jnp.arange(lch)[None, :]
    lag = jnp.clip(t_i - s_i, 0, lch - 1)
    toep = jnp.where((t_i >= s_i)[None, :, :, None, None], jnp.transpose(taps, (1, 0, 2, 3))[:, lag], 0.0)
    toep = jnp.transpose(toep, (0, 1, 4, 2, 3)).reshape(n_grp, lch * S5_GROUP_DIM, lch * S5_GROUP_DIM)
    rev_re = pw_re[:lch][::-1]
    rev_im = pw_im[:lch][::-1]
    w_re = rev_re[..., None] * bb_re[None] - rev_im[..., None] * bb_im[None]
    w_im = rev_re[..., None] * bb_im[None] + rev_im[..., None] * bb_re[None]
    lay_w = lambda w: jnp.pad(jnp.transpose(w, (1, 0, 3, 2)).reshape(n_grp, lch * S5_GROUP_DIM, n_st),
                              ((0, 0), (0, 0), (0, LANES - n_st)))
    tw = jnp.concatenate([toep, lay_w(w_re), lay_w(w_im)], axis=2).astype(BF16)
    nx_re = pw_re[1:]
    nx_im = pw_im[1:]
    v_re = c_re[None] * nx_re[:, :, None, :] - c_im[None] * nx_im[:, :, None, :]
    v_im = c_re[None] * nx_im[:, :, None, :] + c_im[None] * nx_re[:, :, None, :]
    lay_v = lambda w: jnp.pad(jnp.transpose(w, (1, 3, 0, 2)).reshape(n_grp, n_st, lch * S5_GROUP_DIM),
                              ((0, 0), (0, LANES - n_st), (0, 0)))
    vv = jnp.concatenate([lay_v(v_re), -lay_v(v_im)], axis=1).astype(BF16)
    lam = jnp.zeros((n_grp, 8, LANES), F32)
    lam = lam.at[:, 0, :n_st].set(pw_re[lch]).at[:, 1, :n_st].set(pw_im[lch])
    return tw, vv, lam


def _s5_out_kernel(y_ref, u_ref, d_ref, gw_ref, gb_ref, wo_ref, x_ref, g_ref, b_ref, r_ref, o_ref, route_ref,
                   count_ref, carry_ref):
    @pl.when(pl.program_id(0) == 0)
    def _():
        carry_ref[...] = jnp.zeros_like(carry_ref)

    rows = x_ref.shape[0]
    y = y_ref[...] + d_ref[...] * u_ref[...]
    gl = _gelu(y)
    z = gl * _sigmoid(_dot(gl.astype(BF16), gw_ref[...]) + gb_ref[...])
    mix = _dot(z.astype(BF16), wo_ref[...])
    xo = _layernorm(DEEPNORM_ALPHA * x_ref[...] + mix, g_ref[...], b_ref[...])
    o_ref[...] = xo
    logits = jnp.dot(xo, r_ref[...], preferred_element_type=F32, precision=lax.Precision.HIGHEST)
    idx = lax.broadcasted_iota(jnp.int32, logits.shape, 1)
    m1 = jnp.max(logits, axis=-1, keepdims=True)
    i1 = jnp.min(jnp.where(logits == m1, idx, N_EXPERTS), axis=-1, keepdims=True)
    rest = jnp.where(idx == i1, -jnp.inf, logits)
    m2 = jnp.max(rest, axis=-1, keepdims=True)
    i2 = jnp.min(jnp.where(rest == m2, idx, N_EXPERTS), axis=-1, keepdims=True)
    e2 = jnp.exp(m2 - m1)
    g1 = 1.0 / (1.0 + e2)
    g2 = e2 * g1
    sel = jnp.where((idx == i1) | (idx == i2), 1.0, 0.0)
    t_r = lax.broadcasted_iota(jnp.int32, (rows, rows), 0)
    t_c = lax.broadcasted_iota(jnp.int32, (rows, rows), 1)
    earlier = jnp.where(t_c < t_r, 1.0, 0.0).astype(BF16)
    before = _dot(earlier, sel.astype(BF16)) + carry_ref[...]
    r1 = jnp.sum(jnp.where(idx == i1, before, 0.0), axis=-1, keepdims=True)
    r2 = jnp.sum(jnp.where(idx == i2, before, 0.0), axis=-1, keepdims=True)
    total = carry_ref[...] + jnp.sum(sel, axis=0, keepdims=True)
    carry_ref[...] = total
    count_ref[...] = total
    fields = (i1.astype(F32), i2.astype(F32), r1, r2, g1, g2)
    route = jnp.zeros(logits.shape, F32)
    for lane, val in enumerate(fields):
        route = jnp.where(idx == lane, val, route)
    route_ref[...] = route


def _s5_out(y2d, u2d, d_skip, glu_w, glu_b, w_out, x2d, g, b, router, *, rows):
    n, d = x2d.shape
    full = lambda a: pl.BlockSpec(a.shape, lambda i: (0,) * a.ndim)
    row = lambda w: pl.BlockSpec((rows, w), lambda i: (i, 0))
    args = (y2d, u2d, d_skip.reshape(1, -1), glu_w.astype(BF16), glu_b.reshape(1, -1), w_out.astype(BF16), x2d,
            g.reshape(1, -1), b.reshape(1, -1), router)
    in_specs = [row(d), row(d), full(args[2]), full(args[3]), full(args[4]), full(args[5]), row(d), full(args[7]),
                full(args[8]), full(args[9])]
    return pl.pallas_call(
        _s5_out_kernel,
        grid=(n // rows,),
        in_specs=in_specs,
        out_specs=(row(d), row(N_EXPERTS), pl.BlockSpec((1, N_EXPERTS), lambda i: (0, 0))),
        out_shape=(jax.ShapeDtypeStruct((n, d), F32), jax.ShapeDtypeStruct((n, N_EXPERTS), F32),
                   jax.ShapeDtypeStruct((1, N_EXPERTS), F32)),
        scratch_shapes=[pltpu.VMEM((1, N_EXPERTS), F32)],
        compiler_params=_params(("arbitrary",)),
        name="s5_out",
    )(*args)


def _dispatch_kernel(slot_ref, x_ref, xs_in_ref, xs_ref, sem):
    del xs_in_ref
    rows = x_ref.shape[0]

    def issue(r, carry):
        for k in range(2):
            s = slot_ref[0, 0, 2 * r + k]
            pltpu.make_async_copy(x_ref.at[pl.ds(r, 1)], xs_ref.at[pl.ds(s, 1)], sem).start()
        return carry

    lax.fori_loop(0, rows, issue, 0)
    for k in range(2):
        pltpu.make_async_copy(x_ref, xs_ref.at[pl.ds(0, rows)], sem).wait()


def _dispatch(x2d, slots, n_slots, *, rows):
    n, d = x2d.shape
    return pl.pallas_call(
        _dispatch_kernel,
        grid=(n // rows,),
        in_specs=[pl.BlockSpec((1, 1, 2 * rows), lambda i: (i, 0, 0), memory_space=pltpu.SMEM),
                  pl.BlockSpec((rows, d), lambda i: (i, 0)),
                  pl.BlockSpec(memory_space=pl.ANY)],
        out_specs=pl.BlockSpec(memory_space=pl.ANY),
        out_shape=jax.ShapeDtypeStruct((n_slots, d), F32),
        scratch_shapes=[pltpu.SemaphoreType.DMA],
        input_output_aliases={2: 0},
        compiler_params=_params(("arbitrary",)),
        name="moe_dispatch",
    )(slots, x2d, jnp.zeros((n_slots, d), F32))


def _ffn_grouped_kernel(te_ref, na_ref, x_ref, wg_ref, wu_ref, wd_ref, o_ref, xb_ref, acc_ref):
    del te_ref
    i = pl.program_id(0)
    f = pl.program_id(1)
    active = i < na_ref[0]

    @pl.when(active)
    def _():
        @pl.when(f == 0)
        def _():
            xb_ref[...] = x_ref[...].astype(BF16)

        xb = xb_ref[...]
        gate = _dot(xb, wg_ref[0])
        up = _dot(xb, wu_ref[0])
        hid = (gate * _sigmoid(gate) * up).astype(BF16)
        contrib = _dot(hid, wd_ref[0])

        @pl.when(f == 0)
        def _():
            acc_ref[...] = contrib

        @pl.when(f > 0)
        def _():
            acc_ref[...] += contrib

    @pl.when(f == pl.num_programs(1) - 1)
    def _():
        o_ref[...] = jnp.where(active, acc_ref[...], 0.0)


def _ffn_grouped(xs, tile_expert, n_active, w_gate, w_up, w_down, *, rows, fchunk):
    m, d = xs.shape
    n_f = w_gate.shape[2] // fchunk
    tile = lambda i, na: jnp.minimum(i, na[0] - 1)
    chunk = lambda i, f, na: jnp.where(i < na[0], f, n_f - 1)
    return pl.pallas_call(
        _ffn_grouped_kernel,
        grid_spec=pltpu.PrefetchScalarGridSpec(
            num_scalar_prefetch=2,
            grid=(m // rows, n_f),
            in_specs=[pl.BlockSpec((rows, d), lambda i, f, te, na: (tile(i, na), 0)),
                      pl.BlockSpec((1, d, fchunk), lambda i, f, te, na: (te[tile(i, na)], 0, chunk(i, f, na))),
                      pl.BlockSpec((1, d, fchunk), lambda i, f, te, na: (te[tile(i, na)], 0, chunk(i, f, na))),
                      pl.BlockSpec((1, fchunk, d), lambda i, f, te, na: (te[tile(i, na)], chunk(i, f, na), 0))],
            out_specs=pl.BlockSpec((rows, d), lambda i, f, te, na: (i, 0)),
            scratch_shapes=[pltpu.VMEM((rows, d), BF16), pltpu.VMEM((rows, d), F32)],
        ),
        out_shape=jax.ShapeDtypeStruct((m, d), F32),
        compiler_params=_params(("arbitrary", "arbitrary")),
        name="ffn_moe",
    )(tile_expert, n_active, xs, w_gate, w_up, w_down)


def _combine_kernel(slot_ref, x_ref, route_ref, ys_ref, g_ref, b_ref, o_ref, y1_ref, y2_ref, sem):
    rows = x_ref.shape[0]

    def issue(r, carry):
        for k, dst in enumerate((y1_ref, y2_ref)):
            s = slot_ref[0, 0, 2 * r + k]
            pltpu.make_async_copy(ys_ref.at[pl.ds(s, 1)], dst.at[pl.ds(r, 1)], sem).start()
        return carry

    lax.fori_loop(0, rows, issue, 0)
    for dst in (y1_ref, y2_ref):
        pltpu.make_async_copy(ys_ref.at[pl.ds(0, rows)], dst, sem).wait()
    route = route_ref[...]
    moe = route[:, 4:5] * y1_ref[...] + route[:, 5:6] * y2_ref[...]
    o_ref[...] = _layernorm(DEEPNORM_ALPHA * x_ref[...] + moe, g_ref[...], b_ref[...])


def _combine_ln(x2d, route, slots, ys, g, b, *, rows):
    n, d = x2d.shape
    g2, b2 = g.reshape(1, -1), b.reshape(1, -1)
    return pl.pallas_call(
        _combine_kernel,
        grid=(n // rows,),
        in_specs=[pl.BlockSpec((1, 1, 2 * rows), lambda i: (i, 0, 0), memory_space=pltpu.SMEM),
                  pl.BlockSpec((rows, d), lambda i: (i, 0)),
                  pl.BlockSpec((rows, N_EXPERTS), lambda i: (i, 0)),
                  pl.BlockSpec(memory_space=pl.ANY),
                  pl.BlockSpec((1, d), lambda i: (0, 0)),
                  pl.BlockSpec((1, d), lambda i: (0, 0))],
        out_specs=pl.BlockSpec((rows, d), lambda i: (i, 0)),
        out_shape=jax.ShapeDtypeStruct((n, d), F32),
        scratch_shapes=[pltpu.VMEM((rows, d), F32), pltpu.VMEM((rows, d), F32), pltpu.SemaphoreType.DMA],
        compiler_params=_params(("arbitrary",)),
        name="moe_combine",
    )(slots, x2d, route, ys, g2, b2)


def _moe(x2d, route, counts, w_gate, w_up, w_down, g, b, *, rows, fchunk):
    n, _ = x2d.shape
    counts = counts.reshape(-1).astype(jnp.int32)
    padded = (counts + rows - 1) // rows * rows
    ends = jnp.cumsum(padded)
    offs = ends - padded
    n_tiles = (2 * n) // rows + N_EXPERTS
    starts = jnp.arange(n_tiles, dtype=jnp.int32) * rows
    tile_expert = jnp.minimum(jnp.sum(ends[None, :] <= starts[:, None], axis=1), N_EXPERTS - 1).astype(jnp.int32)
    n_active = (ends[-1:] // rows).astype(jnp.int32)
    expert = route[:, 0:2].astype(jnp.int32)
    rank = route[:, 2:4].astype(jnp.int32)
    base = jnp.sum(jnp.where(expert[..., None] == jnp.arange(N_EXPERTS), offs, 0), axis=-1)
    slots = (base + rank).reshape(n // rows, 1, 2 * rows)
    xs = _dispatch(x2d, slots, n_tiles * rows, rows=rows)
    ys = _ffn_grouped(xs, tile_expert, n_active, w_gate, w_up, w_down, rows=rows, fchunk=fchunk)
    return _combine_ln(x2d, route, slots, ys, g, b, rows=rows)


def _layer_even(x2d, positions, bsz, seq, w_in, gm_ln_g, gm_ln_b, w_s, b_s, q_norm, w_uq, kv_norm, w_ukv, w_out,
                ln_g, ln_b, f_gate, f_up, f_down, f_ln_g, f_ln_b, *, rows, ffn_rows, fchunk, attn_blk):
    cos_p, sin_p = _rope_tables(positions)
    ya, q, k, v = _mixer_in(x2d, cos_p, sin_p, w_in, gm_ln_g, gm_ln_b, w_s, b_s, q_norm, w_uq, kv_norm, w_ukv,
                            rows=rows)
    yb = _attention(q, k, v, bsz=bsz, seq=seq, blk=attn_blk)
    wo = w_out.astype(BF16)
    x2d = _proj_ln([ya, yb], [wo[:A_WIDTH], wo[A_WIDTH:]], x2d, ln_g, ln_b, rows=rows, name="mixer_out")
    return _ffn(x2d, f_gate.astype(BF16), f_up.astype(BF16), f_down.astype(BF16), f_ln_g, f_ln_b, rows=ffn_rows,
                fchunk=fchunk)


def _layer_odd(x2d, bsz, seq, w_in, a_re, a_im, log_step, b_re, b_im, c_re, c_im, d_skip, glu_w, glu_b, w_out, ln_g,
               ln_b, router, m_gate, m_up, m_down, m_ln_g, m_ln_b, *, rows, ffn_rows, fchunk):
    n, d = x2d.shape
    n_grp = d // S5_GROUP_DIM
    lch = S5_CHUNK
    n_chunks = seq // lch
    u2d = _matmul(x2d, w_in.astype(BF16), rows=rows, name="s5_in")
    tw, vv, lam = _s5_weights(a_re, a_im, log_step, b_re, b_im, c_re, c_im)
    uc = u2d.reshape(bsz, n_chunks, lch, n_grp, S5_GROUP_DIM).transpose(3, 1, 0, 2, 4)
    uc = uc.reshape(n_grp, n_chunks * bsz, lch * S5_GROUP_DIM).astype(BF16)
    yc = _s5_scan(uc, tw, vv, lam, bsz=bsz)
    y2d = yc.reshape(n_grp, n_chunks, bsz, lch, S5_GROUP_DIM).transpose(2, 1, 3, 0, 4).reshape(n, d)
    x2d, route, counts = _s5_out(y2d, u2d, d_skip, glu_w, glu_b, w_out, x2d, ln_g, ln_b, router, rows=rows)
    return _moe(x2d, route, counts, m_gate.astype(BF16), m_up.astype(BF16), m_down.astype(BF16), m_ln_g, m_ln_b,
                rows=ffn_rows, fchunk=fchunk)


def kernel(x, positions, ab_w_in, gm_ln_g, gm_ln_b, gm_w_s, gm_b_s, mla_q_norm, mla_w_uq, mla_kv_norm, mla_w_ukv, ab_w_out, ab_ln_g, ab_ln_b, ffd_w_gate, ffd_w_up, ffd_w_down, ffd_ln_g, ffd_ln_b, c_w_in, s5_a_re, s5_a_im, s5_log_step, s5_b_re, s5_b_im, s5_c_re, s5_c_im, s5_d, glu_w, glu_b, c_w_out, c_ln_g, c_ln_b, moe_router, moe_w_gate, moe_w_up, moe_w_down, moe_ln_g, moe_ln_b):
    bsz, seq, d = x.shape
    x2d = x.reshape(bsz * seq, d)
    rows = min(512, seq)
    for i in range(DEPTH):
        j = i // 2
        if i % 2 == 0:
            x2d = _layer_even(x2d, positions, bsz, seq, ab_w_in[j], gm_ln_g[j], gm_ln_b[j], gm_w_s[j], gm_b_s[j],
                              mla_q_norm[j], mla_w_uq[j], mla_kv_norm[j], mla_w_ukv[j], ab_w_out[j], ab_ln_g[j],
                              ab_ln_b[j], ffd_w_gate[j], ffd_w_up[j], ffd_w_down[j], ffd_ln_g[j], ffd_ln_b[j],
                              rows=rows, ffn_rows=rows, fchunk=ffd_w_gate.shape[2] // 2, attn_blk=min(256, seq))
        else:
            x2d = _layer_odd(x2d, bsz, seq, c_w_in[j], s5_a_re[j], s5_a_im[j], s5_log_step[j], s5_b_re[j],
                             s5_b_im[j], s5_c_re[j], s5_c_im[j], s5_d[j], glu_w[j], glu_b[j], c_w_out[j], c_ln_g[j],
                             c_ln_b[j], moe_router[j], moe_w_gate[j], moe_w_up[j], moe_w_down[j], moe_ln_g[j],
                             moe_ln_b[j], rows=rows, ffn_rows=rows, fchunk=moe_w_gate.shape[3] // 4)
    return x2d.reshape(bsz, seq, d)
```

```python
import functools
import math

import jax
import jax.numpy as jnp
from jax import lax
from jax.experimental import pallas as pl
from jax.experimental.pallas import tpu as pltpu

F32 = jnp.float32
BF16 = jnp.bfloat16

A_GROUPS = 4
A_GROUP_DIM = 128
A_WIDTH = A_GROUPS * A_GROUP_DIM
A_CHUNK = 128
MLA_HEADS = 8
QK_NOPE = 64
QK_ROPE = 32
QK_HEAD = QK_NOPE + QK_ROPE
V_HEAD = 64
Q_LORA = 384
KV_LORA = 256
ROPE_THETA = 10000.0
S5_GROUP_DIM = 16
S5_STATE = 64
N_EXPERTS = 8
LN_EPS = 1e-5
RMS_EPS = 1e-6
DEPTH = 2
DEEPNORM_ALPHA = (2.0 * DEPTH) ** 0.25

LANES = 128
HEAD_PAD = LANES
S5_CHUNK = 16
VMEM_LIMIT = 56 * 1024 * 1024
NEG_BIG = -1e30


def _params(sem):
    return pltpu.CompilerParams(dimension_semantics=sem, vmem_limit_bytes=VMEM_LIMIT)


def _gelu(x):
    c = math.sqrt(2.0 / math.pi)
    return 0.5 * x * (1.0 + jnp.tanh(c * (x + 0.044715 * (x * x * x))))


def _sigmoid(x):
    return 1.0 / (1.0 + jnp.exp(-x))


def _layernorm(x, g, b):
    mu = jnp.mean(x, axis=-1, keepdims=True)
    xc = x - mu
    var = jnp.mean(xc * xc, axis=-1, keepdims=True)
    return xc * lax.rsqrt(var + LN_EPS) * g + b


def _rmsnorm(x, g):
    ms = jnp.mean(x * x, axis=-1, keepdims=True)
    return x * lax.rsqrt(ms + RMS_EPS) * g


def _dot(a, b):
    return jnp.dot(a, b, preferred_element_type=F32)


def _rope_table_kernel(inv_ref, pos_ref, cos_ref, sin_ref):
    pos = pos_ref[...].astype(F32)
    for j in range(QK_ROPE // 2):
        ang = pos * inv_ref[j]
        cos_ref[j] = jnp.cos(ang)
        sin_ref[j] = jnp.sin(ang)


def _rope_tables(positions):
    n = positions.size
    half = QK_ROPE // 2
    inv_freq = 1.0 / (ROPE_THETA ** (jnp.arange(0, QK_ROPE, 2, dtype=F32) / QK_ROPE))
    pos2d = positions.reshape(n // LANES, LANES)
    cos_t, sin_t = pl.pallas_call(
        _rope_table_kernel,
        out_shape=(jax.ShapeDtypeStruct((half, n // LANES, LANES), F32),) * 2,
        in_specs=[pl.BlockSpec(memory_space=pltpu.SMEM), pl.BlockSpec(memory_space=pltpu.VMEM)],
        out_specs=(pl.BlockSpec(memory_space=pltpu.VMEM),) * 2,
        name="rope_table",
    )(inv_freq, pos2d)
    cos_c = cos_t.reshape(half, n).T
    sin_c = sin_t.reshape(half, n).T
    ones = jnp.ones((n, QK_NOPE), F32)
    zeros_n = jnp.zeros((n, QK_NOPE), F32)
    zeros_p = jnp.zeros((n, HEAD_PAD - QK_HEAD), F32)
    cos_p = jnp.concatenate([ones, cos_c, cos_c, zeros_p], axis=1)
    sin_p = jnp.concatenate([zeros_n, sin_c, sin_c, zeros_p], axis=1)
    return cos_p, sin_p


def _mixer_in_kernel(x_ref, cos_ref, sin_ref, w_in_ref, lng_ref, lnb_ref, ws_ref, bs_ref, qn_ref, wq_ref,
                     kvn_ref, wkv_ref, ya_ref, q_ref, k_ref, v_ref):
    rows = x_ref.shape[0]
    xb = x_ref[...].astype(BF16)
    h = _dot(xb, w_in_ref[...])
    o_q = 2 * A_WIDTH
    o_kv = o_q + Q_LORA
    o_pe = o_kv + KV_LORA
    o_rot = o_pe + HEAD_PAD
    cos_p = cos_ref[...]
    sin_p = sin_ref[...]

    a_u = _gelu(h[:, :A_WIDTH])
    a_v = _gelu(h[:, A_WIDTH:o_q])
    vn = _layernorm(a_v, lng_ref[...], lnb_ref[...]).astype(BF16)
    t_idx = lax.broadcasted_iota(jnp.int32, (A_CHUNK, A_CHUNK), 0)
    s_idx = lax.broadcasted_iota(jnp.int32, (A_CHUNK, A_CHUNK), 1)
    causal = s_idx <= t_idx
    bs = bs_ref[...]
    for g in range(A_GROUPS):
        w_g = jnp.where(causal, ws_ref[g], 0.0).astype(BF16)
        cols = slice(g * A_GROUP_DIM, (g + 1) * A_GROUP_DIM)
        for c in range(rows // A_CHUNK):
            rws = slice(c * A_CHUNK, (c + 1) * A_CHUNK)
            mixed = _dot(w_g, vn[rws, cols]) + bs[:, cols]
            ya_ref[rws, cols] = (a_u[rws, cols] * mixed).astype(BF16)

    cqn = _rmsnorm(h[:, o_q:o_kv], qn_ref[...]).astype(BF16)
    q2 = _dot(cqn, wq_ref[...])
    half = MLA_HEADS * HEAD_PAD
    for hd in range(MLA_HEADS):
        cols = slice(hd * HEAD_PAD, (hd + 1) * HEAD_PAD)
        rot = slice(half + hd * HEAD_PAD, half + (hd + 1) * HEAD_PAD)
        q_ref[:, cols] = (q2[:, cols] * cos_p + q2[:, rot] * sin_p).astype(BF16)

    ckvn = _rmsnorm(h[:, o_kv:o_pe], kvn_ref[...]).astype(BF16)
    kv = _dot(ckvn, wkv_ref[...])
    kpe = h[:, o_pe:o_rot] * cos_p + h[:, o_rot:o_rot + HEAD_PAD] * sin_p
    for hd in range(MLA_HEADS):
        cols = slice(hd * HEAD_PAD, (hd + 1) * HEAD_PAD)
        k_ref[:, cols] = (kv[:, cols] + kpe).astype(BF16)
    v_ref[...] = kv[:, half:].astype(BF16)


def _mixer_in(x2d, cos_p, sin_p, w_in, gm_ln_g, gm_ln_b, w_s, b_s, q_norm, w_uq, kv_norm, w_ukv, *, rows):
    n, d = x2d.shape
    hp = MLA_HEADS * HEAD_PAD
    o_pe = 2 * A_WIDTH + Q_LORA + KV_LORA
    half = QK_ROPE // 2
    w_pe = w_in[:, o_pe:o_pe + QK_ROPE]
    w_pe_rot = jnp.concatenate([-w_pe[:, half:], w_pe[:, :half]], axis=1)
    pad_l = jnp.zeros((d, QK_NOPE), F32)
    pad_r = jnp.zeros((d, HEAD_PAD - QK_HEAD), F32)
    w_in_p = jnp.concatenate([w_in[:, :o_pe], pad_l, w_pe, pad_r, pad_l, w_pe_rot, pad_r], axis=1).astype(BF16)
    wq = w_uq.reshape(Q_LORA, MLA_HEADS, QK_HEAD)
    wq_pe = wq[:, :, QK_NOPE:]
    wq_rot = jnp.concatenate([jnp.zeros((Q_LORA, MLA_HEADS, QK_NOPE), F32), -wq_pe[:, :, half:], wq_pe[:, :, :half]],
                             axis=2)
    padq = ((0, 0), (0, 0), (0, HEAD_PAD - QK_HEAD))
    wq2 = jnp.concatenate([jnp.pad(wq, padq).reshape(Q_LORA, hp), jnp.pad(wq_rot, padq).reshape(Q_LORA, hp)],
                          axis=1).astype(BF16)
    wkv = w_ukv.reshape(KV_LORA, MLA_HEADS, QK_NOPE + V_HEAD)
    wk = jnp.pad(wkv[:, :, :QK_NOPE], ((0, 0), (0, 0), (0, HEAD_PAD - QK_NOPE))).reshape(KV_LORA, hp)
    wv = wkv[:, :, QK_NOPE:].reshape(KV_LORA, MLA_HEADS * V_HEAD)
    wkv2 = jnp.concatenate([wk, wv], axis=1).astype(BF16)
    bs_full = jnp.repeat(b_s.T, A_GROUP_DIM, axis=1)

    full = lambda a: pl.BlockSpec(a.shape, lambda i: (0,) * a.ndim)
    row = lambda w: pl.BlockSpec((rows, w), lambda i: (i, 0))
    args = (x2d, cos_p, sin_p, w_in_p, gm_ln_g.reshape(1, -1), gm_ln_b.reshape(1, -1), w_s, bs_full,
            q_norm.reshape(1, -1), wq2, kv_norm.reshape(1, -1), wkv2)
    in_specs = [row(d), row(HEAD_PAD), row(HEAD_PAD)] + [full(a) for a in args[3:]]
    return pl.pallas_call(
        _mixer_in_kernel,
        grid=(n // rows,),
        in_specs=in_specs,
        out_specs=(row(A_WIDTH), row(hp), row(hp), row(MLA_HEADS * V_HEAD)),
        out_shape=(jax.ShapeDtypeStruct((n, A_WIDTH), BF16), jax.ShapeDtypeStruct((n, hp), BF16),
                   jax.ShapeDtypeStruct((n, hp), BF16), jax.ShapeDtypeStruct((n, MLA_HEADS * V_HEAD), BF16)),
        compiler_params=_params(("parallel",)),
        name="mixer_in",
    )(*args)


def _attn_kernel(q_ref, k_ref, v_ref, o_ref, *, blk):
    seq = q_ref.shape[0]
    scale = QK_HEAD ** -0.5
    row = lax.broadcasted_iota(jnp.int32, (blk, blk), 0)
    col = lax.broadcasted_iota(jnp.int32, (blk, blk), 1)
    diag_mask = col <= row
    first_head_lanes = lax.broadcasted_iota(jnp.int32, (blk, 2 * V_HEAD), 1) < V_HEAD

    qk = lambda a, b: lax.dot_general(a, b, (((1,), (1,)), ((), ())), preferred_element_type=F32)

    for j in range(seq // blk):
        q0 = j * blk
        outs = []
        for hh in range(2):
            cols = slice(hh * HEAD_PAD, (hh + 1) * HEAD_PAD)
            q = q_ref[q0:q0 + blk, cols]
            s_d = jnp.where(diag_mask, qk(q, k_ref[q0:q0 + blk, cols]), NEG_BIG)
            m = jnp.max(s_d, axis=-1, keepdims=True)
            if j > 0:
                s_o = qk(q, k_ref[0:q0, cols])
                m = jnp.maximum(m, jnp.max(s_o, axis=-1, keepdims=True))
            p_d = jnp.exp((s_d - m) * scale)
            l = jnp.sum(p_d, axis=-1, keepdims=True)
            acc = _dot(p_d.astype(BF16), v_ref[q0:q0 + blk, :])
            if j > 0:
                p_o = jnp.exp((s_o - m) * scale)
                l = l + jnp.sum(p_o, axis=-1, keepdims=True)
                acc = acc + _dot(p_o.astype(BF16), v_ref[0:q0, :])
            outs.append(acc / l)
        o_ref[q0:q0 + blk, :] = jnp.where(first_head_lanes, outs[0], outs[1]).astype(BF16)


def _attention(q, k, v, *, bsz, seq, blk):
    n = bsz * seq
    pair = 2 * HEAD_PAD
    return pl.pallas_call(
        functools.partial(_attn_kernel, blk=blk),
        grid=(bsz, MLA_HEADS // 2),
        in_specs=[pl.BlockSpec((seq, pair), lambda b, h: (b, h)), pl.BlockSpec((seq, pair), lambda b, h: (b, h)),
                  pl.BlockSpec((seq, 2 * V_HEAD), lambda b, h: (b, h))],
        out_specs=pl.BlockSpec((seq, 2 * V_HEAD), lambda b, h: (b, h)),
        out_shape=jax.ShapeDtypeStruct((n, MLA_HEADS * V_HEAD), BF16),
        compiler_params=_params(("parallel", "parallel")),
        name="attention",
    )(q, k, v)


def _proj_ln_kernel(*refs, n_in):
    ins = refs[:n_in]
    ws = refs[n_in:2 * n_in]
    x_ref, g_ref, b_ref, o_ref = refs[2 * n_in:]
    acc = DEEPNORM_ALPHA * x_ref[...]
    for a_ref, w_ref in zip(ins, ws):
        acc = acc + _dot(a_ref[...], w_ref[...])
    o_ref[...] = _layernorm(acc, g_ref[...], b_ref[...])


def _proj_ln(ins, ws, x2d, g, b, *, rows, name):
    n, d = x2d.shape
    full = lambda a: pl.BlockSpec(a.shape, lambda i: (0,) * a.ndim)
    row = lambda w: pl.BlockSpec((rows, w), lambda i: (i, 0))
    g2, b2 = g.reshape(1, -1), b.reshape(1, -1)
    return pl.pallas_call(
        functools.partial(_proj_ln_kernel, n_in=len(ins)),
        grid=(n // rows,),
        in_specs=[row(a.shape[1]) for a in ins] + [full(w) for w in ws] + [row(d), full(g2), full(b2)],
        out_specs=row(d),
        out_shape=jax.ShapeDtypeStruct((n, d), F32),
        compiler_params=_params(("parallel",)),
        name=name,
    )(*ins, *ws, x2d, g2, b2)


def _ffn_kernel(x_ref, wg_ref, wu_ref, wd_ref, g_ref, b_ref, o_ref, xb_ref, acc_ref):
    f = pl.program_id(1)

    @pl.when(f == 0)
    def _():
        xb_ref[...] = x_ref[...].astype(BF16)
        acc_ref[...] = jnp.zeros_like(acc_ref)

    xb = xb_ref[...]
    gate = _dot(xb, wg_ref[...])
    up = _dot(xb, wu_ref[...])
    hid = (gate * _sigmoid(gate) * up).astype(BF16)
    acc_ref[...] += _dot(hid, wd_ref[...])

    @pl.when(f == pl.num_programs(1) - 1)
    def _():
        o_ref[...] = _layernorm(DEEPNORM_ALPHA * x_ref[...] + acc_ref[...], g_ref[...], b_ref[...])


def _ffn(x2d, w_gate, w_up, w_down, g, b, *, rows, fchunk):
    n, d = x2d.shape
    dff = w_gate.shape[1]
    g2, b2 = g.reshape(1, -1), b.reshape(1, -1)
    return pl.pallas_call(
        _ffn_kernel,
        grid=(n // rows, dff // fchunk),
        in_specs=[pl.BlockSpec((rows, d), lambda i, f: (i, 0)),
                  pl.BlockSpec((d, fchunk), lambda i, f: (0, f)),
                  pl.BlockSpec((d, fchunk), lambda i, f: (0, f)),
                  pl.BlockSpec((fchunk, d), lambda i, f: (f, 0)),
                  pl.BlockSpec((1, d), lambda i, f: (0, 0)),
                  pl.BlockSpec((1, d), lambda i, f: (0, 0))],
        out_specs=pl.BlockSpec((rows, d), lambda i, f: (i, 0)),
        out_shape=jax.ShapeDtypeStruct((n, d), F32),
        scratch_shapes=[pltpu.VMEM((rows, d), BF16), pltpu.VMEM((rows, d), F32)],
        compiler_params=_params(("parallel", "arbitrary")),
        name="ffn_dense",
    )(x2d, w_gate, w_up, w_down, g2, b2)


def _matmul_kernel(x_ref, w_ref, o_ref):
    o_ref[...] = _dot(x_ref[...].astype(BF16), w_ref[...])


def _matmul(x2d, w, *, rows, name):
    n, d = x2d.shape
    return pl.pallas_call(
        _matmul_kernel,
        grid=(n // rows,),
        in_specs=[pl.BlockSpec((rows, d), lambda i: (i, 0)), pl.BlockSpec(w.shape, lambda i: (0, 0))],
        out_specs=pl.BlockSpec((rows, w.shape[1]), lambda i: (i, 0)),
        out_shape=jax.ShapeDtypeStruct((n, w.shape[1]), F32),
        compiler_params=_params(("parallel",)),
        name=name,
    )(x2d, w)


def _s5_kernel(u_ref, tw_ref, v_ref, lam_ref, y_ref, sre_ref, sim_ref, st_ref, *, bsz):
    lc = u_ref.shape[2]
    n_chunks = u_ref.shape[1] // bsz
    big = _dot(u_ref[0], tw_ref[0])
    y_ref[0] = big[:, :lc]
    sre_ref[...] = big[:, lc:lc + LANES]
    sim_ref[...] = big[:, lc + LANES:]
    a_re = lam_ref[0, 0:1, :]
    a_im = lam_ref[0, 1:2, :]

    def step(j, st):
        re, im = st
        r0 = pl.multiple_of(j * bsz, bsz)
        st_ref[pl.ds(r0, bsz), :LANES] = re.astype(BF16)
        st_ref[pl.ds(r0, bsz), LANES:] = im.astype(BF16)
        new_re = a_re * re - a_im * im + sre_ref[pl.ds(r0, bsz), :]
        new_im = a_re * im + a_im * re + sim_ref[pl.ds(r0, bsz), :]
        return new_re, new_im

    zero = jnp.zeros((bsz, LANES), F32)
    lax.fori_loop(0, n_chunks, step, (zero, zero))
    y_ref[0] += _dot(st_ref[...], v_ref[0])


def _s5_scan(uc, tw, vv, lam, *, bsz):
    n_grp, rows, lc = uc.shape
    return pl.pallas_call(
        functools.partial(_s5_kernel, bsz=bsz),
        grid=(n_grp,),
        in_specs=[pl.BlockSpec((1, rows, lc), lambda g: (g, 0, 0)),
                  pl.BlockSpec((1,) + tw.shape[1:], lambda g: (g, 0, 0)),
                  pl.BlockSpec((1,) + vv.shape[1:], lambda g: (g, 0, 0)),
                  pl.BlockSpec((1,) + lam.shape[1:], lambda g: (g, 0, 0))],
        out_specs=pl.BlockSpec((1, rows, lc), lambda g: (g, 0, 0)),
        out_shape=jax.ShapeDtypeStruct((n_grp, rows, lc), F32),
        scratch_shapes=[pltpu.VMEM((rows, LANES), F32), pltpu.VMEM((rows, LANES), F32),
                        pltpu.VMEM((rows, 2 * LANES), BF16)],
        compiler_params=_params(("parallel",)),
        name="s5_scan",
    )(uc, tw, vv, lam)


def _s5_weights(a_re, a_im, log_step, b_re, b_im, c_re, c_im):
    lch = S5_CHUNK
    n_grp, n_st = a_re.shape
    delta = jnp.exp(log_step)[:, None]
    mag = jnp.exp(delta * a_re)
    abar_re = mag * jnp.cos(delta * a_im)
    abar_im = mag * jnp.sin(delta * a_im)
    den = a_re * a_re + a_im * a_im
    coef_re = ((abar_re - 1.0) * a_re + abar_im * a_im) / den
    coef_im = (abar_im * a_re - (abar_re - 1.0) * a_im) / den
    bb_re = coef_re[..., None] * b_re - coef_im[..., None] * b_im
    bb_im = coef_re[..., None] * b_im + coef_im[..., None] * b_re
    kk = jnp.arange(lch + 1, dtype=F32)[:, None, None]
    pmag = jnp.exp(kk * (delta * a_re)[None])
    pw_re = pmag * jnp.cos(kk * (delta * a_im)[None])
    pw_im = pmag * jnp.sin(kk * (delta * a_im)[None])
    cl_re = c_re[None] * pw_re[:lch, :, None, :] - c_im[None] * pw_im[:lch, :, None, :]
    cl_im = c_re[None] * pw_im[:lch, :, None, :] + c_im[None] * pw_re[:lch, :, None, :]
    taps = (jnp.sum(cl_re[..., None] * bb_re[None, :, None], axis=3)
            - jnp.sum(cl_im[..., None] * bb_im[None, :, None], axis=3))
    s_i = jnp.arange(lch)[:, None]
    t_i = jnp.arange(lch)[None, :]
    lag = jnp.clip(t_i - s_i, 0, lch - 1)
    toep = jnp.where((t_i >= s_i)[None, :, :, None, None], jnp.transpose(taps, (1, 0, 2, 3))[:, lag], 0.0)
    toep = jnp.transpose(toep, (0, 1, 4, 2, 3)).reshape(n_grp, lch * S5_GROUP_DIM, lch * S5_GROUP_DIM)
    rev_re = pw_re[:lch][::-1]
    rev_im = pw_im[:lch][::-1]
    w_re = rev_re[..., None] * bb_re[None] - rev_im[..., None] * bb_im[None]
    w_im = rev_re[..., None] * bb_im[None] + rev_im[..., None] * bb_re[None]
    lay_w = lambda w: jnp.pad(jnp.transpose(w, (1, 0, 3, 2)).reshape(n_grp, lch * S5_GROUP_DIM, n_st),
                              ((0, 0), (0, 0), (0, LANES - n_st)))
    tw = jnp.concatenate([toep, lay_w(w_re), lay_w(w_im)], axis=2).astype(BF16)
    nx_re = pw_re[1:]
    nx_im = pw_im[1:]
    v_re = c_re[None] * nx_re[:, :, None, :] - c_im[None] * nx_im[:, :, None, :]
    v_im = c_re[None] * nx_im[:, :, None, :] + c_im[None] * nx_re[:, :, None, :]
    lay_v = lambda w: jnp.pad(jnp.transpose(w, (1, 3, 0, 2)).reshape(n_grp, n_st, lch * S5_GROUP_DIM),
                              ((0, 0), (0, LANES - n_st), (0, 0)))
    vv = jnp.concatenate([lay_v(v_re), -lay_v(v_im)], axis=1).astype(BF16)
    lam = jnp.zeros((n_grp, 8, LANES), F32)
    lam = lam.at[:, 0, :n_st].set(pw_re[lch]).at[:, 1, :n_st].set(pw_im[lch])
    return tw, vv, lam


def _s5_out_kernel(y_ref, u_ref, d_ref, gw_ref, gb_ref, wo_ref, x_ref, g_ref, b_ref, r_ref, o_ref, route_ref,
                   count_ref, carry_ref):
    @pl.when(pl.program_id(0) == 0)
    def _():
        carry_ref[...] = jnp.zeros_like(carry_ref)

    rows = x_ref.shape[0]
    y = y_ref[...] + d_ref[...] * u_ref[...]
    gl = _gelu(y)
    z = gl * _sigmoid(_dot(gl.astype(BF16), gw_ref[...]) + gb_ref[...])
    mix = _dot(z.astype(BF16), wo_ref[...])
    xo = _layernorm(DEEPNORM_ALPHA * x_ref[...] + mix, g_ref[...], b_ref[...])
    o_ref[...] = xo
    logits = jnp.dot(xo, r_ref[...], preferred_element_type=F32, precision=lax.Precision.HIGHEST)
    idx = lax.broadcasted_iota(jnp.int32, logits.shape, 1)
    m1 = jnp.max(logits, axis=-1, keepdims=True)
    i1 = jnp.min(jnp.where(logits == m1, idx, N_EXPERTS), axis=-1, keepdims=True)
    rest = jnp.where(idx == i1, -jnp.inf, logits)
    m2 = jnp.max(rest, axis=-1, keepdims=True)
    i2 = jnp.min(jnp.where(rest == m2, idx, N_EXPERTS), axis=-1, keepdims=True)
    e2 = jnp.exp(m2 - m1)
    g1 = 1.0 / (1.0 + e2)
    g2 = e2 * g1
    sel = jnp.where((idx == i1) | (idx == i2), 1.0, 0.0)
    t_r = lax.broadcasted_iota(jnp.int32, (rows, rows), 0)
    t_c = lax.broadcasted_iota(jnp.int32, (rows, rows), 1)
    earlier = jnp.where(t_c < t_r, 1.0, 0.0).astype(BF16)
    before = _dot(earlier, sel.astype(BF16)) + carry_ref[...]
    r1 = jnp.sum(jnp.where(idx == i1, before, 0.0), axis=-1, keepdims=True)
    r2 = jnp.sum(jnp.where(idx == i2, before, 0.0), axis=-1, keepdims=True)
    total = carry_ref[...] + jnp.sum(sel, axis=0, keepdims=True)
    carry_ref[...] = total
    count_ref[...] = total
    fields = (i1.astype(F32), i2.astype(F32), r1, r2, g1, g2)
    route = jnp.zeros(logits.shape, F32)
    for lane, val in enumerate(fields):
        route = jnp.where(idx == lane, val, route)
    route_ref[...] = route


def _s5_out(y2d, u2d, d_skip, glu_w, glu_b, w_out, x2d, g, b, router, *, rows):
    n, d = x2d.shape
    full = lambda a: pl.BlockSpec(a.shape, lambda i: (0,) * a.ndim)
    row = lambda w: pl.BlockSpec((rows, w), lambda i: (i, 0))
    args = (y2d, u2d, d_skip.reshape(1, -1), glu_w.astype(BF16), glu_b.reshape(1, -1), w_out.astype(BF16), x2d,
            g.reshape(1, -1), b.reshape(1, -1), router)
    in_specs = [row(d), row(d), full(args[2]), full(args[3]), full(args[4]), full(args[5]), row(d), full(args[7]),
                full(args[8]), full(args[9])]
    return pl.pallas_call(
        _s5_out_kernel,
        grid=(n // rows,),
        in_specs=in_specs,
        out_specs=(row(d), row(N_EXPERTS), pl.BlockSpec((1, N_EXPERTS), lambda i: (0, 0))),
        out_shape=(jax.ShapeDtypeStruct((n, d), F32), jax.ShapeDtypeStruct((n, N_EXPERTS), F32),
                   jax.ShapeDtypeStruct((1, N_EXPERTS), F32)),
        scratch_shapes=[pltpu.VMEM((1, N_EXPERTS), F32)],
        compiler_params=_params(("arbitrary",)),
        name="s5_out",
    )(*args)


def _dispatch_kernel(slot_ref, x_ref, xs_in_ref, xs_ref, sem):
    del xs_in_ref
    rows = x_ref.shape[0]

    def issue(r, carry):
        for k in range(2):
            s = slot_ref[0, 0, 2 * r + k]
            pltpu.make_async_copy(x_ref.at[pl.ds(r, 1)], xs_ref.at[pl.ds(s, 1)], sem).start()
        return carry

    lax.fori_loop(0, rows, issue, 0)
    for k in range(2):
        pltpu.make_async_copy(x_ref, xs_ref.at[pl.ds(0, rows)], sem).wait()


def _dispatch(x2d, slots, n_slots, *, rows):
    n, d = x2d.shape
    return pl.pallas_call(
        _dispatch_kernel,
        grid=(n // rows,),
        in_specs=[pl.BlockSpec((1, 1, 2 * rows), lambda i: (i, 0, 0), memory_space=pltpu.SMEM),
                  pl.BlockSpec((rows, d), lambda i: (i, 0)),
                  pl.BlockSpec(memory_space=pl.ANY)],
        out_specs=pl.BlockSpec(memory_space=pl.ANY),
        out_shape=jax.ShapeDtypeStruct((n_slots, d), F32),
        scratch_shapes=[pltpu.SemaphoreType.DMA],
        input_output_aliases={2: 0},
        compiler_params=_params(("arbitrary",)),
        name="moe_dispatch",
    )(slots, x2d, jnp.zeros((n_slots, d), F32))


def _ffn_grouped_kernel(te_ref, na_ref, x_ref, wg_ref, wu_ref, wd_ref, o_ref, xb_ref, acc_ref):
    del te_ref
    i = pl.program_id(0)
    f = pl.program_id(1)
    active = i < na_ref[0]

    @pl.when(active)
    def _():
        @pl.when(f == 0)
        def _():
            xb_ref[...] = x_ref[...].astype(BF16)

        xb = xb_ref[...]
        gate = _dot(xb, wg_ref[0])
        up = _dot(xb, wu_ref[0])
        hid = (gate * _sigmoid(gate) * up).astype(BF16)
        contrib = _dot(hid, wd_ref[0])

        @pl.when(f == 0)
        def _():
            acc_ref[...] = contrib

        @pl.when(f > 0)
        def _():
            acc_ref[...] += contrib

    @pl.when(f == pl.num_programs(1) - 1)
    def _():
        o_ref[...] = jnp.where(active, acc_ref[...], 0.0)


def _ffn_grouped(xs, tile_expert, n_active, w_gate, w_up, w_down, *, rows, fchunk):
    m, d = xs.shape
    n_f = w_gate.shape[2] // fchunk
    tile = lambda i, na: jnp.minimum(i, na[0] - 1)
    chunk = lambda i, f, na: jnp.where(i < na[0], f, n_f - 1)
    return pl.pallas_call(
        _ffn_grouped_kernel,
        grid_spec=pltpu.PrefetchScalarGridSpec(
            num_scalar_prefetch=2,
            grid=(m // rows, n_f),
            in_specs=[pl.BlockSpec((rows, d), lambda i, f, te, na: (tile(i, na), 0)),
                      pl.BlockSpec((1, d, fchunk), lambda i, f, te, na: (te[tile(i, na)], 0, chunk(i, f, na))),
                      pl.BlockSpec((1, d, fchunk), lambda i, f, te, na: (te[tile(i, na)], 0, chunk(i, f, na))),
                      pl.BlockSpec((1, fchunk, d), lambda i, f, te, na: (te[tile(i, na)], chunk(i, f, na), 0))],
            out_specs=pl.BlockSpec((rows, d), lambda i, f, te, na: (i, 0)),
            scratch_shapes=[pltpu.VMEM((rows, d), BF16), pltpu.VMEM((rows, d), F32)],
        ),
        out_shape=jax.ShapeDtypeStruct((m, d), F32),
        compiler_params=_params(("arbitrary", "arbitrary")),
        name="ffn_moe",
    )(tile_expert, n_active, xs, w_gate, w_up, w_down)


def _combine_kernel(slot_ref, x_ref, route_ref, ys_ref, g_ref, b_ref, o_ref, y1_ref, y2_ref, sem):
    rows = x_ref.shape[0]

    def issue(r, carry):
        for k, dst in enumerate((y1_ref, y2_ref)):
            s = slot_ref[0, 0, 2 * r + k]
            pltpu.make_async_copy(ys_ref.at[pl.ds(s, 1)], dst.at[pl.ds(r, 1)], sem).start()
        return carry

    lax.fori_loop(0, rows, issue, 0)
    for dst in (y1_ref, y2_ref):
        pltpu.make_async_copy(ys_ref.at[pl.ds(0, rows)], dst, sem).wait()
    route = route_ref[...]
    moe = route[:, 4:5] * y1_ref[...] + route[:, 5:6] * y2_ref[...]
    o_ref[...] = _layernorm(DEEPNORM_ALPHA * x_ref[...] + moe, g_ref[...], b_ref[...])


def _combine_ln(x2d, route, slots, ys, g, b, *, rows):
    n, d = x2d.shape
    g2, b2 = g.reshape(1, -1), b.reshape(1, -1)
    return pl.pallas_call(
        _combine_kernel,
        grid=(n // rows,),
        in_specs=[pl.BlockSpec((1, 1, 2 * rows), lambda i: (i, 0, 0), memory_space=pltpu.SMEM),
                  pl.BlockSpec((rows, d), lambda i: (i, 0)),
                  pl.BlockSpec((rows, N_EXPERTS), lambda i: (i, 0)),
                  pl.BlockSpec(memory_space=pl.ANY),
                  pl.BlockSpec((1, d), lambda i: (0, 0)),
                  pl.BlockSpec((1, d), lambda i: (0, 0))],
        out_specs=pl.BlockSpec((rows, d), lambda i: (i, 0)),
        out_shape=jax.ShapeDtypeStruct((n, d), F32),
        scratch_shapes=[pltpu.VMEM((rows, d), F32), pltpu.VMEM((rows, d), F32), pltpu.SemaphoreType.DMA],
        compiler_params=_params(("arbitrary",)),
        name="moe_combine",
    )(slots, x2d, route, ys, g2, b2)


def _moe(x2d, route, counts, w_gate, w_up, w_down, g, b, *, rows, fchunk):
    n, _ = x2d.shape
    counts = counts.reshape(-1).astype(jnp.int32)
    padded = (counts + rows - 1) // rows * rows
    ends = jnp.cumsum(padded)
    offs = ends - padded
    n_tiles = (2 * n) // rows + N_EXPERTS
    starts = jnp.arange(n_tiles, dtype=jnp.int32) * rows
    tile_expert = jnp.minimum(jnp.sum(ends[None, :] <= starts[:, None], axis=1), N_EXPERTS - 1).astype(jnp.int32)
    n_active = (ends[-1:] // rows).astype(jnp.int32)
    expert = route[:, 0:2].astype(jnp.int32)
    rank = route[:, 2:4].astype(jnp.int32)
    base = jnp.sum(jnp.where(expert[..., None] == jnp.arange(N_EXPERTS), offs, 0), axis=-1)
    slots = (base + rank).reshape(n // rows, 1, 2 * rows)
    xs = _dispatch(x2d, slots, n_tiles * rows, rows=rows)
    ys = _ffn_grouped(xs, tile_expert, n_active, w_gate, w_up, w_down, rows=rows, fchunk=fchunk)
    return _combine_ln(x2d, route, slots, ys, g, b, rows=rows)


def _layer_even(x2d, positions, bsz, seq, w_in, gm_ln_g, gm_ln_b, w_s, b_s, q_norm, w_uq, kv_norm, w_ukv, w_out,
                ln_g, ln_b, f_gate, f_up, f_down, f_ln_g, f_ln_b, *, rows, ffn_rows, fchunk, attn_blk):
    cos_p, sin_p = _rope_tables(positions)
    ya, q, k, v = _mixer_in(x2d, cos_p, sin_p, w_in, gm_ln_g, gm_ln_b, w_s, b_s, q_norm, w_uq, kv_norm, w_ukv,
                            rows=rows)
    yb = _attention(q, k, v, bsz=bsz, seq=seq, blk=attn_blk)
    wo = w_out.astype(BF16)
    x2d = _proj_ln([ya, yb], [wo[:A_WIDTH], wo[A_WIDTH:]], x2d, ln_g, ln_b, rows=rows, name="mixer_out")
    return _ffn(x2d, f_gate.astype(BF16), f_up.astype(BF16), f_down.astype(BF16), f_ln_g, f_ln_b, rows=ffn_rows,
                fchunk=fchunk)


def _layer_odd(x2d, bsz, seq, w_in, a_re, a_im, log_step, b_re, b_im, c_re, c_im, d_skip, glu_w, glu_b, w_out, ln_g,
               ln_b, router, m_gate, m_up, m_down, m_ln_g, m_ln_b, *, rows, ffn_rows, fchunk):
    n, d = x2d.shape
    n_grp = d // S5_GROUP_DIM
    lch = S5_CHUNK
    n_chunks = seq // lch
    u2d = _matmul(x2d, w_in.astype(BF16), rows=rows, name="s5_in")
    tw, vv, lam = _s5_weights(a_re, a_im, log_step, b_re, b_im, c_re, c_im)
    uc = u2d.reshape(bsz, n_chunks, lch, n_grp, S5_GROUP_DIM).transpose(3, 1, 0, 2, 4)
    uc = uc.reshape(n_grp, n_chunks * bsz, lch * S5_GROUP_DIM).astype(BF16)
    yc = _s5_scan(uc, tw, vv, lam, bsz=bsz)
    y2d = yc.reshape(n_grp, n_chunks, bsz, lch, S5_GROUP_DIM).transpose(2, 1, 3, 0, 4).reshape(n, d)
    x2d, route, counts = _s5_out(y2d, u2d, d_skip, glu_w, glu_b, w_out, x2d, ln_g, ln_b, router, rows=rows)
    return _moe(x2d, route, counts, m_gate.astype(BF16), m_up.astype(BF16), m_down.astype(BF16), m_ln_g, m_ln_b,
                rows=ffn_rows, fchunk=fchunk)


def kernel(x, positions, ab_w_in, gm_ln_g, gm_ln_b, gm_w_s, gm_b_s, mla_q_norm, mla_w_uq, mla_kv_norm, mla_w_ukv, ab_w_out, ab_ln_g, ab_ln_b, ffd_w_gate, ffd_w_up, ffd_w_down, ffd_ln_g, ffd_ln_b, c_w_in, s5_a_re, s5_a_im, s5_log_step, s5_b_re, s5_b_im, s5_c_re, s5_c_im, s5_d, glu_w, glu_b, c_w_out, c_ln_g, c_ln_b, moe_router, moe_w_gate, moe_w_up, moe_w_down, moe_ln_g, moe_ln_b):
    bsz, seq, d = x.shape
    x2d = x.reshape(bsz * seq, d)
    rows = min(512, seq)
    for i in range(DEPTH):
        j = i // 2
        if i % 2 == 0:
            x2d = _layer_even(x2d, positions, bsz, seq, ab_w_in[j], gm_ln_g[j], gm_ln_b[j], gm_w_s[j], gm_b_s[j],
                              mla_q_norm[j], mla_w_uq[j], mla_kv_norm[j], mla_w_ukv[j], ab_w_out[j], ab_ln_g[j],
                              ab_ln_b[j], ffd_w_gate[j], ffd_w_up[j], ffd_w_down[j], ffd_ln_g[j], ffd_ln_b[j],
                              rows=rows, ffn_rows=rows, fchunk=ffd_w_gate.shape[2] // 2, attn_blk=min(512, seq))
        else:
            x2d = _layer_odd(x2d, bsz, seq, c_w_in[j], s5_a_re[j], s5_a_im[j], s5_log_step[j], s5_b_re[j],
                             s5_b_im[j], s5_c_re[j], s5_c_im[j], s5_d[j], glu_w[j], glu_b[j], c_w_out[j], c_ln_g[j],
                             c_ln_b[j], moe_router[j], moe_w_gate[j], moe_w_up[j], moe_w_down[j], moe_ln_g[j],
                             moe_ln_b[j], rows=rows, ffn_rows=rows, fchunk=moe_w_gate.shape[3] // 4)
    return x2d.reshape(bsz, seq, d)
```

```python
import functools
import math

import jax
import jax.numpy as jnp
from jax import lax
from jax.experimental import pallas as pl
from jax.experimental.pallas import tpu as pltpu

F32 = jnp.float32
BF16 = jnp.bfloat16

A_GROUPS = 4
A_GROUP_DIM = 128
A_WIDTH = A_GROUPS * A_GROUP_DIM
A_CHUNK = 128
MLA_HEADS = 8
QK_NOPE = 64
QK_ROPE = 32
QK_HEAD = QK_NOPE + QK_ROPE
V_HEAD = 64
Q_LORA = 384
KV_LORA = 256
ROPE_THETA = 10000.0
S5_GROUP_DIM = 16
S5_STATE = 64
N_EXPERTS = 8
LN_EPS = 1e-5
RMS_EPS = 1e-6
DEPTH = 2
DEEPNORM_ALPHA = (2.0 * DEPTH) ** 0.25

LANES = 128
HEAD_PAD = LANES
S5_GROUPS_PER_BLOCK = 2 * LANES // S5_GROUP_DIM
S5_STEPS = 16
S5_SCAN_LANES = 4 * LANES
VMEM_LIMIT = 56 * 1024 * 1024
NEG_BIG = -1e30


def _params(sem):
    return pltpu.CompilerParams(dimension_semantics=sem, vmem_limit_bytes=VMEM_LIMIT)


def _gelu(x):
    c = math.sqrt(2.0 / math.pi)
    return 0.5 * x * (1.0 + jnp.tanh(c * (x + 0.044715 * (x * x * x))))


def _sigmoid(x):
    return 1.0 / (1.0 + jnp.exp(-x))


def _layernorm(x, g, b):
    mu = jnp.mean(x, axis=-1, keepdims=True)
    xc = x - mu
    var = jnp.mean(xc * xc, axis=-1, keepdims=True)
    return xc * lax.rsqrt(var + LN_EPS) * g + b


def _rmsnorm(x, g):
    ms = jnp.mean(x * x, axis=-1, keepdims=True)
    return x * lax.rsqrt(ms + RMS_EPS) * g


def _dot(a, b):
    return jnp.dot(a, b, preferred_element_type=F32)


def _rope_table_kernel(inv_ref, pos_ref, cos_ref, sin_ref):
    pos = pos_ref[...].astype(F32)
    for j in range(QK_ROPE // 2):
        ang = pos * inv_ref[j]
        cos_ref[j] = jnp.cos(ang)
        sin_ref[j] = jnp.sin(ang)


def _rope_tables(positions):
    n = positions.size
    half = QK_ROPE // 2
    inv_freq = 1.0 / (ROPE_THETA ** (jnp.arange(0, QK_ROPE, 2, dtype=F32) / QK_ROPE))
    pos2d = positions.reshape(n // LANES, LANES)
    cos_t, sin_t = pl.pallas_call(
        _rope_table_kernel,
        out_shape=(jax.ShapeDtypeStruct((half, n // LANES, LANES), F32),) * 2,
        in_specs=[pl.BlockSpec(memory_space=pltpu.SMEM), pl.BlockSpec(memory_space=pltpu.VMEM)],
        out_specs=(pl.BlockSpec(memory_space=pltpu.VMEM),) * 2,
        name="rope_table",
    )(inv_freq, pos2d)
    cos_c = cos_t.reshape(half, n).T
    sin_c = sin_t.reshape(half, n).T
    ones = jnp.ones((n, QK_NOPE), F32)
    zeros_n = jnp.zeros((n, QK_NOPE), F32)
    zeros_p = jnp.zeros((n, HEAD_PAD - QK_HEAD), F32)
    cos_p = jnp.concatenate([ones, cos_c, cos_c, zeros_p], axis=1)
    sin_p = jnp.concatenate([zeros_n, sin_c, sin_c, zeros_p], axis=1)
    return cos_p, sin_p


def _mixer_in_kernel(x_ref, cos_ref, sin_ref, w_in_ref, lng_ref, lnb_ref, ws_ref, bs_ref, qn_ref, wq_ref,
                     kvn_ref, wkv_ref, ya_ref, q_ref, k_ref, v_ref):
    rows = x_ref.shape[0]
    xb = x_ref[...].astype(BF16)
    h = _dot(xb, w_in_ref[...])
    o_q = 2 * A_WIDTH
    o_kv = o_q + Q_LORA
    o_pe = o_kv + KV_LORA
    o_rot = o_pe + HEAD_PAD
    cos_p = cos_ref[...]
    sin_p = sin_ref[...]

    a_u = _gelu(h[:, :A_WIDTH])
    a_v = _gelu(h[:, A_WIDTH:o_q])
    vn = _layernorm(a_v, lng_ref[...], lnb_ref[...]).astype(BF16)
    t_idx = lax.broadcasted_iota(jnp.int32, (A_CHUNK, A_CHUNK), 0)
    s_idx = lax.broadcasted_iota(jnp.int32, (A_CHUNK, A_CHUNK), 1)
    causal = s_idx <= t_idx
    bs = bs_ref[...]
    for g in range(A_GROUPS):
        w_g = jnp.where(causal, ws_ref[g], 0.0).astype(BF16)
        cols = slice(g * A_GROUP_DIM, (g + 1) * A_GROUP_DIM)
        for c in range(rows // A_CHUNK):
            rws = slice(c * A_CHUNK, (c + 1) * A_CHUNK)
            mixed = _dot(w_g, vn[rws, cols]) + bs[:, cols]
            ya_ref[rws, cols] = (a_u[rws, cols] * mixed).astype(BF16)

    cqn = _rmsnorm(h[:, o_q:o_kv], qn_ref[...]).astype(BF16)
    q2 = _dot(cqn, wq_ref[...])
    half = MLA_HEADS * HEAD_PAD
    for hd in range(MLA_HEADS):
        cols = slice(hd * HEAD_PAD, (hd + 1) * HEAD_PAD)
        rot = slice(half + hd * HEAD_PAD, half + (hd + 1) * HEAD_PAD)
        q_ref[:, cols] = (q2[:, cols] * cos_p + q2[:, rot] * sin_p).astype(BF16)

    ckvn = _rmsnorm(h[:, o_kv:o_pe], kvn_ref[...]).astype(BF16)
    kv = _dot(ckvn, wkv_ref[...])
    kpe = h[:, o_pe:o_rot] * cos_p + h[:, o_rot:o_rot + HEAD_PAD] * sin_p
    for hd in range(MLA_HEADS):
        cols = slice(hd * HEAD_PAD, (hd + 1) * HEAD_PAD)
        k_ref[:, cols] = (kv[:, cols] + kpe).astype(BF16)
    v_ref[...] = kv[:, half:].astype(BF16)


def _mixer_in(x2d, cos_p, sin_p, w_in, gm_ln_g, gm_ln_b, w_s, b_s, q_norm, w_uq, kv_norm, w_ukv, *, rows):
    n, d = x2d.shape
    hp = MLA_HEADS * HEAD_PAD
    o_pe = 2 * A_WIDTH + Q_LORA + KV_LORA
    half = QK_ROPE // 2
    w_pe = w_in[:, o_pe:o_pe + QK_ROPE]
    w_pe_rot = jnp.concatenate([-w_pe[:, half:], w_pe[:, :half]], axis=1)
    pad_l = jnp.zeros((d, QK_NOPE), F32)
    pad_r = jnp.zeros((d, HEAD_PAD - QK_HEAD), F32)
    w_in_p = jnp.concatenate([w_in[:, :o_pe], pad_l, w_pe, pad_r, pad_l, w_pe_rot, pad_r], axis=1).astype(BF16)
    wq = w_uq.reshape(Q_LORA, MLA_HEADS, QK_HEAD)
    wq_pe = wq[:, :, QK_NOPE:]
    wq_rot = jnp.concatenate([jnp.zeros((Q_LORA, MLA_HEADS, QK_NOPE), F32), -wq_pe[:, :, half:], wq_pe[:, :, :half]],
                             axis=2)
    padq = ((0, 0), (0, 0), (0, HEAD_PAD - QK_HEAD))
    wq2 = jnp.concatenate([jnp.pad(wq, padq).reshape(Q_LORA, hp), jnp.pad(wq_rot, padq).reshape(Q_LORA, hp)],
                          axis=1).astype(BF16)
    wkv = w_ukv.reshape(KV_LORA, MLA_HEADS, QK_NOPE + V_HEAD)
    wk = jnp.pad(wkv[:, :, :QK_NOPE], ((0, 0), (0, 0), (0, HEAD_PAD - QK_NOPE))).reshape(KV_LORA, hp)
    wv = wkv[:, :, QK_NOPE:].reshape(KV_LORA, MLA_HEADS * V_HEAD)
    wkv2 = jnp.concatenate([wk, wv], axis=1).astype(BF16)
    bs_full = jnp.repeat(b_s.T, A_GROUP_DIM, axis=1)

    full = lambda a: pl.BlockSpec(a.shape, lambda i: (0,) * a.ndim)
    row = lambda w: pl.BlockSpec((rows, w), lambda i: (i, 0))
    args = (x2d, cos_p, sin_p, w_in_p, gm_ln_g.reshape(1, -1), gm_ln_b.reshape(1, -1), w_s, bs_full,
            q_norm.reshape(1, -1), wq2, kv_norm.reshape(1, -1), wkv2)
    in_specs = [row(d), row(HEAD_PAD), row(HEAD_PAD)] + [full(a) for a in args[3:]]
    return pl.pallas_call(
        _mixer_in_kernel,
        grid=(n // rows,),
        in_specs=in_specs,
        out_specs=(row(A_WIDTH), row(hp), row(hp), row(MLA_HEADS * V_HEAD)),
        out_shape=(jax.ShapeDtypeStruct((n, A_WIDTH), BF16), jax.ShapeDtypeStruct((n, hp), BF16),
                   jax.ShapeDtypeStruct((n, hp), BF16), jax.ShapeDtypeStruct((n, MLA_HEADS * V_HEAD), BF16)),
        compiler_params=_params(("parallel",)),
        name="mixer_in",
    )(*args)


def _attn_kernel(q_ref, k_ref, v_ref, o_ref, *, blk):
    seq = q_ref.shape[0]
    scale = QK_HEAD ** -0.5
    row = lax.broadcasted_iota(jnp.int32, (blk, blk), 0)
    col = lax.broadcasted_iota(jnp.int32, (blk, blk), 1)
    diag_mask = col <= row
    first_head_lanes = lax.broadcasted_iota(jnp.int32, (blk, 2 * V_HEAD), 1) < V_HEAD

    qk = lambda a, b: lax.dot_general(a, b, (((1,), (1,)), ((), ())), preferred_element_type=F32)

    for j in range(seq // blk):
        q0 = j * blk
        outs = []
        for hh in range(2):
            cols = slice(hh * HEAD_PAD, (hh + 1) * HEAD_PAD)
            q = q_ref[q0:q0 + blk, cols]
            s_d = jnp.where(diag_mask, qk(q, k_ref[q0:q0 + blk, cols]), NEG_BIG)
            m = jnp.max(s_d, axis=-1, keepdims=True)
            if j > 0:
                s_o = qk(q, k_ref[0:q0, cols])
                m = jnp.maximum(m, jnp.max(s_o, axis=-1, keepdims=True))
            p_d = jnp.exp((s_d - m) * scale)
            l = jnp.sum(p_d, axis=-1, keepdims=True)
            acc = _dot(p_d.astype(BF16), v_ref[q0:q0 + blk, :])
            if j > 0:
                p_o = jnp.exp((s_o - m) * scale)
                l = l + jnp.sum(p_o, axis=-1, keepdims=True)
                acc = acc + _dot(p_o.astype(BF16), v_ref[0:q0, :])
            outs.append(acc / l)
        o_ref[q0:q0 + blk, :] = jnp.where(first_head_lanes, outs[0], outs[1]).astype(BF16)


def _attention(q, k, v, *, bsz, seq, blk):
    n = bsz * seq
    pair = 2 * HEAD_PAD
    return pl.pallas_call(
        functools.partial(_attn_kernel, blk=blk),
        grid=(bsz, MLA_HEADS // 2),
        in_specs=[pl.BlockSpec((seq, pair), lambda b, h: (b, h)), pl.BlockSpec((seq, pair), lambda b, h: (b, h)),
                  pl.BlockSpec((seq, 2 * V_HEAD), lambda b, h: (b, h))],
        out_specs=pl.BlockSpec((seq, 2 * V_HEAD), lambda b, h: (b, h)),
        out_shape=jax.ShapeDtypeStruct((n, MLA_HEADS * V_HEAD), BF16),
        compiler_params=_params(("parallel", "parallel")),
        name="attention",
    )(q, k, v)


def _proj_ln_kernel(*refs, n_in):
    ins = refs[:n_in]
    ws = refs[n_in:2 * n_in]
    x_ref, g_ref, b_ref, o_ref = refs[2 * n_in:]
    acc = DEEPNORM_ALPHA * x_ref[...]
    for a_ref, w_ref in zip(ins, ws):
        acc = acc + _dot(a_ref[...], w_ref[...])
    o_ref[...] = _layernorm(acc, g_ref[...], b_ref[...])


def _proj_ln(ins, ws, x2d, g, b, *, rows, name):
    n, d = x2d.shape
    full = lambda a: pl.BlockSpec(a.shape, lambda i: (0,) * a.ndim)
    row = lambda w: pl.BlockSpec((rows, w), lambda i: (i, 0))
    g2, b2 = g.reshape(1, -1), b.reshape(1, -1)
    return pl.pallas_call(
        functools.partial(_proj_ln_kernel, n_in=len(ins)),
        grid=(n // rows,),
        in_specs=[row(a.shape[1]) for a in ins] + [full(w) for w in ws] + [row(d), full(g2), full(b2)],
        out_specs=row(d),
        out_shape=jax.ShapeDtypeStruct((n, d), F32),
        compiler_params=_params(("parallel",)),
        name=name,
    )(*ins, *ws, x2d, g2, b2)


def _ffn_kernel(x_ref, wg_ref, wu_ref, wd_ref, g_ref, b_ref, o_ref, xb_ref, acc_ref):
    f = pl.program_id(1)

    @pl.when(f == 0)
    def _():
        xb_ref[...] = x_ref[...].astype(BF16)
        acc_ref[...] = jnp.zeros_like(acc_ref)

    xb = xb_ref[...]
    gate = _dot(xb, wg_ref[...])
    up = _dot(xb, wu_ref[...])
    hid = (gate * _sigmoid(gate) * up).astype(BF16)
    acc_ref[...] += _dot(hid, wd_ref[...])

    @pl.when(f == pl.num_programs(1) - 1)
    def _():
        o_ref[...] = _layernorm(DEEPNORM_ALPHA * x_ref[...] + acc_ref[...], g_ref[...], b_ref[...])


def _ffn(x2d, w_gate, w_up, w_down, g, b, *, rows, fchunk, bsz):
    n, d = x2d.shape
    dff = w_gate.shape[1]
    n_s = n // bsz // rows
    g2, b2 = g.reshape(1, -1), b.reshape(1, -1)
    out = pl.pallas_call(
        _ffn_kernel,
        grid=(n // rows, dff // fchunk),
        in_specs=[pl.BlockSpec((rows, d), lambda i, f: (i, 0)),
                  pl.BlockSpec((d, fchunk), lambda i, f: (0, f)),
                  pl.BlockSpec((d, fchunk), lambda i, f: (0, f)),
                  pl.BlockSpec((fchunk, d), lambda i, f: (f, 0)),
                  pl.BlockSpec((1, d), lambda i, f: (0, 0)),
                  pl.BlockSpec((1, d), lambda i, f: (0, 0))],
        out_specs=pl.BlockSpec((rows, d), lambda i, f: (i % n_s, i // n_s)),
        out_shape=jax.ShapeDtypeStruct((n // bsz, bsz * d), F32),
        scratch_shapes=[pltpu.VMEM((rows, d), BF16), pltpu.VMEM((rows, d), F32)],
        compiler_params=_params(("parallel", "arbitrary")),
        name="ffn_dense",
    )(x2d, w_gate, w_up, w_down, g2, b2)
    return out.reshape(n, d)


def _s5_kernel(x_ref, win_ref, bd_ref, cd_ref, lam_ref, d_ref, y_ref, bur_ref, bui_ref, hr_ref, hi_ref, st_ref, *,
               bsz):
    @pl.when(pl.program_id(0) == 0)
    def _():
        st_ref[...] = jnp.zeros_like(st_ref)

    n_blk, cb, sb2 = bd_ref.shape
    sb = sb2 // 2
    steps = x_ref.shape[0] // bsz
    u = _dot(x_ref[...].astype(BF16), win_ref[...])
    ub = u.astype(BF16)
    for q in range(n_blk):
        bu = _dot(ub[:, q * cb:(q + 1) * cb], bd_ref[q])
        bur_ref[:, q * sb:(q + 1) * sb] = bu[:, :sb]
        bui_ref[:, q * sb:(q + 1) * sb] = bu[:, sb:]
    for c in range(bur_ref.shape[1] // S5_SCAN_LANES):
        lanes = slice(c * S5_SCAN_LANES, (c + 1) * S5_SCAN_LANES)
        a_re = lam_ref[0:1, lanes]
        a_im = lam_ref[1:2, lanes]
        h_re = st_ref[0, :, lanes]
        h_im = st_ref[1, :, lanes]
        for t in range(steps):
            rws = slice(t * bsz, (t + 1) * bsz)
            n_re = a_re * h_re - a_im * h_im + bur_ref[rws, lanes]
            n_im = a_re * h_im + a_im * h_re + bui_ref[rws, lanes]
            hr_ref[rws, lanes] = n_re.astype(BF16)
            hi_ref[rws, lanes] = n_im.astype(BF16)
            h_re, h_im = n_re, n_im
        st_ref[0, :, lanes] = h_re
        st_ref[1, :, lanes] = h_im
    for q in range(n_blk):
        st = slice(q * sb, (q + 1) * sb)
        ch = slice(q * cb, (q + 1) * cb)
        y = _dot(hr_ref[:, st], cd_ref[q, :sb]) + _dot(hi_ref[:, st], cd_ref[q, sb:])
        y_ref[:, ch] = y + d_ref[:, ch] * u[:, ch]


def _s5(x_tb, w_in, bd, cd, lam, d_skip, *, bsz):
    n, d = x_tb.shape
    rows = S5_STEPS * bsz
    n_state = lam.shape[1]
    full = lambda a: pl.BlockSpec(a.shape, lambda i: (0,) * a.ndim)
    d2 = d_skip.reshape(1, -1)
    return pl.pallas_call(
        functools.partial(_s5_kernel, bsz=bsz),
        grid=(n // rows,),
        in_specs=[pl.BlockSpec((rows, d), lambda i: (i, 0)), full(w_in), full(bd), full(cd), full(lam), full(d2)],
        out_specs=pl.BlockSpec((rows, d), lambda i: (i, 0)),
        out_shape=jax.ShapeDtypeStruct((n, d), F32),
        scratch_shapes=[pltpu.VMEM((rows, n_state), F32), pltpu.VMEM((rows, n_state), F32),
                        pltpu.VMEM((rows, n_state), BF16), pltpu.VMEM((rows, n_state), BF16),
                        pltpu.VMEM((2, bsz, n_state), F32)],
        compiler_params=_params(("arbitrary",)),
        name="s5",
    )(x_tb, w_in, bd, cd, lam, d2)


def _s5_weights(a_re, a_im, log_step, b_re, b_im, c_re, c_im):
    n_grp, n_st = a_re.shape
    gpb = S5_GROUPS_PER_BLOCK
    n_blk = n_grp // gpb
    delta = jnp.exp(log_step)[:, None]
    mag = jnp.exp(delta * a_re)
    abar_re = mag * jnp.cos(delta * a_im)
    abar_im = mag * jnp.sin(delta * a_im)
    den = a_re * a_re + a_im * a_im
    coef_re = ((abar_re - 1.0) * a_re + abar_im * a_im) / den
    coef_im = (abar_im * a_re - (abar_re - 1.0) * a_im) / den
    bb_re = coef_re[..., None] * b_re - coef_im[..., None] * b_im
    bb_im = coef_re[..., None] * b_im + coef_im[..., None] * b_re
    same_group = jnp.eye(gpb, dtype=F32)[None, :, None, :, None]

    def expand(w):
        w = jnp.transpose(w.reshape(n_blk, gpb, w.shape[1], w.shape[2]), (0, 1, 3, 2))
        w = w[:, :, :, None, :] * same_group
        return w.reshape(n_blk, gpb * w.shape[2], gpb * w.shape[4])

    bd = jnp.concatenate([expand(bb_re), expand(bb_im)], axis=2).astype(BF16)
    cd = jnp.concatenate([expand(c_re), -expand(c_im)], axis=1).astype(BF16)
    lam = jnp.stack([abar_re.reshape(-1), abar_im.reshape(-1)])
    return bd, cd, lam


def _s5_out_kernel(y_ref, gw_ref, gb_ref, wo_ref, x_ref, g_ref, b_ref, r_ref, o_ref, route_ref, count_ref,
                   carry_ref):
    @pl.when(pl.program_id(0) == 0)
    def _():
        carry_ref[...] = jnp.zeros_like(carry_ref)

    rows = x_ref.shape[0]
    gl = _gelu(y_ref[...])
    z = gl * _sigmoid(_dot(gl.astype(BF16), gw_ref[...]) + gb_ref[...])
    mix = _dot(z.astype(BF16), wo_ref[...])
    xo = _layernorm(DEEPNORM_ALPHA * x_ref[...] + mix, g_ref[...], b_ref[...])
    o_ref[...] = xo
    logits = jnp.dot(xo, r_ref[...], preferred_element_type=F32, precision=lax.Precision.HIGHEST)
    idx = lax.broadcasted_iota(jnp.int32, logits.shape, 1)
    m1 = jnp.max(logits, axis=-1, keepdims=True)
    i1 = jnp.min(jnp.where(logits == m1, idx, N_EXPERTS), axis=-1, keepdims=True)
    rest = jnp.where(idx == i1, -jnp.inf, logits)
    m2 = jnp.max(rest, axis=-1, keepdims=True)
    i2 = jnp.min(jnp.where(rest == m2, idx, N_EXPERTS), axis=-1, keepdims=True)
    e2 = jnp.exp(m2 - m1)
    g1 = 1.0 / (1.0 + e2)
    g2 = e2 * g1
    sel = jnp.where((idx == i1) | (idx == i2), 1.0, 0.0)
    t_r = lax.broadcasted_iota(jnp.int32, (rows, rows), 0)
    t_c = lax.broadcasted_iota(jnp.int32, (rows, rows), 1)
    earlier = jnp.where(t_c < t_r, 1.0, 0.0).astype(BF16)
    before = _dot(earlier, sel.astype(BF16)) + carry_ref[...]
    r1 = jnp.sum(jnp.where(idx == i1, before, 0.0), axis=-1, keepdims=True)
    r2 = jnp.sum(jnp.where(idx == i2, before, 0.0), axis=-1, keepdims=True)
    total = carry_ref[...] + jnp.sum(sel, axis=0, keepdims=True)
    carry_ref[...] = total
    count_ref[...] = total
    fields = (i1.astype(F32), i2.astype(F32), r1, r2, g1, g2)
    route = jnp.zeros(logits.shape, F32)
    for lane, val in enumerate(fields):
        route = jnp.where(idx == lane, val, route)
    route_ref[...] = route


def _s5_out(y2d, glu_w, glu_b, w_out, x2d, g, b, router, *, rows):
    n, d = x2d.shape
    full = lambda a: pl.BlockSpec(a.shape, lambda i: (0,) * a.ndim)
    row = lambda w: pl.BlockSpec((rows, w), lambda i: (i, 0))
    args = (y2d, glu_w.astype(BF16), glu_b.reshape(1, -1), w_out.astype(BF16), x2d, g.reshape(1, -1),
            b.reshape(1, -1), router)
    in_specs = [row(d), full(args[1]), full(args[2]), full(args[3]), row(d), full(args[5]), full(args[6]),
                full(args[7])]
    return pl.pallas_call(
        _s5_out_kernel,
        grid=(n // rows,),
        in_specs=in_specs,
        out_specs=(row(d), row(N_EXPERTS), pl.BlockSpec((1, N_EXPERTS), lambda i: (0, 0))),
        out_shape=(jax.ShapeDtypeStruct((n, d), F32), jax.ShapeDtypeStruct((n, N_EXPERTS), F32),
                   jax.ShapeDtypeStruct((1, N_EXPERTS), F32)),
        scratch_shapes=[pltpu.VMEM((1, N_EXPERTS), F32)],
        compiler_params=_params(("arbitrary",)),
        name="s5_out",
    )(*args)


def _dispatch_kernel(slot_ref, x_ref, xs_in_ref, xs_ref, sem):
    del xs_in_ref
    rows = x_ref.shape[0]

    def issue(r, carry):
        for k in range(2):
            s = slot_ref[0, 0, 2 * r + k]
            pltpu.make_async_copy(x_ref.at[pl.ds(r, 1)], xs_ref.at[pl.ds(s, 1)], sem).start()
        return carry

    lax.fori_loop(0, rows, issue, 0)
    for k in range(2):
        pltpu.make_async_copy(x_ref, xs_ref.at[pl.ds(0, rows)], sem).wait()


def _dispatch(x2d, slots, n_slots, *, rows):
    n, d = x2d.shape
    return pl.pallas_call(
        _dispatch_kernel,
        grid=(n // rows,),
        in_specs=[pl.BlockSpec((1, 1, 2 * rows), lambda i: (i, 0, 0), memory_space=pltpu.SMEM),
                  pl.BlockSpec((rows, d), lambda i: (i, 0)),
                  pl.BlockSpec(memory_space=pl.ANY)],
        out_specs=pl.BlockSpec(memory_space=pl.ANY),
        out_shape=jax.ShapeDtypeStruct((n_slots, d), F32),
        scratch_shapes=[pltpu.SemaphoreType.DMA],
        input_output_aliases={2: 0},
        compiler_params=_params(("arbitrary",)),
        name="moe_dispatch",
    )(slots, x2d, jnp.zeros((n_slots, d), F32))


def _ffn_grouped_kernel(te_ref, na_ref, x_ref, wg_ref, wu_ref, wd_ref, o_ref, xb_ref, acc_ref):
    del te_ref
    i = pl.program_id(0)
    f = pl.program_id(1)
    active = i < na_ref[0]

    @pl.when(active)
    def _():
        @pl.when(f == 0)
        def _():
            xb_ref[...] = x_ref[...].astype(BF16)

        xb = xb_ref[...]
        gate = _dot(xb, wg_ref[0])
        up = _dot(xb, wu_ref[0])
        hid = (gate * _sigmoid(gate) * up).astype(BF16)
        contrib = _dot(hid, wd_ref[0])

        @pl.when(f == 0)
        def _():
            acc_ref[...] = contrib

        @pl.when(f > 0)
        def _():
            acc_ref[...] += contrib

    @pl.when(f == pl.num_programs(1) - 1)
    def _():
        o_ref[...] = jnp.where(active, acc_ref[...], 0.0)


def _ffn_grouped(xs, tile_expert, n_active, w_gate, w_up, w_down, *, rows, fchunk):
    m, d = xs.shape
    n_f = w_gate.shape[2] // fchunk
    tile = lambda i, na: jnp.maximum(jnp.minimum(i, na[0] - 1), 0)
    chunk = lambda i, f, na: jnp.where(i < na[0], f, n_f - 1)
    return pl.pallas_call(
        _ffn_grouped_kernel,
        grid_spec=pltpu.PrefetchScalarGridSpec(
            num_scalar_prefetch=2,
            grid=(m // rows, n_f),
            in_specs=[pl.BlockSpec((rows, d), lambda i, f, te, na: (tile(i, na), 0)),
                      pl.BlockSpec((1, d, fchunk), lambda i, f, te, na: (te[tile(i, na)], 0, chunk(i, f, na))),
                      pl.BlockSpec((1, d, fchunk), lambda i, f, te, na: (te[tile(i, na)], 0, chunk(i, f, na))),
                      pl.BlockSpec((1, fchunk, d), lambda i, f, te, na: (te[tile(i, na)], chunk(i, f, na), 0))],
            out_specs=pl.BlockSpec((rows, d), lambda i, f, te, na: (i, 0)),
            scratch_shapes=[pltpu.VMEM((rows, d), BF16), pltpu.VMEM((rows, d), F32)],
        ),
        out_shape=jax.ShapeDtypeStruct((m, d), F32),
        compiler_params=_params(("arbitrary", "arbitrary")),
        name="ffn_moe",
    )(tile_expert, n_active, xs, w_gate, w_up, w_down)


def _combine_kernel(slot_ref, x_ref, route_ref, ys_ref, g_ref, b_ref, o_ref, y1_ref, y2_ref, sem):
    rows = x_ref.shape[0]

    def issue(r, carry):
        for k, dst in enumerate((y1_ref, y2_ref)):
            s = slot_ref[0, 0, 2 * r + k]
            pltpu.make_async_copy(ys_ref.at[pl.ds(s, 1)], dst.at[pl.ds(r, 1)], sem).start()
        return carry

    lax.fori_loop(0, rows, issue, 0)
    for dst in (y1_ref, y2_ref):
        pltpu.make_async_copy(ys_ref.at[pl.ds(0, rows)], dst, sem).wait()
    route = route_ref[...]
    moe = route[:, 4:5] * y1_ref[...] + route[:, 5:6] * y2_ref[...]
    o_ref[...] = _layernorm(DEEPNORM_ALPHA * x_ref[...] + moe, g_ref[...], b_ref[...])


def _combine_ln(x_tb, route, slots, ys, g, b, *, rows, bsz):
    n, d = x_tb.shape
    n_s = n // bsz // rows
    x2d = x_tb.reshape(n // bsz, bsz * d)
    g2, b2 = g.reshape(1, -1), b.reshape(1, -1)
    return pl.pallas_call(
        _combine_kernel,
        grid=(n // rows,),
        in_specs=[pl.BlockSpec((1, 1, 2 * rows), lambda i: (i, 0, 0), memory_space=pltpu.SMEM),
                  pl.BlockSpec((rows, d), lambda i: (i % n_s, i // n_s)),
                  pl.BlockSpec((rows, N_EXPERTS), lambda i: (i, 0)),
                  pl.BlockSpec(memory_space=pl.ANY),
                  pl.BlockSpec((1, d), lambda i: (0, 0)),
                  pl.BlockSpec((1, d), lambda i: (0, 0))],
        out_specs=pl.BlockSpec((rows, d), lambda i: (i, 0)),
        out_shape=jax.ShapeDtypeStruct((n, d), F32),
        scratch_shapes=[pltpu.VMEM((rows, d), F32), pltpu.VMEM((rows, d), F32), pltpu.SemaphoreType.DMA],
        compiler_params=_params(("arbitrary",)),
        name="moe_combine",
    )(slots, x2d, route, ys, g2, b2)


def _moe(x2d, route, counts, w_gate, w_up, w_down, g, b, *, rows, fchunk, bsz):
    n, _ = x2d.shape
    counts = counts.reshape(-1).astype(jnp.int32)
    padded = (counts + rows - 1) // rows * rows
    ends = jnp.cumsum(padded)
    offs = ends - padded
    n_tiles = (2 * n) // rows + N_EXPERTS
    starts = jnp.arange(n_tiles, dtype=jnp.int32) * rows
    tile_expert = jnp.minimum(jnp.sum(ends[None, :] <= starts[:, None], axis=1), N_EXPERTS - 1).astype(jnp.int32)
    n_active = (ends[-1:] // rows).astype(jnp.int32)
    expert = route[:, 0:2].astype(jnp.int32)
    rank = route[:, 2:4].astype(jnp.int32)
    base = jnp.sum(jnp.where(expert[..., None] == jnp.arange(N_EXPERTS), offs, 0), axis=-1)
    slots = base + rank
    to_bt = lambda a: jnp.transpose(a.reshape(n // bsz, bsz, a.shape[1]), (1, 0, 2)).reshape(n, a.shape[1])
    tiles = lambda a: a.reshape(n // rows, 1, 2 * rows)
    xs = _dispatch(x2d, tiles(slots), n_tiles * rows, rows=rows)
    ys = _ffn_grouped(xs, tile_expert, n_active, w_gate, w_up, w_down, rows=rows, fchunk=fchunk)
    return _combine_ln(x2d, to_bt(route), tiles(to_bt(slots)), ys, g, b, rows=rows, bsz=bsz)


def _layer_even(x2d, positions, bsz, seq, w_in, gm_ln_g, gm_ln_b, w_s, b_s, q_norm, w_uq, kv_norm, w_ukv, w_out,
                ln_g, ln_b, f_gate, f_up, f_down, f_ln_g, f_ln_b, *, rows, ffn_rows, fchunk, attn_blk):
    cos_p, sin_p = _rope_tables(positions)
    ya, q, k, v = _mixer_in(x2d, cos_p, sin_p, w_in, gm_ln_g, gm_ln_b, w_s, b_s, q_norm, w_uq, kv_norm, w_ukv,
                            rows=rows)
    yb = _attention(q, k, v, bsz=bsz, seq=seq, blk=attn_blk)
    wo = w_out.astype(BF16)
    x2d = _proj_ln([ya, yb], [wo[:A_WIDTH], wo[A_WIDTH:]], x2d, ln_g, ln_b, rows=rows, name="mixer_out")
    return _ffn(x2d, f_gate.astype(BF16), f_up.astype(BF16), f_down.astype(BF16), f_ln_g, f_ln_b, rows=ffn_rows,
                fchunk=fchunk, bsz=bsz)


def _layer_odd(x_tb, bsz, w_in, a_re, a_im, log_step, b_re, b_im, c_re, c_im, d_skip, glu_w, glu_b, w_out, ln_g, ln_b,
               router, m_gate, m_up, m_down, m_ln_g, m_ln_b, *, rows, ffn_rows, fchunk):
    bd, cd, lam = _s5_weights(a_re, a_im, log_step, b_re, b_im, c_re, c_im)
    y_tb = _s5(x_tb, w_in.astype(BF16), bd, cd, lam, d_skip, bsz=bsz)
    x_tb, route, counts = _s5_out(y_tb, glu_w, glu_b, w_out, x_tb, ln_g, ln_b, router, rows=rows)
    return _moe(x_tb, route, counts, m_gate.astype(BF16), m_up.astype(BF16), m_down.astype(BF16), m_ln_g, m_ln_b,
                rows=ffn_rows, fchunk=fchunk, bsz=bsz)


def kernel(x, positions, ab_w_in, gm_ln_g, gm_ln_b, gm_w_s, gm_b_s, mla_q_norm, mla_w_uq, mla_kv_norm, mla_w_ukv, ab_w_out, ab_ln_g, ab_ln_b, ffd_w_gate, ffd_w_up, ffd_w_down, ffd_ln_g, ffd_ln_b, c_w_in, s5_a_re, s5_a_im, s5_log_step, s5_b_re, s5_b_im, s5_c_re, s5_c_im, s5_d, glu_w, glu_b, c_w_out, c_ln_g, c_ln_b, moe_router, moe_w_gate, moe_w_up, moe_w_down, moe_ln_g, moe_ln_b):
    bsz, seq, d = x.shape
    assert DEPTH % 2 == 0
    x2d = x.reshape(bsz * seq, d)
    rows = min(512, seq)
    for i in range(DEPTH):
        j = i // 2
        if i % 2 == 0:
            x2d = _layer_even(x2d, positions, bsz, seq, ab_w_in[j], gm_ln_g[j], gm_ln_b[j], gm_w_s[j], gm_b_s[j],
                              mla_q_norm[j], mla_w_uq[j], mla_kv_norm[j], mla_w_ukv[j], ab_w_out[j], ab_ln_g[j],
                              ab_ln_b[j], ffd_w_gate[j], ffd_w_up[j], ffd_w_down[j], ffd_ln_g[j], ffd_ln_b[j],
                              rows=rows, ffn_rows=rows, fchunk=ffd_w_gate.shape[2] // 2, attn_blk=min(512, seq))
        else:
            x2d = _layer_odd(x2d, bsz, c_w_in[j], s5_a_re[j], s5_a_im[j], s5_log_step[j], s5_b_re[j],
                             s5_b_im[j], s5_c_re[j], s5_c_im[j], s5_d[j], glu_w[j], glu_b[j], c_w_out[j], c_ln_g[j],
                             c_ln_b[j], moe_router[j], moe_w_gate[j], moe_w_up[j], moe_w_down[j], moe_ln_g[j],
                             moe_ln_b[j], rows=rows, ffn_rows=rows, fchunk=moe_w_gate.shape[3] // 4)
    return x2d.reshape(bsz, seq, d)
```

```python
import functools
import math

import jax
import jax.numpy as jnp
from jax import lax
from jax.experimental import pallas as pl
from jax.experimental.pallas import tpu as pltpu

F32 = jnp.float32
BF16 = jnp.bfloat16

A_GROUPS = 4
A_GROUP_DIM = 128
A_WIDTH = A_GROUPS * A_GROUP_DIM
A_CHUNK = 128
MLA_HEADS = 8
QK_NOPE = 64
QK_ROPE = 32
QK_HEAD = QK_NOPE + QK_ROPE
V_HEAD = 64
Q_LORA = 384
KV_LORA = 256
ROPE_THETA = 10000.0
S5_GROUP_DIM = 16
S5_STATE = 64
N_EXPERTS = 8
LN_EPS = 1e-5
RMS_EPS = 1e-6
DEPTH = 2
DEEPNORM_ALPHA = (2.0 * DEPTH) ** 0.25

LANES = 128
HEAD_PAD = LANES
S5_GROUPS_PER_BLOCK = 2 * LANES // S5_GROUP_DIM
S5_STEPS = 16
S5_SCAN_LANES = 4 * LANES
VMEM_LIMIT = 56 * 1024 * 1024
NEG_BIG = -1e30
FFN_CHUNK = 512
DMA_UNROLL = 8


def _params(sem):
    return pltpu.CompilerParams(dimension_semantics=sem, vmem_limit_bytes=VMEM_LIMIT)


def _gelu(x):
    c = math.sqrt(2.0 / math.pi)
    return 0.5 * x * (1.0 + jnp.tanh(c * (x + 0.044715 * (x * x * x))))


def _sigmoid(x):
    return 1.0 / (1.0 + jnp.exp(-x))


def _layernorm(x, g, b):
    mu = jnp.mean(x, axis=-1, keepdims=True)
    xc = x - mu
    var = jnp.mean(xc * xc, axis=-1, keepdims=True)
    return xc * lax.rsqrt(var + LN_EPS) * g + b


def _rmsnorm(x, g):
    ms = jnp.mean(x * x, axis=-1, keepdims=True)
    return x * lax.rsqrt(ms + RMS_EPS) * g


def _dot(a, b):
    return jnp.dot(a, b, preferred_element_type=F32)


def _rope_table_kernel(inv_ref, pos_ref, cos_ref, sin_ref):
    pos = pos_ref[...].astype(F32)
    for j in range(QK_ROPE // 2):
        ang = pos * inv_ref[j]
        cos_ref[j] = jnp.cos(ang)
        sin_ref[j] = jnp.sin(ang)


def _rope_tables(positions):
    n = positions.size
    half = QK_ROPE // 2
    inv_freq = 1.0 / (ROPE_THETA ** (jnp.arange(0, QK_ROPE, 2, dtype=F32) / QK_ROPE))
    pos2d = positions.reshape(n // LANES, LANES)
    cos_t, sin_t = pl.pallas_call(
        _rope_table_kernel,
        out_shape=(jax.ShapeDtypeStruct((half, n // LANES, LANES), F32),) * 2,
        in_specs=[pl.BlockSpec(memory_space=pltpu.SMEM), pl.BlockSpec(memory_space=pltpu.VMEM)],
        out_specs=(pl.BlockSpec(memory_space=pltpu.VMEM),) * 2,
        name="rope_table",
    )(inv_freq, pos2d)
    cos_c = cos_t.reshape(half, n).T
    sin_c = sin_t.reshape(half, n).T
    ones = jnp.ones((n, QK_NOPE), F32)
    zeros_n = jnp.zeros((n, QK_NOPE), F32)
    zeros_p = jnp.zeros((n, HEAD_PAD - QK_HEAD), F32)
    cos_p = jnp.concatenate([ones, cos_c, cos_c, zeros_p], axis=1)
    sin_p = jnp.concatenate([zeros_n, sin_c, sin_c, zeros_p], axis=1)
    return cos_p, sin_p


def _mixer_in_kernel(x_ref, cos_ref, sin_ref, w_in_ref, lng_ref, lnb_ref, ws_ref, bs_ref, qn_ref, wq_ref,
                     kvn_ref, wkv_ref, ya_ref, q_ref, k_ref, v_ref):
    rows = x_ref.shape[0]
    xb = x_ref[...].astype(BF16)
    h = _dot(xb, w_in_ref[...])
    o_q = 2 * A_WIDTH
    o_kv = o_q + Q_LORA
    o_pe = o_kv + KV_LORA
    o_rot = o_pe + HEAD_PAD
    cos_p = cos_ref[...]
    sin_p = sin_ref[...]

    a_u = _gelu(h[:, :A_WIDTH])
    a_v = _gelu(h[:, A_WIDTH:o_q])
    vn = _layernorm(a_v, lng_ref[...], lnb_ref[...]).astype(BF16)
    t_idx = lax.broadcasted_iota(jnp.int32, (A_CHUNK, A_CHUNK), 0)
    s_idx = lax.broadcasted_iota(jnp.int32, (A_CHUNK, A_CHUNK), 1)
    causal = s_idx <= t_idx
    bs = bs_ref[...]
    for g in range(A_GROUPS):
        w_g = jnp.where(causal, ws_ref[g], 0.0).astype(BF16)
        cols = slice(g * A_GROUP_DIM, (g + 1) * A_GROUP_DIM)
        for c in range(rows // A_CHUNK):
            rws = slice(c * A_CHUNK, (c + 1) * A_CHUNK)
            mixed = _dot(w_g, vn[rws, cols]) + bs[:, cols]
            ya_ref[rws, cols] = (a_u[rws, cols] * mixed).astype(BF16)

    cqn = _rmsnorm(h[:, o_q:o_kv], qn_ref[...]).astype(BF16)
    q2 = _dot(cqn, wq_ref[...])
    half = MLA_HEADS * HEAD_PAD
    for hd in range(MLA_HEADS):
        cols = slice(hd * HEAD_PAD, (hd + 1) * HEAD_PAD)
        rot = slice(half + hd * HEAD_PAD, half + (hd + 1) * HEAD_PAD)
        q_ref[:, cols] = (q2[:, cols] * cos_p + q2[:, rot] * sin_p).astype(BF16)

    ckvn = _rmsnorm(h[:, o_kv:o_pe], kvn_ref[...]).astype(BF16)
    kv = _dot(ckvn, wkv_ref[...])
    kpe = h[:, o_pe:o_rot] * cos_p + h[:, o_rot:o_rot + HEAD_PAD] * sin_p
    for hd in range(MLA_HEADS):
        cols = slice(hd * HEAD_PAD, (hd + 1) * HEAD_PAD)
        k_ref[:, cols] = (kv[:, cols] + kpe).astype(BF16)
    v_ref[...] = kv[:, half:].astype(BF16)


def _mixer_in(x2d, cos_p, sin_p, w_in, gm_ln_g, gm_ln_b, w_s, b_s, q_norm, w_uq, kv_norm, w_ukv, *, rows):
    n, d = x2d.shape
    hp = MLA_HEADS * HEAD_PAD
    o_pe = 2 * A_WIDTH + Q_LORA + KV_LORA
    half = QK_ROPE // 2
    w_pe = w_in[:, o_pe:o_pe + QK_ROPE]
    w_pe_rot = jnp.concatenate([-w_pe[:, half:], w_pe[:, :half]], axis=1)
    pad_l = jnp.zeros((d, QK_NOPE), F32)
    pad_r = jnp.zeros((d, HEAD_PAD - QK_HEAD), F32)
    w_in_p = jnp.concatenate([w_in[:, :o_pe], pad_l, w_pe, pad_r, pad_l, w_pe_rot, pad_r], axis=1).astype(BF16)
    wq = w_uq.reshape(Q_LORA, MLA_HEADS, QK_HEAD)
    wq_pe = wq[:, :, QK_NOPE:]
    wq_rot = jnp.concatenate([jnp.zeros((Q_LORA, MLA_HEADS, QK_NOPE), F32), -wq_pe[:, :, half:], wq_pe[:, :, :half]],
                             axis=2)
    padq = ((0, 0), (0, 0), (0, HEAD_PAD - QK_HEAD))
    wq2 = jnp.concatenate([jnp.pad(wq, padq).reshape(Q_LORA, hp), jnp.pad(wq_rot, padq).reshape(Q_LORA, hp)],
                          axis=1).astype(BF16)
    wkv = w_ukv.reshape(KV_LORA, MLA_HEADS, QK_NOPE + V_HEAD)
    wk = jnp.pad(wkv[:, :, :QK_NOPE], ((0, 0), (0, 0), (0, HEAD_PAD - QK_NOPE))).reshape(KV_LORA, hp)
    wv = wkv[:, :, QK_NOPE:].reshape(KV_LORA, MLA_HEADS * V_HEAD)
    wkv2 = jnp.concatenate([wk, wv], axis=1).astype(BF16)
    bs_full = jnp.repeat(b_s.T, A_GROUP_DIM, axis=1)

    full = lambda a: pl.BlockSpec(a.shape, lambda i: (0,) * a.ndim)
    row = lambda w: pl.BlockSpec((rows, w), lambda i: (i, 0))
    args = (x2d, cos_p, sin_p, w_in_p, gm_ln_g.reshape(1, -1), gm_ln_b.reshape(1, -1), w_s, bs_full,
            q_norm.reshape(1, -1), wq2, kv_norm.reshape(1, -1), wkv2)
    in_specs = [row(d), row(HEAD_PAD), row(HEAD_PAD)] + [full(a) for a in args[3:]]
    return pl.pallas_call(
        _mixer_in_kernel,
        grid=(n // rows,),
        in_specs=in_specs,
        out_specs=(row(A_WIDTH), row(hp), row(hp), row(MLA_HEADS * V_HEAD)),
        out_shape=(jax.ShapeDtypeStruct((n, A_WIDTH), BF16), jax.ShapeDtypeStruct((n, hp), BF16),
                   jax.ShapeDtypeStruct((n, hp), BF16), jax.ShapeDtypeStruct((n, MLA_HEADS * V_HEAD), BF16)),
        compiler_params=_params(("parallel",)),
        name="mixer_in",
    )(*args)


def _attn_kernel(q_ref, k_ref, v_ref, o_ref, *, blk):
    seq = q_ref.shape[0]
    scale = QK_HEAD ** -0.5
    row = lax.broadcasted_iota(jnp.int32, (blk, blk), 0)
    col = lax.broadcasted_iota(jnp.int32, (blk, blk), 1)
    diag_mask = col <= row
    first_head_lanes = lax.broadcasted_iota(jnp.int32, (blk, 2 * V_HEAD), 1) < V_HEAD

    qk = lambda a, b: lax.dot_general(a, b, (((1,), (1,)), ((), ())), preferred_element_type=F32)

    for j in range(seq // blk):
        q0 = j * blk
        outs = []
        for hh in range(2):
            cols = slice(hh * HEAD_PAD, (hh + 1) * HEAD_PAD)
            q = q_ref[q0:q0 + blk, cols]
            s_d = jnp.where(diag_mask, qk(q, k_ref[q0:q0 + blk, cols]), NEG_BIG)
            m = jnp.max(s_d, axis=-1, keepdims=True)
            if j > 0:
                s_o = qk(q, k_ref[0:q0, cols])
                m = jnp.maximum(m, jnp.max(s_o, axis=-1, keepdims=True))
            p_d = jnp.exp((s_d - m) * scale)
            l = jnp.sum(p_d, axis=-1, keepdims=True)
            acc = _dot(p_d.astype(BF16), v_ref[q0:q0 + blk, :])
            if j > 0:
                p_o = jnp.exp((s_o - m) * scale)
                l = l + jnp.sum(p_o, axis=-1, keepdims=True)
                acc = acc + _dot(p_o.astype(BF16), v_ref[0:q0, :])
            outs.append(acc / l)
        o_ref[q0:q0 + blk, :] = jnp.where(first_head_lanes, outs[0], outs[1]).astype(BF16)


def _attention(q, k, v, *, bsz, seq, blk):
    n = bsz * seq
    pair = 2 * HEAD_PAD
    return pl.pallas_call(
        functools.partial(_attn_kernel, blk=blk),
        grid=(bsz, MLA_HEADS // 2),
        in_specs=[pl.BlockSpec((seq, pair), lambda b, h: (b, h)), pl.BlockSpec((seq, pair), lambda b, h: (b, h)),
                  pl.BlockSpec((seq, 2 * V_HEAD), lambda b, h: (b, h))],
        out_specs=pl.BlockSpec((seq, 2 * V_HEAD), lambda b, h: (b, h)),
        out_shape=jax.ShapeDtypeStruct((n, MLA_HEADS * V_HEAD), BF16),
        compiler_params=_params(("parallel", "parallel")),
        name="attention",
    )(q, k, v)


def _proj_ln_kernel(*refs, n_in):
    ins = refs[:n_in]
    ws = refs[n_in:2 * n_in]
    x_ref, g_ref, b_ref, o_ref = refs[2 * n_in:]
    acc = DEEPNORM_ALPHA * x_ref[...]
    for a_ref, w_ref in zip(ins, ws):
        acc = acc + _dot(a_ref[...], w_ref[...])
    o_ref[...] = _layernorm(acc, g_ref[...], b_ref[...])


def _proj_ln(ins, ws, x2d, g, b, *, rows, name):
    n, d = x2d.shape
    full = lambda a: pl.BlockSpec(a.shape, lambda i: (0,) * a.ndim)
    row = lambda w: pl.BlockSpec((rows, w), lambda i: (i, 0))
    g2, b2 = g.reshape(1, -1), b.reshape(1, -1)
    return pl.pallas_call(
        functools.partial(_proj_ln_kernel, n_in=len(ins)),
        grid=(n // rows,),
        in_specs=[row(a.shape[1]) for a in ins] + [full(w) for w in ws] + [row(d), full(g2), full(b2)],
        out_specs=row(d),
        out_shape=jax.ShapeDtypeStruct((n, d), F32),
        compiler_params=_params(("parallel",)),
        name=name,
    )(*ins, *ws, x2d, g2, b2)


def _swiglu_chunks(xb, wg_ref, wu_ref, wd_ref):
    dff = wg_ref.shape[1]
    acc = None
    for c0 in range(0, dff, FFN_CHUNK):
        cols = slice(c0, min(c0 + FFN_CHUNK, dff))
        gate = _dot(xb, wg_ref[:, cols])
        up = _dot(xb, wu_ref[:, cols])
        hid = (gate * _sigmoid(gate) * up).astype(BF16)
        part = _dot(hid, wd_ref[cols, :])
        acc = part if acc is None else acc + part
    return acc


def _ffn_kernel(x_ref, wg_ref, wu_ref, wd_ref, g_ref, b_ref, o_ref):
    x = x_ref[...]
    ffn = _swiglu_chunks(x.astype(BF16), wg_ref, wu_ref, wd_ref)
    o_ref[...] = _layernorm(DEEPNORM_ALPHA * x + ffn, g_ref[...], b_ref[...])


def _ffn(x2d, w_gate, w_up, w_down, g, b, *, rows, bsz):
    n, d = x2d.shape
    n_s = n // bsz // rows
    g2, b2 = g.reshape(1, -1), b.reshape(1, -1)
    once = lambda a: pl.BlockSpec(a.shape, lambda i: (0,) * a.ndim, pipeline_mode=pl.Buffered(1))
    out = pl.pallas_call(
        _ffn_kernel,
        grid=(n // rows,),
        in_specs=[pl.BlockSpec((rows, d), lambda i: (i, 0)), once(w_gate), once(w_up), once(w_down), once(g2),
                  once(b2)],
        out_specs=pl.BlockSpec((rows, d), lambda i: (i % n_s, i // n_s)),
        out_shape=jax.ShapeDtypeStruct((n // bsz, bsz * d), F32),
        compiler_params=_params(("parallel",)),
        name="ffn_dense",
    )(x2d, w_gate, w_up, w_down, g2, b2)
    return out.reshape(n, d)


def _s5_kernel(x_ref, win_ref, bd_ref, cd_ref, lam_ref, d_ref, y_ref, bur_ref, bui_ref, hr_ref, hi_ref, st_ref, *,
               bsz):
    @pl.when(pl.program_id(0) == 0)
    def _():
        st_ref[...] = jnp.zeros_like(st_ref)

    n_blk, cb, sb2 = bd_ref.shape
    sb = sb2 // 2
    steps = x_ref.shape[0] // bsz
    u = _dot(x_ref[...].astype(BF16), win_ref[...])
    ub = u.astype(BF16)
    for q in range(n_blk):
        bu = _dot(ub[:, q * cb:(q + 1) * cb], bd_ref[q])
        bur_ref[:, q * sb:(q + 1) * sb] = bu[:, :sb]
        bui_ref[:, q * sb:(q + 1) * sb] = bu[:, sb:]
    for c in range(bur_ref.shape[1] // S5_SCAN_LANES):
        lanes = slice(c * S5_SCAN_LANES, (c + 1) * S5_SCAN_LANES)
        a_re = lam_ref[0:1, lanes]
        a_im = lam_ref[1:2, lanes]
        h_re = st_ref[0, :, lanes]
        h_im = st_ref[1, :, lanes]
        for t in range(steps):
            rws = slice(t * bsz, (t + 1) * bsz)
            n_re = a_re * h_re - a_im * h_im + bur_ref[rws, lanes]
            n_im = a_re * h_im + a_im * h_re + bui_ref[rws, lanes]
            hr_ref[rws, lanes] = n_re.astype(BF16)
            hi_ref[rws, lanes] = n_im.astype(BF16)
            h_re, h_im = n_re, n_im
        st_ref[0, :, lanes] = h_re
        st_ref[1, :, lanes] = h_im
    for q in range(n_blk):
        st = slice(q * sb, (q + 1) * sb)
        ch = slice(q * cb, (q + 1) * cb)
        y = _dot(hr_ref[:, st], cd_ref[q, :sb]) + _dot(hi_ref[:, st], cd_ref[q, sb:])
        y_ref[:, ch] = y + d_ref[:, ch] * u[:, ch]


def _s5(x_tb, w_in, bd, cd, lam, d_skip, *, bsz):
    n, d = x_tb.shape
    rows = S5_STEPS * bsz
    n_state = lam.shape[1]
    full = lambda a: pl.BlockSpec(a.shape, lambda i: (0,) * a.ndim)
    d2 = d_skip.reshape(1, -1)
    return pl.pallas_call(
        functools.partial(_s5_kernel, bsz=bsz),
        grid=(n // rows,),
        in_specs=[pl.BlockSpec((rows, d), lambda i: (i, 0)), full(w_in), full(bd), full(cd), full(lam), full(d2)],
        out_specs=pl.BlockSpec((rows, d), lambda i: (i, 0)),
        out_shape=jax.ShapeDtypeStruct((n, d), F32),
        scratch_shapes=[pltpu.VMEM((rows, n_state), F32), pltpu.VMEM((rows, n_state), F32),
                        pltpu.VMEM((rows, n_state), BF16), pltpu.VMEM((rows, n_state), BF16),
                        pltpu.VMEM((2, bsz, n_state), F32)],
        compiler_params=_params(("arbitrary",)),
        name="s5",
    )(x_tb, w_in, bd, cd, lam, d2)


def _s5_weights(a_re, a_im, log_step, b_re, b_im, c_re, c_im):
    n_grp, n_st = a_re.shape
    gpb = S5_GROUPS_PER_BLOCK
    n_blk = n_grp // gpb
    delta = jnp.exp(log_step)[:, None]
    mag = jnp.exp(delta * a_re)
    abar_re = mag * jnp.cos(delta * a_im)
    abar_im = mag * jnp.sin(delta * a_im)
    den = a_re * a_re + a_im * a_im
    coef_re = ((abar_re - 1.0) * a_re + abar_im * a_im) / den
    coef_im = (abar_im * a_re - (abar_re - 1.0) * a_im) / den
    bb_re = coef_re[..., None] * b_re - coef_im[..., None] * b_im
    bb_im = coef_re[..., None] * b_im + coef_im[..., None] * b_re
    same_group = jnp.eye(gpb, dtype=F32)[None, :, None, :, None]

    def expand(w):
        w = jnp.transpose(w.reshape(n_blk, gpb, w.shape[1], w.shape[2]), (0, 1, 3, 2))
        w = w[:, :, :, None, :] * same_group
        return w.reshape(n_blk, gpb * w.shape[2], gpb * w.shape[4])

    bd = jnp.concatenate([expand(bb_re), expand(bb_im)], axis=2).astype(BF16)
    cd = jnp.concatenate([expand(c_re), -expand(c_im)], axis=1).astype(BF16)
    lam = jnp.stack([abar_re.reshape(-1), abar_im.reshape(-1)])
    return bd, cd, lam


def _s5_out_kernel(y_ref, gw_ref, gb_ref, wo_ref, x_ref, g_ref, b_ref, r_ref, o_ref, route_ref, count_ref,
                   carry_ref):
    @pl.when(pl.program_id(0) == 0)
    def _():
        carry_ref[...] = jnp.zeros_like(carry_ref)

    rows = x_ref.shape[0]
    gl = _gelu(y_ref[...])
    z = gl * _sigmoid(_dot(gl.astype(BF16), gw_ref[...]) + gb_ref[...])
    mix = _dot(z.astype(BF16), wo_ref[...])
    xo = _layernorm(DEEPNORM_ALPHA * x_ref[...] + mix, g_ref[...], b_ref[...])
    o_ref[...] = xo
    logits = jnp.dot(xo, r_ref[...], preferred_element_type=F32, precision=lax.Precision.HIGHEST)
    idx = lax.broadcasted_iota(jnp.int32, logits.shape, 1)
    m1 = jnp.max(logits, axis=-1, keepdims=True)
    i1 = jnp.min(jnp.where(logits == m1, idx, N_EXPERTS), axis=-1, keepdims=True)
    rest = jnp.where(idx == i1, -jnp.inf, logits)
    m2 = jnp.max(rest, axis=-1, keepdims=True)
    i2 = jnp.min(jnp.where(rest == m2, idx, N_EXPERTS), axis=-1, keepdims=True)
    e2 = jnp.exp(m2 - m1)
    g1 = 1.0 / (1.0 + e2)
    g2 = e2 * g1
    sel = jnp.where((idx == i1) | (idx == i2), 1.0, 0.0)
    t_r = lax.broadcasted_iota(jnp.int32, (rows, rows), 0)
    t_c = lax.broadcasted_iota(jnp.int32, (rows, rows), 1)
    earlier = jnp.where(t_c < t_r, 1.0, 0.0).astype(BF16)
    before = _dot(earlier, sel.astype(BF16)) + carry_ref[...]
    r1 = jnp.sum(jnp.where(idx == i1, before, 0.0), axis=-1, keepdims=True)
    r2 = jnp.sum(jnp.where(idx == i2, before, 0.0), axis=-1, keepdims=True)
    total = carry_ref[...] + jnp.sum(sel, axis=0, keepdims=True)
    carry_ref[...] = total
    count_ref[...] = total
    fields = (i1.astype(F32), i2.astype(F32), r1, r2, g1, g2)
    route = jnp.zeros(logits.shape, F32)
    for lane, val in enumerate(fields):
        route = jnp.where(idx == lane, val, route)
    route_ref[...] = route


def _s5_out(y2d, glu_w, glu_b, w_out, x2d, g, b, router, *, rows):
    n, d = x2d.shape
    full = lambda a: pl.BlockSpec(a.shape, lambda i: (0,) * a.ndim)
    row = lambda w: pl.BlockSpec((rows, w), lambda i: (i, 0))
    args = (y2d, glu_w.astype(BF16), glu_b.reshape(1, -1), w_out.astype(BF16), x2d, g.reshape(1, -1),
            b.reshape(1, -1), router)
    in_specs = [row(d), full(args[1]), full(args[2]), full(args[3]), row(d), full(args[5]), full(args[6]),
                full(args[7])]
    return pl.pallas_call(
        _s5_out_kernel,
        grid=(n // rows,),
        in_specs=in_specs,
        out_specs=(row(d), row(N_EXPERTS), pl.BlockSpec((1, N_EXPERTS), lambda i: (0, 0))),
        out_shape=(jax.ShapeDtypeStruct((n, d), F32), jax.ShapeDtypeStruct((n, N_EXPERTS), F32),
                   jax.ShapeDtypeStruct((1, N_EXPERTS), F32)),
        scratch_shapes=[pltpu.VMEM((1, N_EXPERTS), F32)],
        compiler_params=_params(("arbitrary",)),
        name="s5_out",
    )(*args)


def _dispatch_kernel(slot_ref, x_ref, xs_in_ref, xs_ref, sem):
    del xs_in_ref
    rows = x_ref.shape[0]

    def issue(r, carry):
        for k in range(2):
            s = slot_ref[0, 0, 2 * r + k]
            pltpu.make_async_copy(x_ref.at[pl.ds(r, 1)], xs_ref.at[pl.ds(s, 1)], sem).start()
        return carry

    lax.fori_loop(0, rows, issue, 0, unroll=DMA_UNROLL)
    for k in range(2):
        pltpu.make_async_copy(x_ref, xs_ref.at[pl.ds(0, rows)], sem).wait()


def _dispatch(x2d, slots, n_slots, *, rows):
    n, d = x2d.shape
    return pl.pallas_call(
        _dispatch_kernel,
        grid=(n // rows,),
        in_specs=[pl.BlockSpec((1, 1, 2 * rows), lambda i: (i, 0, 0), memory_space=pltpu.SMEM),
                  pl.BlockSpec((rows, d), lambda i: (i, 0)),
                  pl.BlockSpec(memory_space=pl.ANY)],
        out_specs=pl.BlockSpec(memory_space=pl.ANY),
        out_shape=jax.ShapeDtypeStruct((n_slots, d), F32),
        scratch_shapes=[pltpu.SemaphoreType.DMA],
        input_output_aliases={2: 0},
        compiler_params=_params(("arbitrary",)),
        name="moe_dispatch",
    )(slots, x2d, jnp.zeros((n_slots, d), F32))


def _ffn_grouped_kernel(te_ref, na_ref, x_ref, wg_ref, wu_ref, wd_ref, o_ref, xb_ref, acc_ref):
    del te_ref
    i = pl.program_id(0)
    f = pl.program_id(1)
    active = i < na_ref[0]

    @pl.when(active)
    def _():
        @pl.when(f == 0)
        def _():
            xb_ref[...] = x_ref[...].astype(BF16)

        contrib = _swiglu_chunks(xb_ref[...], wg_ref.at[0], wu_ref.at[0], wd_ref.at[0])

        @pl.when(f == 0)
        def _():
            acc_ref[...] = contrib

        @pl.when(f > 0)
        def _():
            acc_ref[...] += contrib

    @pl.when(f == pl.num_programs(1) - 1)
    def _():
        o_ref[...] = jnp.where(active, acc_ref[...], 0.0)


def _ffn_grouped(xs, tile_expert, n_active, w_gate, w_up, w_down, *, rows, fchunk):
    m, d = xs.shape
    n_f = w_gate.shape[2] // fchunk
    tile = lambda i, na: jnp.maximum(jnp.minimum(i, na[0] - 1), 0)
    chunk = lambda i, f, na: jnp.where(i < na[0], f, n_f - 1)
    return pl.pallas_call(
        _ffn_grouped_kernel,
        grid_spec=pltpu.PrefetchScalarGridSpec(
            num_scalar_prefetch=2,
            grid=(m // rows, n_f),
            in_specs=[pl.BlockSpec((rows, d), lambda i, f, te, na: (tile(i, na), 0)),
                      pl.BlockSpec((1, d, fchunk), lambda i, f, te, na: (te[tile(i, na)], 0, chunk(i, f, na))),
                      pl.BlockSpec((1, d, fchunk), lambda i, f, te, na: (te[tile(i, na)], 0, chunk(i, f, na))),
                      pl.BlockSpec((1, fchunk, d), lambda i, f, te, na: (te[tile(i, na)], chunk(i, f, na), 0))],
            out_specs=pl.BlockSpec((rows, d), lambda i, f, te, na: (i, 0)),
            scratch_shapes=[pltpu.VMEM((rows, d), BF16), pltpu.VMEM((rows, d), F32)],
        ),
        out_shape=jax.ShapeDtypeStruct((m, d), F32),
        compiler_params=_params(("arbitrary", "arbitrary")),
        name="ffn_moe",
    )(tile_expert, n_active, xs, w_gate, w_up, w_down)


def _combine_kernel(slot_ref, x_ref, route_ref, ys_ref, g_ref, b_ref, o_ref, y1_ref, y2_ref, sem):
    rows = x_ref.shape[0]

    def issue(r, carry):
        for k, dst in enumerate((y1_ref, y2_ref)):
            s = slot_ref[0, 0, 2 * r + k]
            pltpu.make_async_copy(ys_ref.at[pl.ds(s, 1)], dst.at[pl.ds(r, 1)], sem).start()
        return carry

    lax.fori_loop(0, rows, issue, 0, unroll=DMA_UNROLL)
    for dst in (y1_ref, y2_ref):
        pltpu.make_async_copy(ys_ref.at[pl.ds(0, rows)], dst, sem).wait()
    route = route_ref[...]
    moe = route[:, 4:5] * y1_ref[...] + route[:, 5:6] * y2_ref[...]
    o_ref[...] = _layernorm(DEEPNORM_ALPHA * x_ref[...] + moe, g_ref[...], b_ref[...])


def _combine_ln(x_tb, route, slots, ys, g, b, *, rows, bsz):
    n, d = x_tb.shape
    n_s = n // bsz // rows
    x2d = x_tb.reshape(n // bsz, bsz * d)
    g2, b2 = g.reshape(1, -1), b.reshape(1, -1)
    return pl.pallas_call(
        _combine_kernel,
        grid=(n // rows,),
        in_specs=[pl.BlockSpec((1, 1, 2 * rows), lambda i: (i, 0, 0), memory_space=pltpu.SMEM),
                  pl.BlockSpec((rows, d), lambda i: (i % n_s, i // n_s)),
                  pl.BlockSpec((rows, N_EXPERTS), lambda i: (i, 0)),
                  pl.BlockSpec(memory_space=pl.ANY),
                  pl.BlockSpec((1, d), lambda i: (0, 0)),
                  pl.BlockSpec((1, d), lambda i: (0, 0))],
        out_specs=pl.BlockSpec((rows, d), lambda i: (i, 0)),
        out_shape=jax.ShapeDtypeStruct((n, d), F32),
        scratch_shapes=[pltpu.VMEM((rows, d), F32), pltpu.VMEM((rows, d), F32), pltpu.SemaphoreType.DMA],
        compiler_params=_params(("arbitrary",)),
        name="moe_combine",
    )(slots, x2d, route, ys, g2, b2)


def _moe(x2d, route, counts, w_gate, w_up, w_down, g, b, *, rows, fchunk, bsz):
    n, _ = x2d.shape
    counts = counts.reshape(-1).astype(jnp.int32)
    padded = (counts + rows - 1) // rows * rows
    ends = jnp.cumsum(padded)
    offs = ends - padded
    n_tiles = (2 * n) // rows + N_EXPERTS
    starts = jnp.arange(n_tiles, dtype=jnp.int32) * rows
    tile_expert = jnp.minimum(jnp.sum(ends[None, :] <= starts[:, None], axis=1), N_EXPERTS - 1).astype(jnp.int32)
    n_active = (ends[-1:] // rows).astype(jnp.int32)
    expert = route[:, 0:2].astype(jnp.int32)
    rank = route[:, 2:4].astype(jnp.int32)
    base = jnp.sum(jnp.where(expert[..., None] == jnp.arange(N_EXPERTS), offs, 0), axis=-1)
    slots = base + rank
    to_bt = lambda a: jnp.transpose(a.reshape(n // bsz, bsz, a.shape[1]), (1, 0, 2)).reshape(n, a.shape[1])
    tiles = lambda a: a.reshape(n // rows, 1, 2 * rows)
    xs = _dispatch(x2d, tiles(slots), n_tiles * rows, rows=rows)
    ys = _ffn_grouped(xs, tile_expert, n_active, w_gate, w_up, w_down, rows=rows, fchunk=fchunk)
    return _combine_ln(x2d, to_bt(route), tiles(to_bt(slots)), ys, g, b, rows=rows, bsz=bsz)


def _layer_even(x2d, positions, bsz, seq, w_in, gm_ln_g, gm_ln_b, w_s, b_s, q_norm, w_uq, kv_norm, w_ukv, w_out,
                ln_g, ln_b, f_gate, f_up, f_down, f_ln_g, f_ln_b, *, rows, ffn_rows, attn_blk):
    cos_p, sin_p = _rope_tables(positions)
    ya, q, k, v = _mixer_in(x2d, cos_p, sin_p, w_in, gm_ln_g, gm_ln_b, w_s, b_s, q_norm, w_uq, kv_norm, w_ukv,
                            rows=rows)
    yb = _attention(q, k, v, bsz=bsz, seq=seq, blk=attn_blk)
    wo = w_out.astype(BF16)
    x2d = _proj_ln([ya, yb], [wo[:A_WIDTH], wo[A_WIDTH:]], x2d, ln_g, ln_b, rows=rows, name="mixer_out")
    return _ffn(x2d, f_gate.astype(BF16), f_up.astype(BF16), f_down.astype(BF16), f_ln_g, f_ln_b, rows=ffn_rows,
                bsz=bsz)


def _layer_odd(x_tb, bsz, w_in, a_re, a_im, log_step, b_re, b_im, c_re, c_im, d_skip, glu_w, glu_b, w_out, ln_g, ln_b,
               router, m_gate, m_up, m_down, m_ln_g, m_ln_b, *, rows, ffn_rows, fchunk):
    bd, cd, lam = _s5_weights(a_re, a_im, log_step, b_re, b_im, c_re, c_im)
    y_tb = _s5(x_tb, w_in.astype(BF16), bd, cd, lam, d_skip, bsz=bsz)
    x_tb, route, counts = _s5_out(y_tb, glu_w, glu_b, w_out, x_tb, ln_g, ln_b, router, rows=rows)
    return _moe(x_tb, route, counts, m_gate.astype(BF16), m_up.astype(BF16), m_down.astype(BF16), m_ln_g, m_ln_b,
                rows=ffn_rows, fchunk=fchunk, bsz=bsz)


def kernel(x, positions, ab_w_in, gm_ln_g, gm_ln_b, gm_w_s, gm_b_s, mla_q_norm, mla_w_uq, mla_kv_norm, mla_w_ukv, ab_w_out, ab_ln_g, ab_ln_b, ffd_w_gate, ffd_w_up, ffd_w_down, ffd_ln_g, ffd_ln_b, c_w_in, s5_a_re, s5_a_im, s5_log_step, s5_b_re, s5_b_im, s5_c_re, s5_c_im, s5_d, glu_w, glu_b, c_w_out, c_ln_g, c_ln_b, moe_router, moe_w_gate, moe_w_up, moe_w_down, moe_ln_g, moe_ln_b):
    bsz, seq, d = x.shape
    assert DEPTH % 2 == 0
    x2d = x.reshape(bsz * seq, d)
    rows = min(512, seq)
    for i in range(DEPTH):
        j = i // 2
        if i % 2 == 0:
            x2d = _layer_even(x2d, positions, bsz, seq, ab_w_in[j], gm_ln_g[j], gm_ln_b[j], gm_w_s[j], gm_b_s[j],
                              mla_q_norm[j], mla_w_uq[j], mla_kv_norm[j], mla_w_ukv[j], ab_w_out[j], ab_ln_g[j],
                              ab_ln_b[j], ffd_w_gate[j], ffd_w_up[j], ffd_w_down[j], ffd_ln_g[j], ffd_ln_b[j],
                              rows=rows, ffn_rows=rows, attn_blk=min(512, seq))
        else:
            x2d = _layer_odd(x2d, bsz, c_w_in[j], s5_a_re[j], s5_a_im[j], s5_log_step[j], s5_b_re[j],
                             s5_b_im[j], s5_c_re[j], s5_c_im[j], s5_d[j], glu_w[j], glu_b[j], c_w_out[j], c_ln_g[j],
                             c_ln_b[j], moe_router[j], moe_w_gate[j], moe_w_up[j], moe_w_down[j], moe_ln_g[j],
                             moe_ln_b[j], rows=rows, ffn_rows=rows, fchunk=moe_w_gate.shape[3] // 2)
    return x2d.reshape(bsz, seq, d)
```

```python
import functools
import math

import jax
import jax.numpy as jnp
from jax import lax
from jax.experimental import pallas as pl
from jax.experimental.pallas import tpu as pltpu

F32 = jnp.float32
BF16 = jnp.bfloat16

A_GROUPS = 4
A_GROUP_DIM = 128
A_WIDTH = A_GROUPS * A_GROUP_DIM
A_CHUNK = 128
MLA_HEADS = 8
QK_NOPE = 64
QK_ROPE = 32
QK_HEAD = QK_NOPE + QK_ROPE
V_HEAD = 64
Q_LORA = 384
KV_LORA = 256
ROPE_THETA = 10000.0
S5_GROUP_DIM = 16
S5_STATE = 64
N_EXPERTS = 8
LN_EPS = 1e-5
RMS_EPS = 1e-6
DEPTH = 2
DEEPNORM_ALPHA = (2.0 * DEPTH) ** 0.25

LANES = 128
HEAD_PAD = LANES
S5_GROUPS_PER_BLOCK = 2 * LANES // S5_GROUP_DIM
S5_STEPS = 16
S5_SCAN_LANES = 4 * LANES
VMEM_LIMIT = 56 * 1024 * 1024
NEG_BIG = -1e30
FFN_CHUNK = 512
DMA_UNROLL = 8


def _params(sem):
    return pltpu.CompilerParams(dimension_semantics=sem, vmem_limit_bytes=VMEM_LIMIT)


def _gelu(x):
    c = math.sqrt(2.0 / math.pi)
    return 0.5 * x * (1.0 + jnp.tanh(c * (x + 0.044715 * (x * x * x))))


def _sigmoid(x):
    return 1.0 / (1.0 + jnp.exp(-x))


def _layernorm(x, g, b):
    mu = jnp.mean(x, axis=-1, keepdims=True)
    xc = x - mu
    var = jnp.mean(xc * xc, axis=-1, keepdims=True)
    return xc * lax.rsqrt(var + LN_EPS) * g + b


def _rmsnorm(x, g):
    ms = jnp.mean(x * x, axis=-1, keepdims=True)
    return x * lax.rsqrt(ms + RMS_EPS) * g


def _dot(a, b):
    return jnp.dot(a, b, preferred_element_type=F32)


def _rope_table_kernel(inv_ref, pos_ref, cos_ref, sin_ref):
    pos = pos_ref[...].astype(F32)
    for j in range(QK_ROPE // 2):
        ang = pos * inv_ref[j]
        cos_ref[j] = jnp.cos(ang)
        sin_ref[j] = jnp.sin(ang)


def _rope_tables(positions):
    n = positions.size
    half = QK_ROPE // 2
    inv_freq = 1.0 / (ROPE_THETA ** (jnp.arange(0, QK_ROPE, 2, dtype=F32) / QK_ROPE))
    pos2d = positions.reshape(n // LANES, LANES)
    cos_t, sin_t = pl.pallas_call(
        _rope_table_kernel,
        out_shape=(jax.ShapeDtypeStruct((half, n // LANES, LANES), F32),) * 2,
        in_specs=[pl.BlockSpec(memory_space=pltpu.SMEM), pl.BlockSpec(memory_space=pltpu.VMEM)],
        out_specs=(pl.BlockSpec(memory_space=pltpu.VMEM),) * 2,
        name="rope_table",
    )(inv_freq, pos2d)
    cos_c = cos_t.reshape(half, n).T
    sin_c = sin_t.reshape(half, n).T
    ones = jnp.ones((n, QK_NOPE), F32)
    zeros_n = jnp.zeros((n, QK_NOPE), F32)
    zeros_p = jnp.zeros((n, HEAD_PAD - QK_HEAD), F32)
    cos_p = jnp.concatenate([ones, cos_c, cos_c, zeros_p], axis=1)
    sin_p = jnp.concatenate([zeros_n, sin_c, sin_c, zeros_p], axis=1)
    return cos_p, sin_p


def _mixer_in_kernel(x_ref, cos_ref, sin_ref, w_in_ref, lng_ref, lnb_ref, ws_ref, bs_ref, qn_ref, wq_ref,
                     kvn_ref, wkv_ref, ya_ref, q_ref, k_ref, v_ref):
    rows = x_ref.shape[0]
    xb = x_ref[...].astype(BF16)
    h = _dot(xb, w_in_ref[...])
    o_q = 2 * A_WIDTH
    o_kv = o_q + Q_LORA
    o_pe = o_kv + KV_LORA
    o_rot = o_pe + HEAD_PAD
    cos_p = cos_ref[...]
    sin_p = sin_ref[...]

    a_u = _gelu(h[:, :A_WIDTH])
    a_v = _gelu(h[:, A_WIDTH:o_q])
    vn = _layernorm(a_v, lng_ref[...], lnb_ref[...]).astype(BF16)
    t_idx = lax.broadcasted_iota(jnp.int32, (A_CHUNK, A_CHUNK), 0)
    s_idx = lax.broadcasted_iota(jnp.int32, (A_CHUNK, A_CHUNK), 1)
    causal = s_idx <= t_idx
    bs = bs_ref[...]
    for g in range(A_GROUPS):
        w_g = jnp.where(causal, ws_ref[g], 0.0).astype(BF16)
        cols = slice(g * A_GROUP_DIM, (g + 1) * A_GROUP_DIM)
        for c in range(rows // A_CHUNK):
            rws = slice(c * A_CHUNK, (c + 1) * A_CHUNK)
            mixed = _dot(w_g, vn[rws, cols]) + bs[:, cols]
            ya_ref[rws, cols] = (a_u[rws, cols] * mixed).astype(BF16)

    cqn = _rmsnorm(h[:, o_q:o_kv], qn_ref[...]).astype(BF16)
    q2 = _dot(cqn, wq_ref[...])
    half = MLA_HEADS * HEAD_PAD
    for hd in range(MLA_HEADS):
        cols = slice(hd * HEAD_PAD, (hd + 1) * HEAD_PAD)
        rot = slice(half + hd * HEAD_PAD, half + (hd + 1) * HEAD_PAD)
        q_ref[:, cols] = (q2[:, cols] * cos_p + q2[:, rot] * sin_p).astype(BF16)

    ckvn = _rmsnorm(h[:, o_kv:o_pe], kvn_ref[...]).astype(BF16)
    kv = _dot(ckvn, wkv_ref[...])
    kpe = h[:, o_pe:o_rot] * cos_p + h[:, o_rot:o_rot + HEAD_PAD] * sin_p
    for hd in range(MLA_HEADS):
        cols = slice(hd * HEAD_PAD, (hd + 1) * HEAD_PAD)
        k_ref[:, cols] = (kv[:, cols] + kpe).astype(BF16)
    v_ref[...] = kv[:, half:].astype(BF16)


def _mixer_in(x2d, cos_p, sin_p, w_in, gm_ln_g, gm_ln_b, w_s, b_s, q_norm, w_uq, kv_norm, w_ukv, *, rows):
    n, d = x2d.shape
    hp = MLA_HEADS * HEAD_PAD
    o_pe = 2 * A_WIDTH + Q_LORA + KV_LORA
    half = QK_ROPE // 2
    w_pe = w_in[:, o_pe:o_pe + QK_ROPE]
    w_pe_rot = jnp.concatenate([-w_pe[:, half:], w_pe[:, :half]], axis=1)
    pad_l = jnp.zeros((d, QK_NOPE), F32)
    pad_r = jnp.zeros((d, HEAD_PAD - QK_HEAD), F32)
    w_in_p = jnp.concatenate([w_in[:, :o_pe], pad_l, w_pe, pad_r, pad_l, w_pe_rot, pad_r], axis=1).astype(BF16)
    wq = w_uq.reshape(Q_LORA, MLA_HEADS, QK_HEAD)
    wq_pe = wq[:, :, QK_NOPE:]
    wq_rot = jnp.concatenate([jnp.zeros((Q_LORA, MLA_HEADS, QK_NOPE), F32), -wq_pe[:, :, half:], wq_pe[:, :, :half]],
                             axis=2)
    padq = ((0, 0), (0, 0), (0, HEAD_PAD - QK_HEAD))
    wq2 = jnp.concatenate([jnp.pad(wq, padq).reshape(Q_LORA, hp), jnp.pad(wq_rot, padq).reshape(Q_LORA, hp)],
                          axis=1).astype(BF16)
    wkv = w_ukv.reshape(KV_LORA, MLA_HEADS, QK_NOPE + V_HEAD)
    wk = jnp.pad(wkv[:, :, :QK_NOPE], ((0, 0), (0, 0), (0, HEAD_PAD - QK_NOPE))).reshape(KV_LORA, hp)
    wv = wkv[:, :, QK_NOPE:].reshape(KV_LORA, MLA_HEADS * V_HEAD)
    wkv2 = jnp.concatenate([wk, wv], axis=1).astype(BF16)
    bs_full = jnp.repeat(b_s.T, A_GROUP_DIM, axis=1)

    full = lambda a: pl.BlockSpec(a.shape, lambda i: (0,) * a.ndim)
    row = lambda w: pl.BlockSpec((rows, w), lambda i: (i, 0))
    args = (x2d, cos_p, sin_p, w_in_p, gm_ln_g.reshape(1, -1), gm_ln_b.reshape(1, -1), w_s, bs_full,
            q_norm.reshape(1, -1), wq2, kv_norm.reshape(1, -1), wkv2)
    in_specs = [row(d), row(HEAD_PAD), row(HEAD_PAD)] + [full(a) for a in args[3:]]
    return pl.pallas_call(
        _mixer_in_kernel,
        grid=(n // rows,),
        in_specs=in_specs,
        out_specs=(row(A_WIDTH), row(hp), row(hp), row(MLA_HEADS * V_HEAD)),
        out_shape=(jax.ShapeDtypeStruct((n, A_WIDTH), BF16), jax.ShapeDtypeStruct((n, hp), BF16),
                   jax.ShapeDtypeStruct((n, hp), BF16), jax.ShapeDtypeStruct((n, MLA_HEADS * V_HEAD), BF16)),
        compiler_params=_params(("parallel",)),
        name="mixer_in",
    )(*args)


def _attn_kernel(q_ref, k_ref, v_ref, o_ref, *, blk):
    seq = q_ref.shape[0]
    scale = QK_HEAD ** -0.5
    row = lax.broadcasted_iota(jnp.int32, (blk, blk), 0)
    col = lax.broadcasted_iota(jnp.int32, (blk, blk), 1)
    diag_mask = col <= row
    first_head_lanes = lax.broadcasted_iota(jnp.int32, (blk, 2 * V_HEAD), 1) < V_HEAD

    qk = lambda a, b: lax.dot_general(a, b, (((1,), (1,)), ((), ())), preferred_element_type=F32)

    for j in range(seq // blk):
        q0 = j * blk
        outs = []
        for hh in range(2):
            cols = slice(hh * HEAD_PAD, (hh + 1) * HEAD_PAD)
            q = q_ref[q0:q0 + blk, cols]
            s_d = jnp.where(diag_mask, qk(q, k_ref[q0:q0 + blk, cols]), NEG_BIG)
            m = jnp.max(s_d, axis=-1, keepdims=True)
            if j > 0:
                s_o = qk(q, k_ref[0:q0, cols])
                m = jnp.maximum(m, jnp.max(s_o, axis=-1, keepdims=True))
            p_d = jnp.exp((s_d - m) * scale)
            l = jnp.sum(p_d, axis=-1, keepdims=True)
            acc = _dot(p_d.astype(BF16), v_ref[q0:q0 + blk, :])
            if j > 0:
                p_o = jnp.exp((s_o - m) * scale)
                l = l + jnp.sum(p_o, axis=-1, keepdims=True)
                acc = acc + _dot(p_o.astype(BF16), v_ref[0:q0, :])
            outs.append(acc / l)
        o_ref[q0:q0 + blk, :] = jnp.where(first_head_lanes, outs[0], outs[1]).astype(BF16)


def _attention(q, k, v, *, bsz, seq, blk):
    n = bsz * seq
    pair = 2 * HEAD_PAD
    return pl.pallas_call(
        functools.partial(_attn_kernel, blk=blk),
        grid=(bsz, MLA_HEADS // 2),
        in_specs=[pl.BlockSpec((seq, pair), lambda b, h: (b, h)), pl.BlockSpec((seq, pair), lambda b, h: (b, h)),
                  pl.BlockSpec((seq, 2 * V_HEAD), lambda b, h: (b, h))],
        out_specs=pl.BlockSpec((seq, 2 * V_HEAD), lambda b, h: (b, h)),
        out_shape=jax.ShapeDtypeStruct((n, MLA_HEADS * V_HEAD), BF16),
        compiler_params=_params(("parallel", "parallel")),
        name="attention",
    )(q, k, v)


def _proj_ln_kernel(*refs, n_in):
    ins = refs[:n_in]
    ws = refs[n_in:2 * n_in]
    x_ref, g_ref, b_ref, o_ref = refs[2 * n_in:]
    acc = DEEPNORM_ALPHA * x_ref[...]
    for a_ref, w_ref in zip(ins, ws):
        acc = acc + _dot(a_ref[...], w_ref[...])
    o_ref[...] = _layernorm(acc, g_ref[...], b_ref[...])


def _proj_ln(ins, ws, x2d, g, b, *, rows, name):
    n, d = x2d.shape
    full = lambda a: pl.BlockSpec(a.shape, lambda i: (0,) * a.ndim)
    row = lambda w: pl.BlockSpec((rows, w), lambda i: (i, 0))
    g2, b2 = g.reshape(1, -1), b.reshape(1, -1)
    return pl.pallas_call(
        functools.partial(_proj_ln_kernel, n_in=len(ins)),
        grid=(n // rows,),
        in_specs=[row(a.shape[1]) for a in ins] + [full(w) for w in ws] + [row(d), full(g2), full(b2)],
        out_specs=row(d),
        out_shape=jax.ShapeDtypeStruct((n, d), F32),
        compiler_params=_params(("parallel",)),
        name=name,
    )(*ins, *ws, x2d, g2, b2)


def _swiglu_chunks(xb, wg_ref, wu_ref, wd_ref):
    dff = wg_ref.shape[1]
    acc = None
    for c0 in range(0, dff, FFN_CHUNK):
        cols = slice(c0, min(c0 + FFN_CHUNK, dff))
        gate = _dot(xb, wg_ref[:, cols])
        up = _dot(xb, wu_ref[:, cols])
        hid = (gate * _sigmoid(gate) * up).astype(BF16)
        part = _dot(hid, wd_ref[cols, :])
        acc = part if acc is None else acc + part
    return acc


def _ffn_kernel(x_ref, wg_ref, wu_ref, wd_ref, g_ref, b_ref, o_ref):
    x = x_ref[...]
    ffn = _swiglu_chunks(x.astype(BF16), wg_ref, wu_ref, wd_ref)
    o_ref[...] = _layernorm(DEEPNORM_ALPHA * x + ffn, g_ref[...], b_ref[...])


def _ffn(x2d, w_gate, w_up, w_down, g, b, *, rows):
    n, d = x2d.shape
    g2, b2 = g.reshape(1, -1), b.reshape(1, -1)
    once = lambda a: pl.BlockSpec(a.shape, lambda i: (0,) * a.ndim, pipeline_mode=pl.Buffered(1))
    return pl.pallas_call(
        _ffn_kernel,
        grid=(n // rows,),
        in_specs=[pl.BlockSpec((rows, d), lambda i: (i, 0)), once(w_gate), once(w_up), once(w_down), once(g2),
                  once(b2)],
        out_specs=pl.BlockSpec((rows, d), lambda i: (i, 0)),
        out_shape=jax.ShapeDtypeStruct((n, d), F32),
        compiler_params=_params(("parallel",)),
        name="ffn_dense",
    )(x2d, w_gate, w_up, w_down, g2, b2)


def _s5_kernel(x_ref, perm_ref, win_ref, bd_ref, cd_ref, lam_ref, d_ref, y_hbm, bur_ref, bui_ref, hr_ref, hi_ref,
               st_ref, ys_ref, sem, *, bsz):
    i = pl.program_id(0)
    n_steps = pl.num_programs(0)
    slot = i % 2
    steps = x_ref.shape[1]
    rows = bsz * steps

    def out_copies(at_step, at_slot):
        return [pltpu.make_async_copy(ys_ref.at[at_slot, pl.ds(t * bsz, bsz)], y_hbm.at[:, at_step * steps + t, :],
                                      sem.at[at_slot]) for t in range(steps)]

    @pl.when(i == 0)
    def _():
        st_ref[...] = jnp.zeros_like(st_ref)

    @pl.when(i >= 2)
    def _():
        for cp in out_copies(i - 2, slot):
            cp.wait()

    n_blk, cb, sb2 = bd_ref.shape
    sb = sb2 // 2
    xb = x_ref[...].reshape(rows, x_ref.shape[2]).astype(BF16)
    xb = _dot(perm_ref[...], xb).astype(BF16)
    u = _dot(xb, win_ref[...])
    ub = u.astype(BF16)
    for q in range(n_blk):
        bu = _dot(ub[:, q * cb:(q + 1) * cb], bd_ref[q])
        bur_ref[:, q * sb:(q + 1) * sb] = bu[:, :sb]
        bui_ref[:, q * sb:(q + 1) * sb] = bu[:, sb:]
    for c in range(bur_ref.shape[1] // S5_SCAN_LANES):
        lanes = slice(c * S5_SCAN_LANES, (c + 1) * S5_SCAN_LANES)
        a_re = lam_ref[0:1, lanes]
        a_im = lam_ref[1:2, lanes]
        h_re = st_ref[0, :, lanes]
        h_im = st_ref[1, :, lanes]
        for t in range(steps):
            rws = slice(t * bsz, (t + 1) * bsz)
            n_re = a_re * h_re - a_im * h_im + bur_ref[rws, lanes]
            n_im = a_re * h_im + a_im * h_re + bui_ref[rws, lanes]
            hr_ref[rws, lanes] = n_re.astype(BF16)
            hi_ref[rws, lanes] = n_im.astype(BF16)
            h_re, h_im = n_re, n_im
        st_ref[0, :, lanes] = h_re
        st_ref[1, :, lanes] = h_im
    for q in range(n_blk):
        st = slice(q * sb, (q + 1) * sb)
        ch = slice(q * cb, (q + 1) * cb)
        y = _dot(hr_ref[:, st], cd_ref[q, :sb]) + _dot(hi_ref[:, st], cd_ref[q, sb:])
        ys_ref[slot, :, ch] = y + d_ref[:, ch] * u[:, ch]
    for cp in out_copies(i, slot):
        cp.start()

    @pl.when(i == n_steps - 1)
    def _():
        for cp in out_copies(i, slot):
            cp.wait()

    @pl.when((i == n_steps - 1) & (i >= 1))
    def _():
        for cp in out_copies(i - 1, 1 - slot):
            cp.wait()


def _s5(x2d, w_in, bd, cd, lam, d_skip, *, bsz):
    n, d = x2d.shape
    seq = n // bsz
    rows = S5_STEPS * bsz
    n_state = lam.shape[1]
    full = lambda a: pl.BlockSpec(a.shape, lambda i: (0,) * a.ndim)
    d2 = d_skip.reshape(1, -1)
    r_out = jnp.arange(rows)[:, None]
    r_in = jnp.arange(rows)[None, :]
    perm = (r_in == (r_out % bsz) * S5_STEPS + r_out // bsz).astype(BF16)
    y = pl.pallas_call(
        functools.partial(_s5_kernel, bsz=bsz),
        grid=(seq // S5_STEPS,),
        in_specs=[pl.BlockSpec((bsz, S5_STEPS, d), lambda i: (0, i, 0)), full(perm), full(w_in), full(bd), full(cd),
                  full(lam), full(d2)],
        out_specs=pl.BlockSpec(memory_space=pl.ANY),
        out_shape=jax.ShapeDtypeStruct((bsz, seq, d), F32),
        scratch_shapes=[pltpu.VMEM((rows, n_state), F32), pltpu.VMEM((rows, n_state), F32),
                        pltpu.VMEM((rows, n_state), BF16), pltpu.VMEM((rows, n_state), BF16),
                        pltpu.VMEM((2, bsz, n_state), F32), pltpu.VMEM((2, rows, d), F32),
                        pltpu.SemaphoreType.DMA((2,))],
        compiler_params=_params(("arbitrary",)),
        name="s5",
    )(x2d.reshape(bsz, seq, d), perm, w_in, bd, cd, lam, d2)
    return y.reshape(n, d)


def _s5_weights(a_re, a_im, log_step, b_re, b_im, c_re, c_im):
    n_grp, n_st = a_re.shape
    gpb = S5_GROUPS_PER_BLOCK
    n_blk = n_grp // gpb
    delta = jnp.exp(log_step)[:, None]
    mag = jnp.exp(delta * a_re)
    abar_re = mag * jnp.cos(delta * a_im)
    abar_im = mag * jnp.sin(delta * a_im)
    den = a_re * a_re + a_im * a_im
    coef_re = ((abar_re - 1.0) * a_re + abar_im * a_im) / den
    coef_im = (abar_im * a_re - (abar_re - 1.0) * a_im) / den
    bb_re = coef_re[..., None] * b_re - coef_im[..., None] * b_im
    bb_im = coef_re[..., None] * b_im + coef_im[..., None] * b_re
    same_group = jnp.eye(gpb, dtype=F32)[None, :, None, :, None]

    def expand(w):
        w = jnp.transpose(w.reshape(n_blk, gpb, w.shape[1], w.shape[2]), (0, 1, 3, 2))
        w = w[:, :, :, None, :] * same_group
        return w.reshape(n_blk, gpb * w.shape[2], gpb * w.shape[4])

    bd = jnp.concatenate([expand(bb_re), expand(bb_im)], axis=2).astype(BF16)
    cd = jnp.concatenate([expand(c_re), -expand(c_im)], axis=1).astype(BF16)
    lam = jnp.stack([abar_re.reshape(-1), abar_im.reshape(-1)])
    return bd, cd, lam


def _s5_out_kernel(y_ref, gw_ref, gb_ref, wo_ref, x_ref, g_ref, b_ref, r_ref, o_ref, route_ref, count_ref,
                   carry_ref):
    @pl.when(pl.program_id(0) == 0)
    def _():
        carry_ref[...] = jnp.zeros_like(carry_ref)

    rows = x_ref.shape[0]
    gl = _gelu(y_ref[...])
    z = gl * _sigmoid(_dot(gl.astype(BF16), gw_ref[...]) + gb_ref[...])
    mix = _dot(z.astype(BF16), wo_ref[...])
    xo = _layernorm(DEEPNORM_ALPHA * x_ref[...] + mix, g_ref[...], b_ref[...])
    o_ref[...] = xo
    x_hi = xo.astype(BF16)
    x_lo = (xo - x_hi.astype(F32)).astype(BF16)
    r = r_ref[...]
    r_hi = r.astype(BF16)
    r_lo = (r - r_hi.astype(F32)).astype(BF16)
    logits = _dot(x_hi, r_hi) + (_dot(x_lo, r_hi) + _dot(x_hi, r_lo))
    idx = lax.broadcasted_iota(jnp.int32, logits.shape, 1)
    m1 = jnp.max(logits, axis=-1, keepdims=True)
    i1 = jnp.min(jnp.where(logits == m1, idx, N_EXPERTS), axis=-1, keepdims=True)
    rest = jnp.where(idx == i1, -jnp.inf, logits)
    m2 = jnp.max(rest, axis=-1, keepdims=True)
    i2 = jnp.min(jnp.where(rest == m2, idx, N_EXPERTS), axis=-1, keepdims=True)
    e2 = jnp.exp(m2 - m1)
    g1 = 1.0 / (1.0 + e2)
    g2 = e2 * g1
    sel = jnp.where((idx == i1) | (idx == i2), 1.0, 0.0)
    t_r = lax.broadcasted_iota(jnp.int32, (rows, rows), 0)
    t_c = lax.broadcasted_iota(jnp.int32, (rows, rows), 1)
    earlier = jnp.where(t_c < t_r, 1.0, 0.0).astype(BF16)
    before = _dot(earlier, sel.astype(BF16)) + carry_ref[...]
    r1 = jnp.sum(jnp.where(idx == i1, before, 0.0), axis=-1, keepdims=True)
    r2 = jnp.sum(jnp.where(idx == i2, before, 0.0), axis=-1, keepdims=True)
    total = carry_ref[...] + jnp.sum(sel, axis=0, keepdims=True)
    carry_ref[...] = total
    count_ref[...] = total
    fields = (i1.astype(F32), i2.astype(F32), r1, r2, g1, g2)
    route = jnp.zeros(logits.shape, F32)
    for lane, val in enumerate(fields):
        route = jnp.where(idx == lane, val, route)
    route_ref[...] = route


def _s5_out(y2d, glu_w, glu_b, w_out, x2d, g, b, router, *, rows):
    n, d = x2d.shape
    full = lambda a: pl.BlockSpec(a.shape, lambda i: (0,) * a.ndim)
    row = lambda w: pl.BlockSpec((rows, w), lambda i: (i, 0))
    args = (y2d, glu_w.astype(BF16), glu_b.reshape(1, -1), w_out.astype(BF16), x2d, g.reshape(1, -1),
            b.reshape(1, -1), router)
    in_specs = [row(d), full(args[1]), full(args[2]), full(args[3]), row(d), full(args[5]), full(args[6]),
                full(args[7])]
    return pl.pallas_call(
        _s5_out_kernel,
        grid=(n // rows,),
        in_specs=in_specs,
        out_specs=(row(d), row(N_EXPERTS), pl.BlockSpec((1, N_EXPERTS), lambda i: (0, 0))),
        out_shape=(jax.ShapeDtypeStruct((n, d), F32), jax.ShapeDtypeStruct((n, N_EXPERTS), F32),
                   jax.ShapeDtypeStruct((1, N_EXPERTS), F32)),
        scratch_shapes=[pltpu.VMEM((1, N_EXPERTS), F32)],
        compiler_params=_params(("arbitrary",)),
        name="s5_out",
    )(*args)


def _dispatch_kernel(slot_ref, x_ref, xs_in_ref, xs_ref, sem):
    del xs_in_ref
    rows = x_ref.shape[0]

    def issue(r, carry):
        for k in range(2):
            s = slot_ref[0, 0, 2 * r + k]
            pltpu.make_async_copy(x_ref.at[pl.ds(r, 1)], xs_ref.at[pl.ds(s, 1)], sem).start()
        return carry

    lax.fori_loop(0, rows, issue, 0, unroll=DMA_UNROLL)
    for k in range(2):
        pltpu.make_async_copy(x_ref, xs_ref.at[pl.ds(0, rows)], sem).wait()


def _dispatch(x2d, slots, n_slots, *, rows):
    n, d = x2d.shape
    return pl.pallas_call(
        _dispatch_kernel,
        grid=(n // rows,),
        in_specs=[pl.BlockSpec((1, 1, 2 * rows), lambda i: (i, 0, 0), memory_space=pltpu.SMEM),
                  pl.BlockSpec((rows, d), lambda i: (i, 0)),
                  pl.BlockSpec(memory_space=pl.ANY)],
        out_specs=pl.BlockSpec(memory_space=pl.ANY),
        out_shape=jax.ShapeDtypeStruct((n_slots, d), F32),
        scratch_shapes=[pltpu.SemaphoreType.DMA],
        input_output_aliases={2: 0},
        compiler_params=_params(("arbitrary",)),
        name="moe_dispatch",
    )(slots, x2d, jnp.zeros((n_slots, d), F32))


def _ffn_grouped_kernel(te_ref, na_ref, x_ref, wg_ref, wu_ref, wd_ref, o_ref, xb_ref, acc_ref):
    del te_ref
    i = pl.program_id(0)
    f = pl.program_id(1)
    active = i < na_ref[0]

    @pl.when(active)
    def _():
        @pl.when(f == 0)
        def _():
            xb_ref[...] = x_ref[...].astype(BF16)

        contrib = _swiglu_chunks(xb_ref[...], wg_ref.at[0], wu_ref.at[0], wd_ref.at[0])

        @pl.when(f == 0)
        def _():
            acc_ref[...] = contrib

        @pl.when(f > 0)
        def _():
            acc_ref[...] += contrib

    last = f == pl.num_programs(1) - 1

    @pl.when(last & active)
    def _():
        o_ref[...] = acc_ref[...]

    @pl.when(last & jnp.logical_not(active))
    def _():
        o_ref[...] = jnp.zeros_like(o_ref)


def _ffn_grouped(xs, tile_expert, n_active, w_gate, w_up, w_down, *, rows, fchunk):
    m, d = xs.shape
    n_f = w_gate.shape[2] // fchunk
    tile = lambda i, na: jnp.maximum(jnp.minimum(i, na[0] - 1), 0)
    chunk = lambda i, f, na: jnp.where(i < na[0], f, n_f - 1)
    return pl.pallas_call(
        _ffn_grouped_kernel,
        grid_spec=pltpu.PrefetchScalarGridSpec(
            num_scalar_prefetch=2,
            grid=(m // rows, n_f),
            in_specs=[pl.BlockSpec((rows, d), lambda i, f, te, na: (tile(i, na), 0)),
                      pl.BlockSpec((1, d, fchunk), lambda i, f, te, na: (te[tile(i, na)], 0, chunk(i, f, na))),
                      pl.BlockSpec((1, d, fchunk), lambda i, f, te, na: (te[tile(i, na)], 0, chunk(i, f, na))),
                      pl.BlockSpec((1, fchunk, d), lambda i, f, te, na: (te[tile(i, na)], chunk(i, f, na), 0))],
            out_specs=pl.BlockSpec((rows, d), lambda i, f, te, na: (i, 0)),
            scratch_shapes=[pltpu.VMEM((rows, d), BF16), pltpu.VMEM((rows, d), F32)],
        ),
        out_shape=jax.ShapeDtypeStruct((m, d), F32),
        compiler_params=_params(("arbitrary", "arbitrary")),
        name="ffn_moe",
    )(tile_expert, n_active, xs, w_gate, w_up, w_down)


def _combine_kernel(slot_ref, x_ref, route_ref, ys_ref, g_ref, b_ref, o_ref, y1_ref, y2_ref, sem):
    rows = x_ref.shape[0]

    def issue(r, carry):
        for k, dst in enumerate((y1_ref, y2_ref)):
            s = slot_ref[0, 0, 2 * r + k]
            pltpu.make_async_copy(ys_ref.at[pl.ds(s, 1)], dst.at[pl.ds(r, 1)], sem).start()
        return carry

    lax.fori_loop(0, rows, issue, 0, unroll=DMA_UNROLL)
    for dst in (y1_ref, y2_ref):
        pltpu.make_async_copy(ys_ref.at[pl.ds(0, rows)], dst, sem).wait()
    route = route_ref[...]
    moe = route[:, 4:5] * y1_ref[...] + route[:, 5:6] * y2_ref[...]
    o_ref[...] = _layernorm(DEEPNORM_ALPHA * x_ref[...] + moe, g_ref[...], b_ref[...])


def _combine_ln(x2d, route, slots, ys, g, b, *, rows):
    n, d = x2d.shape
    g2, b2 = g.reshape(1, -1), b.reshape(1, -1)
    return pl.pallas_call(
        _combine_kernel,
        grid=(n // rows,),
        in_specs=[pl.BlockSpec((1, 1, 2 * rows), lambda i: (i, 0, 0), memory_space=pltpu.SMEM),
                  pl.BlockSpec((rows, d), lambda i: (i, 0)),
                  pl.BlockSpec((rows, N_EXPERTS), lambda i: (i, 0)),
                  pl.BlockSpec(memory_space=pl.ANY),
                  pl.BlockSpec((1, d), lambda i: (0, 0)),
                  pl.BlockSpec((1, d), lambda i: (0, 0))],
        out_specs=pl.BlockSpec((rows, d), lambda i: (i, 0)),
        out_shape=jax.ShapeDtypeStruct((n, d), F32),
        scratch_shapes=[pltpu.VMEM((rows, d), F32), pltpu.VMEM((rows, d), F32), pltpu.SemaphoreType.DMA],
        compiler_params=_params(("arbitrary",)),
        name="moe_combine",
    )(slots, x2d, route, ys, g2, b2)


def _moe(x2d, route, counts, w_gate, w_up, w_down, g, b, *, rows, fchunk):
    n, _ = x2d.shape
    counts = counts.reshape(-1).astype(jnp.int32)
    padded = (counts + rows - 1) // rows * rows
    ends = jnp.cumsum(padded)
    offs = ends - padded
    n_tiles = (2 * n) // rows + N_EXPERTS
    starts = jnp.arange(n_tiles, dtype=jnp.int32) * rows
    tile_expert = jnp.minimum(jnp.sum(ends[None, :] <= starts[:, None], axis=1), N_EXPERTS - 1).astype(jnp.int32)
    n_active = (ends[-1:] // rows).astype(jnp.int32)
    expert = route[:, 0:2].astype(jnp.int32)
    rank = route[:, 2:4].astype(jnp.int32)
    base = jnp.sum(jnp.where(expert[..., None] == jnp.arange(N_EXPERTS), offs, 0), axis=-1)
    slots = (base + rank).reshape(n // rows, 1, 2 * rows)
    xs = _dispatch(x2d, slots, n_tiles * rows, rows=rows)
    ys = _ffn_grouped(xs, tile_expert, n_active, w_gate, w_up, w_down, rows=rows, fchunk=fchunk)
    return _combine_ln(x2d, route, slots, ys, g, b, rows=rows)


def _layer_even(x2d, positions, bsz, seq, w_in, gm_ln_g, gm_ln_b, w_s, b_s, q_norm, w_uq, kv_norm, w_ukv, w_out,
                ln_g, ln_b, f_gate, f_up, f_down, f_ln_g, f_ln_b, *, rows, ffn_rows, attn_blk):
    cos_p, sin_p = _rope_tables(positions)
    ya, q, k, v = _mixer_in(x2d, cos_p, sin_p, w_in, gm_ln_g, gm_ln_b, w_s, b_s, q_norm, w_uq, kv_norm, w_ukv,
                            rows=rows)
    yb = _attention(q, k, v, bsz=bsz, seq=seq, blk=attn_blk)
    wo = w_out.astype(BF16)
    x2d = _proj_ln([ya, yb], [wo[:A_WIDTH], wo[A_WIDTH:]], x2d, ln_g, ln_b, rows=rows, name="mixer_out")
    return _ffn(x2d, f_gate.astype(BF16), f_up.astype(BF16), f_down.astype(BF16), f_ln_g, f_ln_b, rows=ffn_rows)


def _layer_odd(x2d, bsz, w_in, a_re, a_im, log_step, b_re, b_im, c_re, c_im, d_skip, glu_w, glu_b, w_out, ln_g, ln_b,
               router, m_gate, m_up, m_down, m_ln_g, m_ln_b, *, rows, ffn_rows, fchunk):
    bd, cd, lam = _s5_weights(a_re, a_im, log_step, b_re, b_im, c_re, c_im)
    y2d = _s5(x2d, w_in.astype(BF16), bd, cd, lam, d_skip, bsz=bsz)
    x2d, route, counts = _s5_out(y2d, glu_w, glu_b, w_out, x2d, ln_g, ln_b, router, rows=rows)
    return _moe(x2d, route, counts, m_gate.astype(BF16), m_up.astype(BF16), m_down.astype(BF16), m_ln_g, m_ln_b,
                rows=ffn_rows, fchunk=fchunk)


def kernel(x, positions, ab_w_in, gm_ln_g, gm_ln_b, gm_w_s, gm_b_s, mla_q_norm, mla_w_uq, mla_kv_norm, mla_w_ukv, ab_w_out, ab_ln_g, ab_ln_b, ffd_w_gate, ffd_w_up, ffd_w_down, ffd_ln_g, ffd_ln_b, c_w_in, s5_a_re, s5_a_im, s5_log_step, s5_b_re, s5_b_im, s5_c_re, s5_c_im, s5_d, glu_w, glu_b, c_w_out, c_ln_g, c_ln_b, moe_router, moe_w_gate, moe_w_up, moe_w_down, moe_ln_g, moe_ln_b):
    bsz, seq, d = x.shape
    x2d = x.reshape(bsz * seq, d)
    rows = min(512, seq)
    for i in range(DEPTH):
        j = i // 2
        if i % 2 == 0:
            x2d = _layer_even(x2d, positions, bsz, seq, ab_w_in[j], gm_ln_g[j], gm_ln_b[j], gm_w_s[j], gm_b_s[j],
                              mla_q_norm[j], mla_w_uq[j], mla_kv_norm[j], mla_w_ukv[j], ab_w_out[j], ab_ln_g[j],
                              ab_ln_b[j], ffd_w_gate[j], ffd_w_up[j], ffd_w_down[j], ffd_ln_g[j], ffd_ln_b[j],
                              rows=rows, ffn_rows=rows, attn_blk=min(512, seq))
        else:
            x2d = _layer_odd(x2d, bsz, c_w_in[j], s5_a_re[j], s5_a_im[j], s5_log_step[j], s5_b_re[j],
                             s5_b_im[j], s5_c_re[j], s5_c_im[j], s5_d[j], glu_w[j], glu_b[j], c_w_out[j], c_ln_g[j],
                             c_ln_b[j], moe_router[j], moe_w_gate[j], moe_w_up[j], moe_w_down[j], moe_ln_g[j],
                             moe_ln_b[j], rows=rows, ffn_rows=rows, fchunk=moe_w_gate.shape[3] // 2)
    return x2d.reshape(bsz, seq, d)
```

```python
import functools
import math

import jax
import jax.numpy as jnp
from jax import lax
from jax.experimental import pallas as pl
from jax.experimental.pallas import tpu as pltpu

F32 = jnp.float32
BF16 = jnp.bfloat16

A_GROUPS = 4
A_GROUP_DIM = 128
A_WIDTH = A_GROUPS * A_GROUP_DIM
A_CHUNK = 128
MLA_HEADS = 8
QK_NOPE = 64
QK_ROPE = 32
QK_HEAD = QK_NOPE + QK_ROPE
V_HEAD = 64
Q_LORA = 384
KV_LORA = 256
ROPE_THETA = 10000.0
S5_GROUP_DIM = 16
S5_STATE = 64
N_EXPERTS = 8
LN_EPS = 1e-5
RMS_EPS = 1e-6
DEPTH = 2
DEEPNORM_ALPHA = (2.0 * DEPTH) ** 0.25

LANES = 128
HEAD_PAD = LANES
S5_GROUPS_PER_BLOCK = 2 * LANES // S5_GROUP_DIM
S5_STEPS = 16
S5_SCAN_LANES = 4 * LANES
VMEM_LIMIT = 56 * 1024 * 1024
NEG_BIG = -1e30
FFN_CHUNK = 512
DMA_UNROLL = 16


def _params(sem):
    return pltpu.CompilerParams(dimension_semantics=sem, vmem_limit_bytes=VMEM_LIMIT)


def _gelu(x):
    c = math.sqrt(2.0 / math.pi)
    return 0.5 * x * (1.0 + jnp.tanh(c * (x + 0.044715 * (x * x * x))))


def _sigmoid(x):
    return 1.0 / (1.0 + jnp.exp(-x))


def _layernorm(x, g, b):
    mu = jnp.mean(x, axis=-1, keepdims=True)
    xc = x - mu
    var = jnp.mean(xc * xc, axis=-1, keepdims=True)
    return xc * lax.rsqrt(var + LN_EPS) * g + b


def _rmsnorm(x, g):
    ms = jnp.mean(x * x, axis=-1, keepdims=True)
    return x * lax.rsqrt(ms + RMS_EPS) * g


def _dot(a, b):
    return jnp.dot(a, b, preferred_element_type=F32)


def _rope_table_kernel(inv_ref, pos_ref, cos_ref, sin_ref):
    pos = pos_ref[...].astype(F32)
    for j in range(QK_ROPE // 2):
        ang = pos * inv_ref[j]
        cos_ref[j] = jnp.cos(ang)
        sin_ref[j] = jnp.sin(ang)


def _rope_tables(positions):
    n = positions.size
    half = QK_ROPE // 2
    inv_freq = 1.0 / (ROPE_THETA ** (jnp.arange(0, QK_ROPE, 2, dtype=F32) / QK_ROPE))
    pos2d = positions.reshape(n // LANES, LANES)
    cos_t, sin_t = pl.pallas_call(
        _rope_table_kernel,
        out_shape=(jax.ShapeDtypeStruct((half, n // LANES, LANES), F32),) * 2,
        in_specs=[pl.BlockSpec(memory_space=pltpu.SMEM), pl.BlockSpec(memory_space=pltpu.VMEM)],
        out_specs=(pl.BlockSpec(memory_space=pltpu.VMEM),) * 2,
        name="rope_table",
    )(inv_freq, pos2d)
    cos_c = cos_t.reshape(half, n).T
    sin_c = sin_t.reshape(half, n).T
    ones = jnp.ones((n, QK_NOPE), F32)
    zeros_n = jnp.zeros((n, QK_NOPE), F32)
    zeros_p = jnp.zeros((n, HEAD_PAD - QK_HEAD), F32)
    cos_p = jnp.concatenate([ones, cos_c, cos_c, zeros_p], axis=1)
    sin_p = jnp.concatenate([zeros_n, sin_c, sin_c, zeros_p], axis=1)
    return cos_p, sin_p


def _mixer_in_kernel(x_ref, cos_ref, sin_ref, w_in_ref, lng_ref, lnb_ref, ws_ref, bs_ref, qn_ref, wq_ref,
                     kvn_ref, wkv_ref, ya_ref, q_ref, k_ref, v_ref):
    rows = x_ref.shape[0]
    xb = x_ref[...].astype(BF16)
    h = _dot(xb, w_in_ref[...])
    o_q = 2 * A_WIDTH
    o_kv = o_q + Q_LORA
    o_pe = o_kv + KV_LORA
    o_rot = o_pe + HEAD_PAD
    cos_p = cos_ref[...]
    sin_p = sin_ref[...]

    a_u = _gelu(h[:, :A_WIDTH])
    a_v = _gelu(h[:, A_WIDTH:o_q])
    vn = _layernorm(a_v, lng_ref[...], lnb_ref[...]).astype(BF16)
    t_idx = lax.broadcasted_iota(jnp.int32, (A_CHUNK, A_CHUNK), 0)
    s_idx = lax.broadcasted_iota(jnp.int32, (A_CHUNK, A_CHUNK), 1)
    causal = s_idx <= t_idx
    bs = bs_ref[...]
    for g in range(A_GROUPS):
        w_g = jnp.where(causal, ws_ref[g], 0.0).astype(BF16)
        cols = slice(g * A_GROUP_DIM, (g + 1) * A_GROUP_DIM)
        for c in range(rows // A_CHUNK):
            rws = slice(c * A_CHUNK, (c + 1) * A_CHUNK)
            mixed = _dot(w_g, vn[rws, cols]) + bs[:, cols]
            ya_ref[rws, cols] = (a_u[rws, cols] * mixed).astype(BF16)

    cqn = _rmsnorm(h[:, o_q:o_kv], qn_ref[...]).astype(BF16)
    q2 = _dot(cqn, wq_ref[...])
    half = MLA_HEADS * HEAD_PAD
    for hd in range(MLA_HEADS):
        cols = slice(hd * HEAD_PAD, (hd + 1) * HEAD_PAD)
        rot = slice(half + hd * HEAD_PAD, half + (hd + 1) * HEAD_PAD)
        q_ref[:, cols] = (q2[:, cols] * cos_p + q2[:, rot] * sin_p).astype(BF16)

    ckvn = _rmsnorm(h[:, o_kv:o_pe], kvn_ref[...]).astype(BF16)
    kv = _dot(ckvn, wkv_ref[...])
    kpe = h[:, o_pe:o_rot] * cos_p + h[:, o_rot:o_rot + HEAD_PAD] * sin_p
    for hd in range(MLA_HEADS):
        cols = slice(hd * HEAD_PAD, (hd + 1) * HEAD_PAD)
        k_ref[:, cols] = (kv[:, cols] + kpe).astype(BF16)
    v_ref[...] = kv[:, half:].astype(BF16)


def _mixer_in(x2d, cos_p, sin_p, w_in, gm_ln_g, gm_ln_b, w_s, b_s, q_norm, w_uq, kv_norm, w_ukv, *, rows):
    n, d = x2d.shape
    hp = MLA_HEADS * HEAD_PAD
    o_pe = 2 * A_WIDTH + Q_LORA + KV_LORA
    half = QK_ROPE // 2
    w_pe = w_in[:, o_pe:o_pe + QK_ROPE]
    w_pe_rot = jnp.concatenate([-w_pe[:, half:], w_pe[:, :half]], axis=1)
    pad_l = jnp.zeros((d, QK_NOPE), F32)
    pad_r = jnp.zeros((d, HEAD_PAD - QK_HEAD), F32)
    w_in_p = jnp.concatenate([w_in[:, :o_pe], pad_l, w_pe, pad_r, pad_l, w_pe_rot, pad_r], axis=1).astype(BF16)
    wq = w_uq.reshape(Q_LORA, MLA_HEADS, QK_HEAD)
    wq_pe = wq[:, :, QK_NOPE:]
    wq_rot = jnp.concatenate([jnp.zeros((Q_LORA, MLA_HEADS, QK_NOPE), F32), -wq_pe[:, :, half:], wq_pe[:, :, :half]],
                             axis=2)
    padq = ((0, 0), (0, 0), (0, HEAD_PAD - QK_HEAD))
    wq2 = jnp.concatenate([jnp.pad(wq, padq).reshape(Q_LORA, hp), jnp.pad(wq_rot, padq).reshape(Q_LORA, hp)],
                          axis=1).astype(BF16)
    wkv = w_ukv.reshape(KV_LORA, MLA_HEADS, QK_NOPE + V_HEAD)
    wk = jnp.pad(wkv[:, :, :QK_NOPE], ((0, 0), (0, 0), (0, HEAD_PAD - QK_NOPE))).reshape(KV_LORA, hp)
    wv = wkv[:, :, QK_NOPE:].reshape(KV_LORA, MLA_HEADS * V_HEAD)
    wkv2 = jnp.concatenate([wk, wv], axis=1).astype(BF16)
    bs_full = jnp.repeat(b_s.T, A_GROUP_DIM, axis=1)

    full = lambda a: pl.BlockSpec(a.shape, lambda i: (0,) * a.ndim)
    row = lambda w: pl.BlockSpec((rows, w), lambda i: (i, 0))
    args = (x2d, cos_p, sin_p, w_in_p, gm_ln_g.reshape(1, -1), gm_ln_b.reshape(1, -1), w_s, bs_full,
            q_norm.reshape(1, -1), wq2, kv_norm.reshape(1, -1), wkv2)
    in_specs = [row(d), row(HEAD_PAD), row(HEAD_PAD)] + [full(a) for a in args[3:]]
    return pl.pallas_call(
        _mixer_in_kernel,
        grid=(n // rows,),
        in_specs=in_specs,
        out_specs=(row(A_WIDTH), row(hp), row(hp), row(MLA_HEADS * V_HEAD)),
        out_shape=(jax.ShapeDtypeStruct((n, A_WIDTH), BF16), jax.ShapeDtypeStruct((n, hp), BF16),
                   jax.ShapeDtypeStruct((n, hp), BF16), jax.ShapeDtypeStruct((n, MLA_HEADS * V_HEAD), BF16)),
        compiler_params=_params(("parallel",)),
        name="mixer_in",
    )(*args)


def _attn_kernel(q_ref, k_ref, v_ref, o_ref, *, blk):
    seq = q_ref.shape[0]
    scale = QK_HEAD ** -0.5
    row = lax.broadcasted_iota(jnp.int32, (blk, blk), 0)
    col = lax.broadcasted_iota(jnp.int32, (blk, blk), 1)
    diag_mask = col <= row
    first_head_lanes = lax.broadcasted_iota(jnp.int32, (blk, 2 * V_HEAD), 1) < V_HEAD

    qk = lambda a, b: lax.dot_general(a, b, (((1,), (1,)), ((), ())), preferred_element_type=F32)

    for j in range(seq // blk):
        q0 = j * blk
        outs = []
        for hh in range(2):
            cols = slice(hh * HEAD_PAD, (hh + 1) * HEAD_PAD)
            q = q_ref[q0:q0 + blk, cols]
            s_d = jnp.where(diag_mask, qk(q, k_ref[q0:q0 + blk, cols]), NEG_BIG)
            m = jnp.max(s_d, axis=-1, keepdims=True)
            if j > 0:
                s_o = qk(q, k_ref[0:q0, cols])
                m = jnp.maximum(m, jnp.max(s_o, axis=-1, keepdims=True))
            p_d = jnp.exp((s_d - m) * scale)
            l = jnp.sum(p_d, axis=-1, keepdims=True)
            acc = _dot(p_d.astype(BF16), v_ref[q0:q0 + blk, :])
            if j > 0:
                p_o = jnp.exp((s_o - m) * scale)
                l = l + jnp.sum(p_o, axis=-1, keepdims=True)
                acc = acc + _dot(p_o.astype(BF16), v_ref[0:q0, :])
            outs.append(acc / l)
        o_ref[q0:q0 + blk, :] = jnp.where(first_head_lanes, outs[0], outs[1]).astype(BF16)


def _attention(q, k, v, *, bsz, seq, blk):
    n = bsz * seq
    pair = 2 * HEAD_PAD
    return pl.pallas_call(
        functools.partial(_attn_kernel, blk=blk),
        grid=(bsz, MLA_HEADS // 2),
        in_specs=[pl.BlockSpec((seq, pair), lambda b, h: (b, h)), pl.BlockSpec((seq, pair), lambda b, h: (b, h)),
                  pl.BlockSpec((seq, 2 * V_HEAD), lambda b, h: (b, h))],
        out_specs=pl.BlockSpec((seq, 2 * V_HEAD), lambda b, h: (b, h)),
        out_shape=jax.ShapeDtypeStruct((n, MLA_HEADS * V_HEAD), BF16),
        compiler_params=_params(("parallel", "parallel")),
        name="attention",
    )(q, k, v)


def _swiglu_chunks(xb, wg_ref, wu_ref, wd_ref):
    dff = wg_ref.shape[1]
    acc = None
    for c0 in range(0, dff, FFN_CHUNK):
        cols = slice(c0, min(c0 + FFN_CHUNK, dff))
        gate = _dot(xb, wg_ref[:, cols])
        up = _dot(xb, wu_ref[:, cols])
        hid = (gate * _sigmoid(gate) * up).astype(BF16)
        part = _dot(hid, wd_ref[cols, :])
        acc = part if acc is None else acc + part
    return acc


def _mixer_ffn_kernel(ya_ref, yb_ref, woa_ref, wob_ref, x_ref, g1_ref, b1_ref, wg_ref, wu_ref, wd_ref, g2_ref, b2_ref,
                      o_ref):
    mix = _dot(ya_ref[...], woa_ref[...]) + _dot(yb_ref[...], wob_ref[...])
    x1 = _layernorm(DEEPNORM_ALPHA * x_ref[...] + mix, g1_ref[...], b1_ref[...])
    ffn = _swiglu_chunks(x1.astype(BF16), wg_ref, wu_ref, wd_ref)
    o_ref[...] = _layernorm(DEEPNORM_ALPHA * x1 + ffn, g2_ref[...], b2_ref[...])


def _mixer_ffn(ya, yb, wo_a, wo_b, x2d, g1, b1, w_gate, w_up, w_down, g2, b2, *, rows):
    n, d = x2d.shape
    once = lambda a: pl.BlockSpec(a.shape, lambda i: (0,) * a.ndim, pipeline_mode=pl.Buffered(1))
    row = lambda w: pl.BlockSpec((rows, w), lambda i: (i, 0))
    vec = lambda v: v.reshape(1, -1)
    args = (ya, yb, wo_a, wo_b, x2d, vec(g1), vec(b1), w_gate, w_up, w_down, vec(g2), vec(b2))
    in_specs = [row(ya.shape[1]), row(yb.shape[1]), once(wo_a), once(wo_b), row(d)] + [once(a) for a in args[5:]]
    return pl.pallas_call(
        _mixer_ffn_kernel,
        grid=(n // rows,),
        in_specs=in_specs,
        out_specs=row(d),
        out_shape=jax.ShapeDtypeStruct((n, d), F32),
        compiler_params=_params(("parallel",)),
        name="mixer_ffn",
    )(*args)


def _s5_kernel(x_ref, perm_ref, win_ref, bd_ref, cd_ref, lam_ref, d_ref, y_hbm, bur_ref, bui_ref, hr_ref, hi_ref,
               st_ref, ys_ref, sem, *, bsz):
    i = pl.program_id(0)
    n_steps = pl.num_programs(0)
    slot = i % 2
    steps = x_ref.shape[1]
    rows = bsz * steps

    def out_copies(at_step, at_slot):
        return [pltpu.make_async_copy(ys_ref.at[at_slot, pl.ds(t * bsz, bsz)], y_hbm.at[:, at_step * steps + t, :],
                                      sem.at[at_slot]) for t in range(steps)]

    @pl.when(i == 0)
    def _():
        st_ref[...] = jnp.zeros_like(st_ref)

    @pl.when(i >= 2)
    def _():
        for cp in out_copies(i - 2, slot):
            cp.wait()

    n_blk, cb, sb2 = bd_ref.shape
    sb = sb2 // 2
    xb = x_ref[...].reshape(rows, x_ref.shape[2]).astype(BF16)
    xb = _dot(perm_ref[...], xb).astype(BF16)
    u = _dot(xb, win_ref[...])
    ub = u.astype(BF16)
    for q in range(n_blk):
        bu = _dot(ub[:, q * cb:(q + 1) * cb], bd_ref[q])
        bur_ref[:, q * sb:(q + 1) * sb] = bu[:, :sb]
        bui_ref[:, q * sb:(q + 1) * sb] = bu[:, sb:]
    for c in range(bur_ref.shape[1] // S5_SCAN_LANES):
        lanes = slice(c * S5_SCAN_LANES, (c + 1) * S5_SCAN_LANES)
        a_re = lam_ref[0:1, lanes]
        a_im = lam_ref[1:2, lanes]
        h_re = st_ref[0, :, lanes]
        h_im = st_ref[1, :, lanes]
        for t in range(steps):
            rws = slice(t * bsz, (t + 1) * bsz)
            n_re = a_re * h_re - a_im * h_im + bur_ref[rws, lanes]
            n_im = a_re * h_im + a_im * h_re + bui_ref[rws, lanes]
            hr_ref[rws, lanes] = n_re.astype(BF16)
            hi_ref[rws, lanes] = n_im.astype(BF16)
            h_re, h_im = n_re, n_im
        st_ref[0, :, lanes] = h_re
        st_ref[1, :, lanes] = h_im
    for q in range(n_blk):
        st = slice(q * sb, (q + 1) * sb)
        ch = slice(q * cb, (q + 1) * cb)
        y = _dot(hr_ref[:, st], cd_ref[q, :sb]) + _dot(hi_ref[:, st], cd_ref[q, sb:])
        ys_ref[slot, :, ch] = y + d_ref[:, ch] * u[:, ch]
    for cp in out_copies(i, slot):
        cp.start()

    @pl.when(i == n_steps - 1)
    def _():
        for cp in out_copies(i, slot):
            cp.wait()

    @pl.when((i == n_steps - 1) & (i >= 1))
    def _():
        for cp in out_copies(i - 1, 1 - slot):
            cp.wait()


def _s5(x2d, w_in, bd, cd, lam, d_skip, *, bsz):
    n, d = x2d.shape
    seq = n // bsz
    rows = S5_STEPS * bsz
    n_state = lam.shape[1]
    full = lambda a: pl.BlockSpec(a.shape, lambda i: (0,) * a.ndim)
    d2 = d_skip.reshape(1, -1)
    r_out = jnp.arange(rows)[:, None]
    r_in = jnp.arange(rows)[None, :]
    perm = (r_in == (r_out % bsz) * S5_STEPS + r_out // bsz).astype(BF16)
    y = pl.pallas_call(
        functools.partial(_s5_kernel, bsz=bsz),
        grid=(seq // S5_STEPS,),
        in_specs=[pl.BlockSpec((bsz, S5_STEPS, d), lambda i: (0, i, 0)), full(perm), full(w_in), full(bd), full(cd),
                  full(lam), full(d2)],
        out_specs=pl.BlockSpec(memory_space=pl.ANY),
        out_shape=jax.ShapeDtypeStruct((bsz, seq, d), F32),
        scratch_shapes=[pltpu.VMEM((rows, n_state), F32), pltpu.VMEM((rows, n_state), F32),
                        pltpu.VMEM((rows, n_state), BF16), pltpu.VMEM((rows, n_state), BF16),
                        pltpu.VMEM((2, bsz, n_state), F32), pltpu.VMEM((2, rows, d), F32),
                        pltpu.SemaphoreType.DMA((2,))],
        compiler_params=_params(("arbitrary",)),
        name="s5",
    )(x2d.reshape(bsz, seq, d), perm, w_in, bd, cd, lam, d2)
    return y.reshape(n, d)


def _s5_weights(a_re, a_im, log_step, b_re, b_im, c_re, c_im):
    n_grp, n_st = a_re.shape
    gpb = S5_GROUPS_PER_BLOCK
    n_blk = n_grp // gpb
    delta = jnp.exp(log_step)[:, None]
    mag = jnp.exp(delta * a_re)
    abar_re = mag * jnp.cos(delta * a_im)
    abar_im = mag * jnp.sin(delta * a_im)
    den = a_re * a_re + a_im * a_im
    coef_re = ((abar_re - 1.0) * a_re + abar_im * a_im) / den
    coef_im = (abar_im * a_re - (abar_re - 1.0) * a_im) / den
    bb_re = coef_re[..., None] * b_re - coef_im[..., None] * b_im
    bb_im = coef_re[..., None] * b_im + coef_im[..., None] * b_re
    same_group = jnp.eye(gpb, dtype=F32)[None, :, None, :, None]

    def expand(w):
        w = jnp.transpose(w.reshape(n_blk, gpb, w.shape[1], w.shape[2]), (0, 1, 3, 2))
        w = w[:, :, :, None, :] * same_group
        return w.reshape(n_blk, gpb * w.shape[2], gpb * w.shape[4])

    bd = jnp.concatenate([expand(bb_re), expand(bb_im)], axis=2).astype(BF16)
    cd = jnp.concatenate([expand(c_re), -expand(c_im)], axis=1).astype(BF16)
    lam = jnp.stack([abar_re.reshape(-1), abar_im.reshape(-1)])
    return bd, cd, lam


def _s5_out_kernel(y_ref, gw_ref, gb_ref, wo_ref, x_ref, g_ref, b_ref, r_ref, o_ref, route_ref, count_ref,
                   carry_ref):
    @pl.when(pl.program_id(0) == 0)
    def _():
        carry_ref[...] = jnp.zeros_like(carry_ref)

    rows = x_ref.shape[0]
    gl = _gelu(y_ref[...])
    z = gl * _sigmoid(_dot(gl.astype(BF16), gw_ref[...]) + gb_ref[...])
    mix = _dot(z.astype(BF16), wo_ref[...])
    xo = _layernorm(DEEPNORM_ALPHA * x_ref[...] + mix, g_ref[...], b_ref[...])
    o_ref[...] = xo
    x_hi = xo.astype(BF16)
    x_lo = (xo - x_hi.astype(F32)).astype(BF16)
    r = r_ref[...]
    r_hi = r.astype(BF16)
    r_lo = (r - r_hi.astype(F32)).astype(BF16)
    logits = _dot(x_hi, r_hi) + (_dot(x_lo, r_hi) + _dot(x_hi, r_lo))
    idx = lax.broadcasted_iota(jnp.int32, logits.shape, 1)
    m1 = jnp.max(logits, axis=-1, keepdims=True)
    i1 = jnp.min(jnp.where(logits == m1, idx, N_EXPERTS), axis=-1, keepdims=True)
    rest = jnp.where(idx == i1, -jnp.inf, logits)
    m2 = jnp.max(rest, axis=-1, keepdims=True)
    i2 = jnp.min(jnp.where(rest == m2, idx, N_EXPERTS), axis=-1, keepdims=True)
    e2 = jnp.exp(m2 - m1)
    g1 = 1.0 / (1.0 + e2)
    g2 = e2 * g1
    sel = jnp.where((idx == i1) | (idx == i2), 1.0, 0.0)
    t_r = lax.broadcasted_iota(jnp.int32, (rows, rows), 0)
    t_c = lax.broadcasted_iota(jnp.int32, (rows, rows), 1)
    earlier = jnp.where(t_c < t_r, 1.0, 0.0).astype(BF16)
    before = _dot(earlier, sel.astype(BF16)) + carry_ref[...]
    r1 = jnp.sum(jnp.where(idx == i1, before, 0.0), axis=-1, keepdims=True)
    r2 = jnp.sum(jnp.where(idx == i2, before, 0.0), axis=-1, keepdims=True)
    total = carry_ref[...] + jnp.sum(sel, axis=0, keepdims=True)
    carry_ref[...] = total
    count_ref[...] = total
    fields = (i1.astype(F32), i2.astype(F32), r1, r2, g1, g2)
    route = jnp.zeros(logits.shape, F32)
    for lane, val in enumerate(fields):
        route = jnp.where(idx == lane, val, route)
    route_ref[...] = route


def _s5_out(y2d, glu_w, glu_b, w_out, x2d, g, b, router, *, rows):
    n, d = x2d.shape
    full = lambda a: pl.BlockSpec(a.shape, lambda i: (0,) * a.ndim)
    row = lambda w: pl.BlockSpec((rows, w), lambda i: (i, 0))
    args = (y2d, glu_w.astype(BF16), glu_b.reshape(1, -1), w_out.astype(BF16), x2d, g.reshape(1, -1),
            b.reshape(1, -1), router)
    in_specs = [row(d), full(args[1]), full(args[2]), full(args[3]), row(d), full(args[5]), full(args[6]),
                full(args[7])]
    return pl.pallas_call(
        _s5_out_kernel,
        grid=(n // rows,),
        in_specs=in_specs,
        out_specs=(row(d), row(N_EXPERTS), pl.BlockSpec((1, N_EXPERTS), lambda i: (0, 0))),
        out_shape=(jax.ShapeDtypeStruct((n, d), F32), jax.ShapeDtypeStruct((n, N_EXPERTS), F32),
                   jax.ShapeDtypeStruct((1, N_EXPERTS), F32)),
        scratch_shapes=[pltpu.VMEM((1, N_EXPERTS), F32)],
        compiler_params=_params(("arbitrary",)),
        name="s5_out",
    )(*args)


def _dispatch_kernel(slot_ref, x_ref, xs_in_ref, xs_ref, sem):
    del xs_in_ref
    rows = x_ref.shape[0]

    def issue(r, carry):
        for k in range(2):
            s = slot_ref[0, 0, 2 * r + k]
            pltpu.make_async_copy(x_ref.at[pl.ds(r, 1)], xs_ref.at[pl.ds(s, 1)], sem).start()
        return carry

    lax.fori_loop(0, rows, issue, 0, unroll=DMA_UNROLL)
    for k in range(2):
        pltpu.make_async_copy(x_ref, xs_ref.at[pl.ds(0, rows)], sem).wait()


def _dispatch(x2d, slots, n_slots, *, rows):
    n, d = x2d.shape
    return pl.pallas_call(
        _dispatch_kernel,
        grid=(n // rows,),
        in_specs=[pl.BlockSpec((1, 1, 2 * rows), lambda i: (i, 0, 0), memory_space=pltpu.SMEM),
                  pl.BlockSpec((rows, d), lambda i: (i, 0)),
                  pl.BlockSpec(memory_space=pl.ANY)],
        out_specs=pl.BlockSpec(memory_space=pl.ANY),
        out_shape=jax.ShapeDtypeStruct((n_slots, d), F32),
        scratch_shapes=[pltpu.SemaphoreType.DMA],
        input_output_aliases={2: 0},
        compiler_params=_params(("arbitrary",)),
        name="moe_dispatch",
    )(slots, x2d, jnp.zeros((n_slots, d), F32))


def _ffn_grouped_kernel(te_ref, na_ref, x_ref, wg_ref, wu_ref, wd_ref, o_ref, xb_ref, acc_ref):
    del te_ref
    i = pl.program_id(0)
    f = pl.program_id(1)
    active = i < na_ref[0]

    @pl.when(active)
    def _():
        @pl.when(f == 0)
        def _():
            xb_ref[...] = x_ref[...].astype(BF16)

        contrib = _swiglu_chunks(xb_ref[...], wg_ref.at[0], wu_ref.at[0], wd_ref.at[0])

        @pl.when(f == 0)
        def _():
            acc_ref[...] = contrib

        @pl.when(f > 0)
        def _():
            acc_ref[...] += contrib

    last = f == pl.num_programs(1) - 1

    @pl.when(last & active)
    def _():
        o_ref[...] = acc_ref[...]

    @pl.when(last & jnp.logical_not(active))
    def _():
        o_ref[...] = jnp.zeros_like(o_ref)


def _ffn_grouped(xs, tile_expert, n_active, w_gate, w_up, w_down, *, rows, fchunk):
    m, d = xs.shape
    n_f = w_gate.shape[2] // fchunk
    tile = lambda i, na: jnp.maximum(jnp.minimum(i, na[0] - 1), 0)
    chunk = lambda i, f, na: jnp.where(i < na[0], f, n_f - 1)
    return pl.pallas_call(
        _ffn_grouped_kernel,
        grid_spec=pltpu.PrefetchScalarGridSpec(
            num_scalar_prefetch=2,
            grid=(m // rows, n_f),
            in_specs=[pl.BlockSpec((rows, d), lambda i, f, te, na: (tile(i, na), 0)),
                      pl.BlockSpec((1, d, fchunk), lambda i, f, te, na: (te[tile(i, na)], 0, chunk(i, f, na))),
                      pl.BlockSpec((1, d, fchunk), lambda i, f, te, na: (te[tile(i, na)], 0, chunk(i, f, na))),
                      pl.BlockSpec((1, fchunk, d), lambda i, f, te, na: (te[tile(i, na)], chunk(i, f, na), 0))],
            out_specs=pl.BlockSpec((rows, d), lambda i, f, te, na: (i, 0)),
            scratch_shapes=[pltpu.VMEM((rows, d), BF16), pltpu.VMEM((rows, d), F32)],
        ),
        out_shape=jax.ShapeDtypeStruct((m, d), F32),
        compiler_params=_params(("arbitrary", "arbitrary")),
        name="ffn_moe",
    )(tile_expert, n_active, xs, w_gate, w_up, w_down)


def _combine_kernel(slot_ref, x_ref, route_ref, ys_ref, g_ref, b_ref, o_ref, y1_ref, y2_ref, sem):
    rows = x_ref.shape[0]

    def issue(r, carry):
        for k, dst in enumerate((y1_ref, y2_ref)):
            s = slot_ref[0, 0, 2 * r + k]
            pltpu.make_async_copy(ys_ref.at[pl.ds(s, 1)], dst.at[pl.ds(r, 1)], sem).start()
        return carry

    lax.fori_loop(0, rows, issue, 0, unroll=DMA_UNROLL)
    for dst in (y1_ref, y2_ref):
        pltpu.make_async_copy(ys_ref.at[pl.ds(0, rows)], dst, sem).wait()
    route = route_ref[...]
    moe = route[:, 4:5] * y1_ref[...] + route[:, 5:6] * y2_ref[...]
    o_ref[...] = _layernorm(DEEPNORM_ALPHA * x_ref[...] + moe, g_ref[...], b_ref[...])


def _combine_ln(x2d, route, slots, ys, g, b, *, rows):
    n, d = x2d.shape
    g2, b2 = g.reshape(1, -1), b.reshape(1, -1)
    return pl.pallas_call(
        _combine_kernel,
        grid=(n // rows,),
        in_specs=[pl.BlockSpec((1, 1, 2 * rows), lambda i: (i, 0, 0), memory_space=pltpu.SMEM),
                  pl.BlockSpec((rows, d), lambda i: (i, 0)),
                  pl.BlockSpec((rows, N_EXPERTS), lambda i: (i, 0)),
                  pl.BlockSpec(memory_space=pl.ANY),
                  pl.BlockSpec((1, d), lambda i: (0, 0)),
                  pl.BlockSpec((1, d), lambda i: (0, 0))],
        out_specs=pl.BlockSpec((rows, d), lambda i: (i, 0)),
        out_shape=jax.ShapeDtypeStruct((n, d), F32),
        scratch_shapes=[pltpu.VMEM((rows, d), F32), pltpu.VMEM((rows, d), F32), pltpu.SemaphoreType.DMA],
        compiler_params=_params(("arbitrary",)),
        name="moe_combine",
    )(slots, x2d, route, ys, g2, b2)


def _moe(x2d, route, counts, w_gate, w_up, w_down, g, b, *, rows, fchunk):
    n, _ = x2d.shape
    counts = counts.reshape(-1).astype(jnp.int32)
    padded = (counts + rows - 1) // rows * rows
    ends = jnp.cumsum(padded)
    offs = ends - padded
    n_tiles = (2 * n) // rows + N_EXPERTS
    starts = jnp.arange(n_tiles, dtype=jnp.int32) * rows
    tile_expert = jnp.minimum(jnp.sum(ends[None, :] <= starts[:, None], axis=1), N_EXPERTS - 1).astype(jnp.int32)
    n_active = (ends[-1:] // rows).astype(jnp.int32)
    expert = route[:, 0:2].astype(jnp.int32)
    rank = route[:, 2:4].astype(jnp.int32)
    base = jnp.sum(jnp.where(expert[..., None] == jnp.arange(N_EXPERTS), offs, 0), axis=-1)
    slots = (base + rank).reshape(n // rows, 1, 2 * rows)
    xs = _dispatch(x2d, slots, n_tiles * rows, rows=rows)
    ys = _ffn_grouped(xs, tile_expert, n_active, w_gate, w_up, w_down, rows=rows, fchunk=fchunk)
    return _combine_ln(x2d, route, slots, ys, g, b, rows=rows)


def _layer_even(x2d, positions, bsz, seq, w_in, gm_ln_g, gm_ln_b, w_s, b_s, q_norm, w_uq, kv_norm, w_ukv, w_out,
                ln_g, ln_b, f_gate, f_up, f_down, f_ln_g, f_ln_b, *, rows, ffn_rows, attn_blk):
    cos_p, sin_p = _rope_tables(positions)
    ya, q, k, v = _mixer_in(x2d, cos_p, sin_p, w_in, gm_ln_g, gm_ln_b, w_s, b_s, q_norm, w_uq, kv_norm, w_ukv,
                            rows=rows)
    yb = _attention(q, k, v, bsz=bsz, seq=seq, blk=attn_blk)
    wo = w_out.astype(BF16)
    return _mixer_ffn(ya, yb, wo[:A_WIDTH], wo[A_WIDTH:], x2d, ln_g, ln_b, f_gate.astype(BF16), f_up.astype(BF16),
                      f_down.astype(BF16), f_ln_g, f_ln_b, rows=ffn_rows)


def _layer_odd(x2d, bsz, w_in, a_re, a_im, log_step, b_re, b_im, c_re, c_im, d_skip, glu_w, glu_b, w_out, ln_g, ln_b,
               router, m_gate, m_up, m_down, m_ln_g, m_ln_b, *, rows, ffn_rows, fchunk):
    bd, cd, lam = _s5_weights(a_re, a_im, log_step, b_re, b_im, c_re, c_im)
    y2d = _s5(x2d, w_in.astype(BF16), bd, cd, lam, d_skip, bsz=bsz)
    x2d, route, counts = _s5_out(y2d, glu_w, glu_b, w_out, x2d, ln_g, ln_b, router, rows=rows)
    return _moe(x2d, route, counts, m_gate.astype(BF16), m_up.astype(BF16), m_down.astype(BF16), m_ln_g, m_ln_b,
                rows=ffn_rows, fchunk=fchunk)


def kernel(x, positions, ab_w_in, gm_ln_g, gm_ln_b, gm_w_s, gm_b_s, mla_q_norm, mla_w_uq, mla_kv_norm, mla_w_ukv, ab_w_out, ab_ln_g, ab_ln_b, ffd_w_gate, ffd_w_up, ffd_w_down, ffd_ln_g, ffd_ln_b, c_w_in, s5_a_re, s5_a_im, s5_log_step, s5_b_re, s5_b_im, s5_c_re, s5_c_im, s5_d, glu_w, glu_b, c_w_out, c_ln_g, c_ln_b, moe_router, moe_w_gate, moe_w_up, moe_w_down, moe_ln_g, moe_ln_b):
    bsz, seq, d = x.shape
    x2d = x.reshape(bsz * seq, d)
    rows = min(512, seq)
    for i in range(DEPTH):
        j = i // 2
        if i % 2 == 0:
            x2d = _layer_even(x2d, positions, bsz, seq, ab_w_in[j], gm_ln_g[j], gm_ln_b[j], gm_w_s[j], gm_b_s[j],
                              mla_q_norm[j], mla_w_uq[j], mla_kv_norm[j], mla_w_ukv[j], ab_w_out[j], ab_ln_g[j],
                              ab_ln_b[j], ffd_w_gate[j], ffd_w_up[j], ffd_w_down[j], ffd_ln_g[j], ffd_ln_b[j],
                              rows=rows, ffn_rows=rows, attn_blk=min(512, seq))
        else:
            x2d = _layer_odd(x2d, bsz, c_w_in[j], s5_a_re[j], s5_a_im[j], s5_log_step[j], s5_b_re[j],
                             s5_b_im[j], s5_c_re[j], s5_c_im[j], s5_d[j], glu_w[j], glu_b[j], c_w_out[j], c_ln_g[j],
                             c_ln_b[j], moe_router[j], moe_w_gate[j], moe_w_up[j], moe_w_down[j], moe_ln_g[j],
                             moe_ln_b[j], rows=rows, ffn_rows=rows, fchunk=moe_w_gate.shape[3] // 2)
    return x2d.reshape(bsz, seq, d)
```

```python
import functools
import math

import jax
import jax.numpy as jnp
from jax import lax
from jax.experimental import pallas as pl
from jax.experimental.pallas import tpu as pltpu

F32 = jnp.float32
BF16 = jnp.bfloat16

A_GROUPS = 4
A_GROUP_DIM = 128
A_WIDTH = A_GROUPS * A_GROUP_DIM
A_CHUNK = 128
MLA_HEADS = 8
QK_NOPE = 64
QK_ROPE = 32
QK_HEAD = QK_NOPE + QK_ROPE
V_HEAD = 64
Q_LORA = 384
KV_LORA = 256
ROPE_THETA = 10000.0
S5_GROUP_DIM = 16
S5_STATE = 64
N_EXPERTS = 8
LN_EPS = 1e-5
RMS_EPS = 1e-6
DEPTH = 2
DEEPNORM_ALPHA = (2.0 * DEPTH) ** 0.25

LANES = 128
HEAD_PAD = LANES
S5_GROUPS_PER_BLOCK = 2 * LANES // S5_GROUP_DIM
S5_STEPS = 16
S5_SCAN_LANES = 4 * LANES
VMEM_LIMIT = 56 * 1024 * 1024
NEG_BIG = -1e30
FFN_CHUNK = 512
DMA_UNROLL = 16


def _params(sem):
    return pltpu.CompilerParams(dimension_semantics=sem, vmem_limit_bytes=VMEM_LIMIT)


def _gelu(x):
    c = math.sqrt(2.0 / math.pi)
    return 0.5 * x * (1.0 + jnp.tanh(c * (x + 0.044715 * (x * x * x))))


def _sigmoid(x):
    return 1.0 / (1.0 + jnp.exp(-x))


def _layernorm(x, g, b):
    mu = jnp.mean(x, axis=-1, keepdims=True)
    xc = x - mu
    var = jnp.mean(xc * xc, axis=-1, keepdims=True)
    return xc * lax.rsqrt(var + LN_EPS) * g + b


def _rmsnorm(x, g):
    ms = jnp.mean(x * x, axis=-1, keepdims=True)
    return x * lax.rsqrt(ms + RMS_EPS) * g


def _dot(a, b):
    return jnp.dot(a, b, preferred_element_type=F32)


def _rope_table_kernel(inv_ref, pos_ref, cos_ref, sin_ref):
    pos = pos_ref[...].astype(F32)
    for j in range(QK_ROPE // 2):
        ang = pos * inv_ref[j]
        cos_ref[j] = jnp.cos(ang)
        sin_ref[j] = jnp.sin(ang)


def _rope_tables(positions):
    n = positions.size
    half = QK_ROPE // 2
    inv_freq = 1.0 / (ROPE_THETA ** (jnp.arange(0, QK_ROPE, 2, dtype=F32) / QK_ROPE))
    pos2d = positions.reshape(n // LANES, LANES)
    cos_t, sin_t = pl.pallas_call(
        _rope_table_kernel,
        out_shape=(jax.ShapeDtypeStruct((half, n // LANES, LANES), F32),) * 2,
        in_specs=[pl.BlockSpec(memory_space=pltpu.SMEM), pl.BlockSpec(memory_space=pltpu.VMEM)],
        out_specs=(pl.BlockSpec(memory_space=pltpu.VMEM),) * 2,
        name="rope_table",
    )(inv_freq, pos2d)
    cos_c = cos_t.reshape(half, n).T
    sin_c = sin_t.reshape(half, n).T
    ones = jnp.ones((n, QK_NOPE), F32)
    zeros_n = jnp.zeros((n, QK_NOPE), F32)
    zeros_p = jnp.zeros((n, HEAD_PAD - QK_HEAD), F32)
    cos_p = jnp.concatenate([ones, cos_c, cos_c, zeros_p], axis=1)
    sin_p = jnp.concatenate([zeros_n, sin_c, sin_c, zeros_p], axis=1)
    return cos_p, sin_p


def _mixer_in_kernel(x_ref, cos_ref, sin_ref, w_in_ref, lng_ref, lnb_ref, ws_ref, bs_ref, qn_ref, wq_ref,
                     kvn_ref, wkv_ref, ya_ref, q_ref, k_ref, v_ref):
    rows = x_ref.shape[0]
    xb = x_ref[...].astype(BF16)
    h = _dot(xb, w_in_ref[...])
    o_q = 2 * A_WIDTH
    o_kv = o_q + Q_LORA
    o_pe = o_kv + KV_LORA
    o_rot = o_pe + HEAD_PAD
    cos_p = cos_ref[...]
    sin_p = sin_ref[...]

    a_u = _gelu(h[:, :A_WIDTH])
    a_v = _gelu(h[:, A_WIDTH:o_q])
    vn = _layernorm(a_v, lng_ref[...], lnb_ref[...]).astype(BF16)
    t_idx = lax.broadcasted_iota(jnp.int32, (A_CHUNK, A_CHUNK), 0)
    s_idx = lax.broadcasted_iota(jnp.int32, (A_CHUNK, A_CHUNK), 1)
    causal = s_idx <= t_idx
    bs = bs_ref[...]
    for g in range(A_GROUPS):
        w_g = jnp.where(causal, ws_ref[g], 0.0).astype(BF16)
        cols = slice(g * A_GROUP_DIM, (g + 1) * A_GROUP_DIM)
        for c in range(rows // A_CHUNK):
            rws = slice(c * A_CHUNK, (c + 1) * A_CHUNK)
            mixed = _dot(w_g, vn[rws, cols]) + bs[:, cols]
            ya_ref[rws, cols] = (a_u[rws, cols] * mixed).astype(BF16)

    cqn = _rmsnorm(h[:, o_q:o_kv], qn_ref[...]).astype(BF16)
    q2 = _dot(cqn, wq_ref[...])
    half = MLA_HEADS * HEAD_PAD
    for hd in range(MLA_HEADS):
        cols = slice(hd * HEAD_PAD, (hd + 1) * HEAD_PAD)
        rot = slice(half + hd * HEAD_PAD, half + (hd + 1) * HEAD_PAD)
        q_ref[:, cols] = (q2[:, cols] * cos_p + q2[:, rot] * sin_p).astype(BF16)

    ckvn = _rmsnorm(h[:, o_kv:o_pe], kvn_ref[...]).astype(BF16)
    kv = _dot(ckvn, wkv_ref[...])
    kpe = h[:, o_pe:o_rot] * cos_p + h[:, o_rot:o_rot + HEAD_PAD] * sin_p
    for hd in range(MLA_HEADS):
        cols = slice(hd * HEAD_PAD, (hd + 1) * HEAD_PAD)
        k_ref[:, cols] = (kv[:, cols] + kpe).astype(BF16)
    v_ref[...] = kv[:, half:].astype(BF16)


def _mixer_in(x2d, cos_p, sin_p, w_in, gm_ln_g, gm_ln_b, w_s, b_s, q_norm, w_uq, kv_norm, w_ukv, *, rows):
    n, d = x2d.shape
    hp = MLA_HEADS * HEAD_PAD
    o_pe = 2 * A_WIDTH + Q_LORA + KV_LORA
    half = QK_ROPE // 2
    w_pe = w_in[:, o_pe:o_pe + QK_ROPE]
    w_pe_rot = jnp.concatenate([-w_pe[:, half:], w_pe[:, :half]], axis=1)
    pad_l = jnp.zeros((d, QK_NOPE), F32)
    pad_r = jnp.zeros((d, HEAD_PAD - QK_HEAD), F32)
    w_in_p = jnp.concatenate([w_in[:, :o_pe], pad_l, w_pe, pad_r, pad_l, w_pe_rot, pad_r], axis=1).astype(BF16)
    wq = w_uq.reshape(Q_LORA, MLA_HEADS, QK_HEAD)
    wq_pe = wq[:, :, QK_NOPE:]
    wq_rot = jnp.concatenate([jnp.zeros((Q_LORA, MLA_HEADS, QK_NOPE), F32), -wq_pe[:, :, half:], wq_pe[:, :, :half]],
                             axis=2)
    padq = ((0, 0), (0, 0), (0, HEAD_PAD - QK_HEAD))
    wq2 = jnp.concatenate([jnp.pad(wq, padq).reshape(Q_LORA, hp), jnp.pad(wq_rot, padq).reshape(Q_LORA, hp)],
                          axis=1).astype(BF16)
    wkv = w_ukv.reshape(KV_LORA, MLA_HEADS, QK_NOPE + V_HEAD)
    wk = jnp.pad(wkv[:, :, :QK_NOPE], ((0, 0), (0, 0), (0, HEAD_PAD - QK_NOPE))).reshape(KV_LORA, hp)
    wv = wkv[:, :, QK_NOPE:].reshape(KV_LORA, MLA_HEADS * V_HEAD)
    wkv2 = jnp.concatenate([wk, wv], axis=1).astype(BF16)
    bs_full = jnp.repeat(b_s.T, A_GROUP_DIM, axis=1)

    full = lambda a: pl.BlockSpec(a.shape, lambda i: (0,) * a.ndim)
    row = lambda w: pl.BlockSpec((rows, w), lambda i: (i, 0))
    args = (x2d, cos_p, sin_p, w_in_p, gm_ln_g.reshape(1, -1), gm_ln_b.reshape(1, -1), w_s, bs_full,
            q_norm.reshape(1, -1), wq2, kv_norm.reshape(1, -1), wkv2)
    in_specs = [row(d), row(HEAD_PAD), row(HEAD_PAD)] + [full(a) for a in args[3:]]
    return pl.pallas_call(
        _mixer_in_kernel,
        grid=(n // rows,),
        in_specs=in_specs,
        out_specs=(row(A_WIDTH), row(hp), row(hp), row(MLA_HEADS * V_HEAD)),
        out_shape=(jax.ShapeDtypeStruct((n, A_WIDTH), BF16), jax.ShapeDtypeStruct((n, hp), BF16),
                   jax.ShapeDtypeStruct((n, hp), BF16), jax.ShapeDtypeStruct((n, MLA_HEADS * V_HEAD), BF16)),
        compiler_params=_params(("parallel",)),
        name="mixer_in",
    )(*args)


def _attn_kernel(q_ref, k_ref, v_ref, o_ref, *, blk):
    seq = q_ref.shape[0]
    scale = QK_HEAD ** -0.5
    row = lax.broadcasted_iota(jnp.int32, (blk, blk), 0)
    col = lax.broadcasted_iota(jnp.int32, (blk, blk), 1)
    diag_mask = col <= row
    first_head_lanes = lax.broadcasted_iota(jnp.int32, (blk, 2 * V_HEAD), 1) < V_HEAD

    qk = lambda a, b: lax.dot_general(a, b, (((1,), (1,)), ((), ())), preferred_element_type=F32)

    for j in range(seq // blk):
        q0 = j * blk
        outs = []
        for hh in range(2):
            cols = slice(hh * HEAD_PAD, (hh + 1) * HEAD_PAD)
            q = q_ref[q0:q0 + blk, cols]
            s_d = jnp.where(diag_mask, qk(q, k_ref[q0:q0 + blk, cols]), NEG_BIG)
            m = jnp.max(s_d, axis=-1, keepdims=True)
            if j > 0:
                s_o = qk(q, k_ref[0:q0, cols])
                m = jnp.maximum(m, jnp.max(s_o, axis=-1, keepdims=True))
            p_d = jnp.exp((s_d - m) * scale)
            l = jnp.sum(p_d, axis=-1, keepdims=True)
            acc = _dot(p_d.astype(BF16), v_ref[q0:q0 + blk, :])
            if j > 0:
                p_o = jnp.exp((s_o - m) * scale)
                l = l + jnp.sum(p_o, axis=-1, keepdims=True)
                acc = acc + _dot(p_o.astype(BF16), v_ref[0:q0, :])
            outs.append(acc / l)
        o_ref[q0:q0 + blk, :] = jnp.where(first_head_lanes, outs[0], outs[1]).astype(BF16)


def _attention(q, k, v, *, bsz, seq, blk):
    n = bsz * seq
    pair = 2 * HEAD_PAD
    return pl.pallas_call(
        functools.partial(_attn_kernel, blk=blk),
        grid=(bsz, MLA_HEADS // 2),
        in_specs=[pl.BlockSpec((seq, pair), lambda b, h: (b, h)), pl.BlockSpec((seq, pair), lambda b, h: (b, h)),
                  pl.BlockSpec((seq, 2 * V_HEAD), lambda b, h: (b, h))],
        out_specs=pl.BlockSpec((seq, 2 * V_HEAD), lambda b, h: (b, h)),
        out_shape=jax.ShapeDtypeStruct((n, MLA_HEADS * V_HEAD), BF16),
        compiler_params=_params(("parallel", "parallel")),
        name="attention",
    )(q, k, v)


def _swiglu_chunks(xb, wg_ref, wu_ref, wd_ref):
    dff = wg_ref.shape[1]
    acc = None
    for c0 in range(0, dff, FFN_CHUNK):
        cols = slice(c0, min(c0 + FFN_CHUNK, dff))
        gate = _dot(xb, wg_ref[:, cols])
        up = _dot(xb, wu_ref[:, cols])
        hid = (gate * _sigmoid(gate) * up).astype(BF16)
        part = _dot(hid, wd_ref[cols, :])
        acc = part if acc is None else acc + part
    return acc


def _mixer_ffn_kernel(ya_ref, yb_ref, woa_ref, wob_ref, x_ref, g1_ref, b1_ref, wg_ref, wu_ref, wd_ref, g2_ref, b2_ref,
                      o_ref):
    mix = _dot(ya_ref[...], woa_ref[...]) + _dot(yb_ref[...], wob_ref[...])
    x1 = _layernorm(DEEPNORM_ALPHA * x_ref[...] + mix, g1_ref[...], b1_ref[...])
    ffn = _swiglu_chunks(x1.astype(BF16), wg_ref, wu_ref, wd_ref)
    o_ref[...] = _layernorm(DEEPNORM_ALPHA * x1 + ffn, g2_ref[...], b2_ref[...])


def _mixer_ffn(ya, yb, wo_a, wo_b, x2d, g1, b1, w_gate, w_up, w_down, g2, b2, *, rows):
    n, d = x2d.shape
    once = lambda a: pl.BlockSpec(a.shape, lambda i: (0,) * a.ndim, pipeline_mode=pl.Buffered(1))
    row = lambda w: pl.BlockSpec((rows, w), lambda i: (i, 0))
    vec = lambda v: v.reshape(1, -1)
    args = (ya, yb, wo_a, wo_b, x2d, vec(g1), vec(b1), w_gate, w_up, w_down, vec(g2), vec(b2))
    in_specs = [row(ya.shape[1]), row(yb.shape[1]), once(wo_a), once(wo_b), row(d)] + [once(a) for a in args[5:]]
    return pl.pallas_call(
        _mixer_ffn_kernel,
        grid=(n // rows,),
        in_specs=in_specs,
        out_specs=row(d),
        out_shape=jax.ShapeDtypeStruct((n, d), F32),
        compiler_params=_params(("parallel",)),
        name="mixer_ffn",
    )(*args)


def _s5_kernel(x_ref, perm_ref, win_ref, bd_ref, cd_ref, lam_ref, d_ref, y_hbm, bur_ref, bui_ref, hr_ref, hi_ref,
               st_ref, ys_ref, sem, *, bsz):
    i = pl.program_id(0)
    n_steps = pl.num_programs(0)
    slot = i % 2
    steps = x_ref.shape[1]
    rows = bsz * steps

    def out_copies(at_step, at_slot):
        return [pltpu.make_async_copy(ys_ref.at[at_slot, pl.ds(t * bsz, bsz)], y_hbm.at[:, at_step * steps + t, :],
                                      sem.at[at_slot]) for t in range(steps)]

    @pl.when(i == 0)
    def _():
        st_ref[...] = jnp.zeros_like(st_ref)

    @pl.when(i >= 2)
    def _():
        for cp in out_copies(i - 2, slot):
            cp.wait()

    n_blk, cb, sb2 = bd_ref.shape
    sb = sb2 // 2
    xb = x_ref[...].reshape(rows, x_ref.shape[2]).astype(BF16)
    xb = _dot(perm_ref[...], xb).astype(BF16)
    u = _dot(xb, win_ref[...])
    ub = u.astype(BF16)
    for q in range(n_blk):
        bu = _dot(ub[:, q * cb:(q + 1) * cb], bd_ref[q])
        bur_ref[:, q * sb:(q + 1) * sb] = bu[:, :sb]
        bui_ref[:, q * sb:(q + 1) * sb] = bu[:, sb:]
    for c in range(bur_ref.shape[1] // S5_SCAN_LANES):
        lanes = slice(c * S5_SCAN_LANES, (c + 1) * S5_SCAN_LANES)
        a_re = lam_ref[0:1, lanes]
        a_im = lam_ref[1:2, lanes]
        h_re = st_ref[0, :, lanes]
        h_im = st_ref[1, :, lanes]
        for t in range(steps):
            rws = slice(t * bsz, (t + 1) * bsz)
            n_re = a_re * h_re - a_im * h_im + bur_ref[rws, lanes]
            n_im = a_re * h_im + a_im * h_re + bui_ref[rws, lanes]
            hr_ref[rws, lanes] = n_re.astype(BF16)
            hi_ref[rws, lanes] = n_im.astype(BF16)
            h_re, h_im = n_re, n_im
        st_ref[0, :, lanes] = h_re
        st_ref[1, :, lanes] = h_im
    for q in range(n_blk):
        st = slice(q * sb, (q + 1) * sb)
        ch = slice(q * cb, (q + 1) * cb)
        y = _dot(hr_ref[:, st], cd_ref[q, :sb]) + _dot(hi_ref[:, st], cd_ref[q, sb:])
        ys_ref[slot, :, ch] = y + d_ref[:, ch] * u[:, ch]
    for cp in out_copies(i, slot):
        cp.start()

    @pl.when(i == n_steps - 1)
    def _():
        for cp in out_copies(i, slot):
            cp.wait()

    @pl.when((i == n_steps - 1) & (i >= 1))
    def _():
        for cp in out_copies(i - 1, 1 - slot):
            cp.wait()


def _s5(x2d, w_in, bd, cd, lam, d_skip, *, bsz):
    n, d = x2d.shape
    seq = n // bsz
    rows = S5_STEPS * bsz
    n_state = lam.shape[1]
    full = lambda a: pl.BlockSpec(a.shape, lambda i: (0,) * a.ndim)
    d2 = d_skip.reshape(1, -1)
    r_out = jnp.arange(rows)[:, None]
    r_in = jnp.arange(rows)[None, :]
    perm = (r_in == (r_out % bsz) * S5_STEPS + r_out // bsz).astype(BF16)
    y = pl.pallas_call(
        functools.partial(_s5_kernel, bsz=bsz),
        grid=(seq // S5_STEPS,),
        in_specs=[pl.BlockSpec((bsz, S5_STEPS, d), lambda i: (0, i, 0)), full(perm), full(w_in), full(bd), full(cd),
                  full(lam), full(d2)],
        out_specs=pl.BlockSpec(memory_space=pl.ANY),
        out_shape=jax.ShapeDtypeStruct((bsz, seq, d), F32),
        scratch_shapes=[pltpu.VMEM((rows, n_state), F32), pltpu.VMEM((rows, n_state), F32),
                        pltpu.VMEM((rows, n_state), BF16), pltpu.VMEM((rows, n_state), BF16),
                        pltpu.VMEM((2, bsz, n_state), F32), pltpu.VMEM((2, rows, d), F32),
                        pltpu.SemaphoreType.DMA((2,))],
        compiler_params=_params(("arbitrary",)),
        name="s5",
    )(x2d.reshape(bsz, seq, d), perm, w_in, bd, cd, lam, d2)
    return y.reshape(n, d)


def _s5_weights(a_re, a_im, log_step, b_re, b_im, c_re, c_im):
    n_grp, n_st = a_re.shape
    gpb = S5_GROUPS_PER_BLOCK
    n_blk = n_grp // gpb
    delta = jnp.exp(log_step)[:, None]
    mag = jnp.exp(delta * a_re)
    abar_re = mag * jnp.cos(delta * a_im)
    abar_im = mag * jnp.sin(delta * a_im)
    den = a_re * a_re + a_im * a_im
    coef_re = ((abar_re - 1.0) * a_re + abar_im * a_im) / den
    coef_im = (abar_im * a_re - (abar_re - 1.0) * a_im) / den
    bb_re = coef_re[..., None] * b_re - coef_im[..., None] * b_im
    bb_im = coef_re[..., None] * b_im + coef_im[..., None] * b_re
    same_group = jnp.eye(gpb, dtype=F32)[None, :, None, :, None]

    def expand(w):
        w = jnp.transpose(w.reshape(n_blk, gpb, w.shape[1], w.shape[2]), (0, 1, 3, 2))
        w = w[:, :, :, None, :] * same_group
        return w.reshape(n_blk, gpb * w.shape[2], gpb * w.shape[4])

    bd = jnp.concatenate([expand(bb_re), expand(bb_im)], axis=2).astype(BF16)
    cd = jnp.concatenate([expand(c_re), -expand(c_im)], axis=1).astype(BF16)
    lam = jnp.stack([abar_re.reshape(-1), abar_im.reshape(-1)])
    return bd, cd, lam


def _s5_out_kernel(y_ref, gw_ref, gb_ref, wo_ref, x_ref, g_ref, b_ref, r_ref, o_ref, route_ref, count_ref,
                   carry_ref):
    @pl.when(pl.program_id(0) == 0)
    def _():
        carry_ref[...] = jnp.zeros_like(carry_ref)

    rows = x_ref.shape[0]
    gl = _gelu(y_ref[...])
    z = gl * _sigmoid(_dot(gl.astype(BF16), gw_ref[...]) + gb_ref[...])
    mix = _dot(z.astype(BF16), wo_ref[...])
    xo = _layernorm(DEEPNORM_ALPHA * x_ref[...] + mix, g_ref[...], b_ref[...])
    o_ref[...] = xo
    x_hi = xo.astype(BF16)
    x_lo = (xo - x_hi.astype(F32)).astype(BF16)
    r = r_ref[...]
    r_hi = r.astype(BF16)
    r_lo = (r - r_hi.astype(F32)).astype(BF16)
    logits = _dot(x_hi, r_hi) + (_dot(x_lo, r_hi) + _dot(x_hi, r_lo))
    idx = lax.broadcasted_iota(jnp.int32, logits.shape, 1)
    m1 = jnp.max(logits, axis=-1, keepdims=True)
    i1 = jnp.min(jnp.where(logits == m1, idx, N_EXPERTS), axis=-1, keepdims=True)
    rest = jnp.where(idx == i1, -jnp.inf, logits)
    m2 = jnp.max(rest, axis=-1, keepdims=True)
    i2 = jnp.min(jnp.where(rest == m2, idx, N_EXPERTS), axis=-1, keepdims=True)
    e2 = jnp.exp(m2 - m1)
    g1 = 1.0 / (1.0 + e2)
    g2 = e2 * g1
    sel = jnp.where((idx == i1) | (idx == i2), 1.0, 0.0)
    t_r = lax.broadcasted_iota(jnp.int32, (rows, rows), 0)
    t_c = lax.broadcasted_iota(jnp.int32, (rows, rows), 1)
    earlier = jnp.where(t_c < t_r, 1.0, 0.0).astype(BF16)
    before = _dot(earlier, sel.astype(BF16)) + carry_ref[...]
    r1 = jnp.sum(jnp.where(idx == i1, before, 0.0), axis=-1, keepdims=True)
    r2 = jnp.sum(jnp.where(idx == i2, before, 0.0), axis=-1, keepdims=True)
    total = carry_ref[...] + jnp.sum(sel, axis=0, keepdims=True)
    carry_ref[...] = total
    count_ref[...] = total
    fields = (i1.astype(F32), i2.astype(F32), r1, r2, g1, g2)
    route = jnp.zeros(logits.shape, F32)
    for lane, val in enumerate(fields):
        route = jnp.where(idx == lane, val, route)
    route_ref[...] = route


def _s5_out(y2d, glu_w, glu_b, w_out, x2d, g, b, router, *, rows):
    n, d = x2d.shape
    full = lambda a: pl.BlockSpec(a.shape, lambda i: (0,) * a.ndim)
    row = lambda w: pl.BlockSpec((rows, w), lambda i: (i, 0))
    args = (y2d, glu_w.astype(BF16), glu_b.reshape(1, -1), w_out.astype(BF16), x2d, g.reshape(1, -1),
            b.reshape(1, -1), router)
    in_specs = [row(d), full(args[1]), full(args[2]), full(args[3]), row(d), full(args[5]), full(args[6]),
                full(args[7])]
    return pl.pallas_call(
        _s5_out_kernel,
        grid=(n // rows,),
        in_specs=in_specs,
        out_specs=(row(d), row(N_EXPERTS), pl.BlockSpec((1, N_EXPERTS), lambda i: (0, 0))),
        out_shape=(jax.ShapeDtypeStruct((n, d), F32), jax.ShapeDtypeStruct((n, N_EXPERTS), F32),
                   jax.ShapeDtypeStruct((1, N_EXPERTS), F32)),
        scratch_shapes=[pltpu.VMEM((1, N_EXPERTS), F32)],
        compiler_params=_params(("arbitrary",)),
        name="s5_out",
    )(*args)


def _dispatch_kernel(slot_ref, x_ref, xs_in_ref, xs_ref, sem):
    del xs_in_ref
    rows = x_ref.shape[0]

    def issue(r, carry):
        for k in range(2):
            s = slot_ref[0, 0, 2 * r + k]
            pltpu.make_async_copy(x_ref.at[pl.ds(r, 1)], xs_ref.at[pl.ds(s, 1)], sem).start()
        return carry

    lax.fori_loop(0, rows, issue, 0, unroll=DMA_UNROLL)
    for k in range(2):
        pltpu.make_async_copy(x_ref, xs_ref.at[pl.ds(0, rows)], sem).wait()


def _dispatch(x2d, slots, n_slots, *, rows):
    n, d = x2d.shape
    return pl.pallas_call(
        _dispatch_kernel,
        grid=(n // rows,),
        in_specs=[pl.BlockSpec((1, 1, 2 * rows), lambda i: (i, 0, 0), memory_space=pltpu.SMEM),
                  pl.BlockSpec((rows, d), lambda i: (i, 0)),
                  pl.BlockSpec(memory_space=pl.ANY)],
        out_specs=pl.BlockSpec(memory_space=pl.ANY),
        out_shape=jax.ShapeDtypeStruct((n_slots, d), F32),
        scratch_shapes=[pltpu.SemaphoreType.DMA],
        input_output_aliases={2: 0},
        compiler_params=_params(("arbitrary",)),
        name="moe_dispatch",
    )(slots, x2d, jnp.zeros((n_slots, d), F32))


def _ffn_grouped_kernel(te_ref, na_ref, x_ref, wg_ref, wu_ref, wd_ref, o_ref, xb_ref, acc_ref):
    del te_ref
    i = pl.program_id(0)
    f = pl.program_id(1)
    active = i < na_ref[0]

    @pl.when(active)
    def _():
        @pl.when(f == 0)
        def _():
            xb_ref[...] = x_ref[...].astype(BF16)

        contrib = _swiglu_chunks(xb_ref[...], wg_ref.at[0], wu_ref.at[0], wd_ref.at[0])

        @pl.when(f == 0)
        def _():
            acc_ref[...] = contrib

        @pl.when(f > 0)
        def _():
            acc_ref[...] += contrib

    last = f == pl.num_programs(1) - 1

    @pl.when(last & active)
    def _():
        o_ref[...] = acc_ref[...]

    @pl.when(last & jnp.logical_not(active))
    def _():
        o_ref[...] = jnp.zeros_like(o_ref)


def _ffn_grouped(xs, tile_expert, n_active, w_gate, w_up, w_down, *, rows, fchunk):
    m, d = xs.shape
    n_f = w_gate.shape[2] // fchunk
    tile = lambda i, na: jnp.maximum(jnp.minimum(i, na[0] - 1), 0)
    chunk = lambda i, f, na: jnp.where(i < na[0], f, n_f - 1)
    return pl.pallas_call(
        _ffn_grouped_kernel,
        grid_spec=pltpu.PrefetchScalarGridSpec(
            num_scalar_prefetch=2,
            grid=(m // rows, n_f),
            in_specs=[pl.BlockSpec((rows, d), lambda i, f, te, na: (tile(i, na), 0)),
                      pl.BlockSpec((1, d, fchunk), lambda i, f, te, na: (te[tile(i, na)], 0, chunk(i, f, na))),
                      pl.BlockSpec((1, d, fchunk), lambda i, f, te, na: (te[tile(i, na)], 0, chunk(i, f, na))),
                      pl.BlockSpec((1, fchunk, d), lambda i, f, te, na: (te[tile(i, na)], chunk(i, f, na), 0))],
            out_specs=pl.BlockSpec((rows, d), lambda i, f, te, na: (i, 0)),
            scratch_shapes=[pltpu.VMEM((rows, d), BF16), pltpu.VMEM((rows, d), F32)],
        ),
        out_shape=jax.ShapeDtypeStruct((m, d), F32),
        compiler_params=_params(("arbitrary", "arbitrary")),
        name="ffn_moe",
    )(tile_expert, n_active, xs, w_gate, w_up, w_down)


def _combine_kernel(slot_ref, x_ref, route_ref, ys_ref, g_ref, b_ref, o_ref, y1_ref, y2_ref, sem):
    half = x_ref.shape[0] // 2

    for h in range(2):
        def issue(r, carry, h=h):
            for k, dst in enumerate((y1_ref, y2_ref)):
                s = slot_ref[0, 0, 2 * r + k]
                pltpu.make_async_copy(ys_ref.at[pl.ds(s, 1)], dst.at[pl.ds(r, 1)], sem.at[h]).start()
            return carry

        lax.fori_loop(h * half, (h + 1) * half, issue, 0, unroll=DMA_UNROLL)
    for h in range(2):
        rws = pl.ds(h * half, half)
        for dst in (y1_ref, y2_ref):
            pltpu.make_async_copy(ys_ref.at[pl.ds(0, half)], dst.at[rws], sem.at[h]).wait()
        route = route_ref[rws, :]
        moe = route[:, 4:5] * y1_ref[rws, :] + route[:, 5:6] * y2_ref[rws, :]
        o_ref[rws, :] = _layernorm(DEEPNORM_ALPHA * x_ref[rws, :] + moe, g_ref[...], b_ref[...])


def _combine_ln(x2d, route, slots, ys, g, b, *, rows):
    n, d = x2d.shape
    g2, b2 = g.reshape(1, -1), b.reshape(1, -1)
    return pl.pallas_call(
        _combine_kernel,
        grid=(n // rows,),
        in_specs=[pl.BlockSpec((1, 1, 2 * rows), lambda i: (i, 0, 0), memory_space=pltpu.SMEM),
                  pl.BlockSpec((rows, d), lambda i: (i, 0)),
                  pl.BlockSpec((rows, N_EXPERTS), lambda i: (i, 0)),
                  pl.BlockSpec(memory_space=pl.ANY),
                  pl.BlockSpec((1, d), lambda i: (0, 0)),
                  pl.BlockSpec((1, d), lambda i: (0, 0))],
        out_specs=pl.BlockSpec((rows, d), lambda i: (i, 0)),
        out_shape=jax.ShapeDtypeStruct((n, d), F32),
        scratch_shapes=[pltpu.VMEM((rows, d), F32), pltpu.VMEM((rows, d), F32), pltpu.SemaphoreType.DMA((2,))],
        compiler_params=_params(("arbitrary",)),
        name="moe_combine",
    )(slots, x2d, route, ys, g2, b2)


def _moe(x2d, route, counts, w_gate, w_up, w_down, g, b, *, rows, fchunk):
    n, _ = x2d.shape
    counts = counts.reshape(-1).astype(jnp.int32)
    padded = (counts + rows - 1) // rows * rows
    ends = jnp.cumsum(padded)
    offs = ends - padded
    n_tiles = (2 * n) // rows + N_EXPERTS
    starts = jnp.arange(n_tiles, dtype=jnp.int32) * rows
    tile_expert = jnp.minimum(jnp.sum(ends[None, :] <= starts[:, None], axis=1), N_EXPERTS - 1).astype(jnp.int32)
    n_active = (ends[-1:] // rows).astype(jnp.int32)
    expert = route[:, 0:2].astype(jnp.int32)
    rank = route[:, 2:4].astype(jnp.int32)
    base = jnp.sum(jnp.where(expert[..., None] == jnp.arange(N_EXPERTS), offs, 0), axis=-1)
    slots = (base + rank).reshape(n // rows, 1, 2 * rows)
    xs = _dispatch(x2d, slots, n_tiles * rows, rows=rows)
    ys = _ffn_grouped(xs, tile_expert, n_active, w_gate, w_up, w_down, rows=rows, fchunk=fchunk)
    return _combine_ln(x2d, route, slots, ys, g, b, rows=rows)


def _layer_even(x2d, positions, bsz, seq, w_in, gm_ln_g, gm_ln_b, w_s, b_s, q_norm, w_uq, kv_norm, w_ukv, w_out,
                ln_g, ln_b, f_gate, f_up, f_down, f_ln_g, f_ln_b, *, rows, ffn_rows, attn_blk):
    cos_p, sin_p = _rope_tables(positions)
    ya, q, k, v = _mixer_in(x2d, cos_p, sin_p, w_in, gm_ln_g, gm_ln_b, w_s, b_s, q_norm, w_uq, kv_norm, w_ukv,
                            rows=rows)
    yb = _attention(q, k, v, bsz=bsz, seq=seq, blk=attn_blk)
    wo = w_out.astype(BF16)
    return _mixer_ffn(ya, yb, wo[:A_WIDTH], wo[A_WIDTH:], x2d, ln_g, ln_b, f_gate.astype(BF16), f_up.astype(BF16),
                      f_down.astype(BF16), f_ln_g, f_ln_b, rows=ffn_rows)


def _layer_odd(x2d, bsz, w_in, a_re, a_im, log_step, b_re, b_im, c_re, c_im, d_skip, glu_w, glu_b, w_out, ln_g, ln_b,
               router, m_gate, m_up, m_down, m_ln_g, m_ln_b, *, rows, ffn_rows, fchunk):
    bd, cd, lam = _s5_weights(a_re, a_im, log_step, b_re, b_im, c_re, c_im)
    y2d = _s5(x2d, w_in.astype(BF16), bd, cd, lam, d_skip, bsz=bsz)
    x2d, route, counts = _s5_out(y2d, glu_w, glu_b, w_out, x2d, ln_g, ln_b, router, rows=rows)
    return _moe(x2d, route, counts, m_gate.astype(BF16), m_up.astype(BF16), m_down.astype(BF16), m_ln_g, m_ln_b,
                rows=ffn_rows, fchunk=fchunk)


def kernel(x, positions, ab_w_in, gm_ln_g, gm_ln_b, gm_w_s, gm_b_s, mla_q_norm, mla_w_uq, mla_kv_norm, mla_w_ukv, ab_w_out, ab_ln_g, ab_ln_b, ffd_w_gate, ffd_w_up, ffd_w_down, ffd_ln_g, ffd_ln_b, c_w_in, s5_a_re, s5_a_im, s5_log_step, s5_b_re, s5_b_im, s5_c_re, s5_c_im, s5_d, glu_w, glu_b, c_w_out, c_ln_g, c_ln_b, moe_router, moe_w_gate, moe_w_up, moe_w_down, moe_ln_g, moe_ln_b):
    bsz, seq, d = x.shape
    x2d = x.reshape(bsz * seq, d)
    rows = min(512, seq)
    for i in range(DEPTH):
        j = i // 2
        if i % 2 == 0:
            x2d = _layer_even(x2d, positions, bsz, seq, ab_w_in[j], gm_ln_g[j], gm_ln_b[j], gm_w_s[j], gm_b_s[j],
                              mla_q_norm[j], mla_w_uq[j], mla_kv_norm[j], mla_w_ukv[j], ab_w_out[j], ab_ln_g[j],
                              ab_ln_b[j], ffd_w_gate[j], ffd_w_up[j], ffd_w_down[j], ffd_ln_g[j], ffd_ln_b[j],
                              rows=rows, ffn_rows=rows, attn_blk=min(512, seq))
        else:
            x2d = _layer_odd(x2d, bsz, c_w_in[j], s5_a_re[j], s5_a_im[j], s5_log_step[j], s5_b_re[j],
                             s5_b_im[j], s5_c_re[j], s5_c_im[j], s5_d[j], glu_w[j], glu_b[j], c_w_out[j], c_ln_g[j],
                             c_ln_b[j], moe_router[j], moe_w_gate[j], moe_w_up[j], moe_w_down[j], moe_ln_g[j],
                             moe_ln_b[j], rows=rows, ffn_rows=rows, fchunk=moe_w_gate.shape[3] // 2)
    return x2d.reshape(bsz, seq, d)
```

```python
import functools
import math

import jax
import jax.numpy as jnp
from jax import lax
from jax.experimental import pallas as pl
from jax.experimental.pallas import tpu as pltpu

F32 = jnp.float32
BF16 = jnp.bfloat16

A_GROUPS = 4
A_GROUP_DIM = 128
A_WIDTH = A_GROUPS * A_GROUP_DIM
A_CHUNK = 128
MLA_HEADS = 8
QK_NOPE = 64
QK_ROPE = 32
QK_HEAD = QK_NOPE + QK_ROPE
V_HEAD = 64
Q_LORA = 384
KV_LORA = 256
ROPE_THETA = 10000.0
S5_GROUP_DIM = 16
S5_STATE = 64
N_EXPERTS = 8
LN_EPS = 1e-5
RMS_EPS = 1e-6
DEPTH = 2
DEEPNORM_ALPHA = (2.0 * DEPTH) ** 0.25

LANES = 128
HEAD_PAD = LANES
S5_GROUPS_PER_BLOCK = 2 * LANES // S5_GROUP_DIM
S5_STEPS = 16
S5_SCAN_LANES = 4 * LANES
VMEM_LIMIT = 56 * 1024 * 1024
NEG_BIG = -1e30
ROW_TILE = 512
ATTN_BLOCK = 512
COMBINE_PARTS = 4
FFN_CHUNK = 512
DMA_UNROLL = 16


def _params(sem):
    return pltpu.CompilerParams(dimension_semantics=sem, vmem_limit_bytes=VMEM_LIMIT)


def _gelu(x):
    c = math.sqrt(2.0 / math.pi)
    return 0.5 * x * (1.0 + jnp.tanh(c * (x + 0.044715 * (x * x * x))))


def _sigmoid(x):
    return 1.0 / (1.0 + jnp.exp(-x))


def _layernorm(x, g, b):
    mu = jnp.mean(x, axis=-1, keepdims=True)
    xc = x - mu
    var = jnp.mean(xc * xc, axis=-1, keepdims=True)
    return xc * lax.rsqrt(var + LN_EPS) * g + b


def _rmsnorm(x, g):
    ms = jnp.mean(x * x, axis=-1, keepdims=True)
    return x * lax.rsqrt(ms + RMS_EPS) * g


def _dot(a, b):
    return jnp.dot(a, b, preferred_element_type=F32)


def _rope_table_kernel(inv_ref, pos_ref, cos_ref, sin_ref):
    pos = pos_ref[...].astype(F32)
    for j in range(QK_ROPE // 2):
        ang = pos * inv_ref[j]
        cos_ref[j] = jnp.cos(ang)
        sin_ref[j] = jnp.sin(ang)


def _rope_tables(positions):
    n = positions.size
    half = QK_ROPE // 2
    inv_freq = 1.0 / (ROPE_THETA ** (jnp.arange(0, QK_ROPE, 2, dtype=F32) / QK_ROPE))
    pos2d = positions.reshape(n // LANES, LANES)
    cos_t, sin_t = pl.pallas_call(
        _rope_table_kernel,
        out_shape=(jax.ShapeDtypeStruct((half, n // LANES, LANES), F32),) * 2,
        in_specs=[pl.BlockSpec(memory_space=pltpu.SMEM), pl.BlockSpec(memory_space=pltpu.VMEM)],
        out_specs=(pl.BlockSpec(memory_space=pltpu.VMEM),) * 2,
        name="rope_table",
    )(inv_freq, pos2d)
    cos_c = cos_t.reshape(half, n).T
    sin_c = sin_t.reshape(half, n).T
    ones = jnp.ones((n, QK_NOPE), F32)
    zeros_n = jnp.zeros((n, QK_NOPE), F32)
    zeros_p = jnp.zeros((n, HEAD_PAD - QK_HEAD), F32)
    cos_p = jnp.concatenate([ones, cos_c, cos_c, zeros_p], axis=1)
    sin_p = jnp.concatenate([zeros_n, sin_c, sin_c, zeros_p], axis=1)
    return cos_p, sin_p


def _mixer_in_kernel(x_ref, cos_ref, sin_ref, w_in_ref, lng_ref, lnb_ref, ws_ref, bs_ref, qn_ref, wq_ref,
                     kvn_ref, wkv_ref, ya_ref, q_ref, k_ref, v_ref):
    rows = x_ref.shape[0]
    xb = x_ref[...].astype(BF16)
    h = _dot(xb, w_in_ref[...])
    o_q = 2 * A_WIDTH
    o_kv = o_q + Q_LORA
    o_pe = o_kv + KV_LORA
    o_rot = o_pe + HEAD_PAD
    cos_p = cos_ref[...]
    sin_p = sin_ref[...]

    a_u = _gelu(h[:, :A_WIDTH])
    a_v = _gelu(h[:, A_WIDTH:o_q])
    vn = _layernorm(a_v, lng_ref[...], lnb_ref[...]).astype(BF16)
    t_idx = lax.broadcasted_iota(jnp.int32, (A_CHUNK, A_CHUNK), 0)
    s_idx = lax.broadcasted_iota(jnp.int32, (A_CHUNK, A_CHUNK), 1)
    causal = s_idx <= t_idx
    bs = bs_ref[...]
    for g in range(A_GROUPS):
        w_g = jnp.where(causal, ws_ref[g], 0.0).astype(BF16)
        cols = slice(g * A_GROUP_DIM, (g + 1) * A_GROUP_DIM)
        for c in range(rows // A_CHUNK):
            rws = slice(c * A_CHUNK, (c + 1) * A_CHUNK)
            mixed = _dot(w_g, vn[rws, cols]) + bs[:, cols]
            ya_ref[rws, cols] = (a_u[rws, cols] * mixed).astype(BF16)

    cqn = _rmsnorm(h[:, o_q:o_kv], qn_ref[...]).astype(BF16)
    q2 = _dot(cqn, wq_ref[...])
    half = MLA_HEADS * HEAD_PAD
    for hd in range(MLA_HEADS):
        cols = slice(hd * HEAD_PAD, (hd + 1) * HEAD_PAD)
        rot = slice(half + hd * HEAD_PAD, half + (hd + 1) * HEAD_PAD)
        q_ref[:, cols] = (q2[:, cols] * cos_p + q2[:, rot] * sin_p).astype(BF16)

    ckvn = _rmsnorm(h[:, o_kv:o_pe], kvn_ref[...]).astype(BF16)
    kv = _dot(ckvn, wkv_ref[...])
    kpe = h[:, o_pe:o_rot] * cos_p + h[:, o_rot:o_rot + HEAD_PAD] * sin_p
    for hd in range(MLA_HEADS):
        cols = slice(hd * HEAD_PAD, (hd + 1) * HEAD_PAD)
        k_ref[:, cols] = (kv[:, cols] + kpe).astype(BF16)
    v_ref[...] = kv[:, half:].astype(BF16)


def _mixer_in(x2d, cos_p, sin_p, w_in, gm_ln_g, gm_ln_b, w_s, b_s, q_norm, w_uq, kv_norm, w_ukv, *, rows):
    n, d = x2d.shape
    hp = MLA_HEADS * HEAD_PAD
    o_pe = 2 * A_WIDTH + Q_LORA + KV_LORA
    half = QK_ROPE // 2
    w_pe = w_in[:, o_pe:o_pe + QK_ROPE]
    w_pe_rot = jnp.concatenate([-w_pe[:, half:], w_pe[:, :half]], axis=1)
    pad_l = jnp.zeros((d, QK_NOPE), F32)
    pad_r = jnp.zeros((d, HEAD_PAD - QK_HEAD), F32)
    w_in_p = jnp.concatenate([w_in[:, :o_pe], pad_l, w_pe, pad_r, pad_l, w_pe_rot, pad_r], axis=1).astype(BF16)
    wq = w_uq.reshape(Q_LORA, MLA_HEADS, QK_HEAD)
    wq_pe = wq[:, :, QK_NOPE:]
    wq_rot = jnp.concatenate([jnp.zeros((Q_LORA, MLA_HEADS, QK_NOPE), F32), -wq_pe[:, :, half:], wq_pe[:, :, :half]],
                             axis=2)
    padq = ((0, 0), (0, 0), (0, HEAD_PAD - QK_HEAD))
    wq2 = jnp.concatenate([jnp.pad(wq, padq).reshape(Q_LORA, hp), jnp.pad(wq_rot, padq).reshape(Q_LORA, hp)],
                          axis=1).astype(BF16)
    wkv = w_ukv.reshape(KV_LORA, MLA_HEADS, QK_NOPE + V_HEAD)
    wk = jnp.pad(wkv[:, :, :QK_NOPE], ((0, 0), (0, 0), (0, HEAD_PAD - QK_NOPE))).reshape(KV_LORA, hp)
    wv = wkv[:, :, QK_NOPE:].reshape(KV_LORA, MLA_HEADS * V_HEAD)
    wkv2 = jnp.concatenate([wk, wv], axis=1).astype(BF16)
    bs_full = jnp.repeat(b_s.T, A_GROUP_DIM, axis=1)

    full = lambda a: pl.BlockSpec(a.shape, lambda i: (0,) * a.ndim)
    row = lambda w: pl.BlockSpec((rows, w), lambda i: (i, 0))
    args = (x2d, cos_p, sin_p, w_in_p, gm_ln_g.reshape(1, -1), gm_ln_b.reshape(1, -1), w_s, bs_full,
            q_norm.reshape(1, -1), wq2, kv_norm.reshape(1, -1), wkv2)
    in_specs = [row(d), row(HEAD_PAD), row(HEAD_PAD)] + [full(a) for a in args[3:]]
    return pl.pallas_call(
        _mixer_in_kernel,
        grid=(n // rows,),
        in_specs=in_specs,
        out_specs=(row(A_WIDTH), row(hp), row(hp), row(MLA_HEADS * V_HEAD)),
        out_shape=(jax.ShapeDtypeStruct((n, A_WIDTH), BF16), jax.ShapeDtypeStruct((n, hp), BF16),
                   jax.ShapeDtypeStruct((n, hp), BF16), jax.ShapeDtypeStruct((n, MLA_HEADS * V_HEAD), BF16)),
        compiler_params=_params(("parallel",)),
        name="mixer_in",
    )(*args)


def _attn_kernel(q_ref, k_ref, v_ref, o_ref, *, blk):
    seq = q_ref.shape[0]
    scale = QK_HEAD ** -0.5
    row = lax.broadcasted_iota(jnp.int32, (blk, blk), 0)
    col = lax.broadcasted_iota(jnp.int32, (blk, blk), 1)
    diag_mask = col <= row
    first_head_lanes = lax.broadcasted_iota(jnp.int32, (blk, 2 * V_HEAD), 1) < V_HEAD

    qk = lambda a, b: lax.dot_general(a, b, (((1,), (1,)), ((), ())), preferred_element_type=F32)

    for j in range(seq // blk):
        q0 = j * blk
        outs = []
        for hh in range(2):
            cols = slice(hh * HEAD_PAD, (hh + 1) * HEAD_PAD)
            q = q_ref[q0:q0 + blk, cols]
            s_d = jnp.where(diag_mask, qk(q, k_ref[q0:q0 + blk, cols]), NEG_BIG)
            m = jnp.max(s_d, axis=-1, keepdims=True)
            if j > 0:
                s_o = qk(q, k_ref[0:q0, cols])
                m = jnp.maximum(m, jnp.max(s_o, axis=-1, keepdims=True))
            p_d = jnp.exp((s_d - m) * scale)
            l = jnp.sum(p_d, axis=-1, keepdims=True)
            acc = _dot(p_d.astype(BF16), v_ref[q0:q0 + blk, :])
            if j > 0:
                p_o = jnp.exp((s_o - m) * scale)
                l = l + jnp.sum(p_o, axis=-1, keepdims=True)
                acc = acc + _dot(p_o.astype(BF16), v_ref[0:q0, :])
            outs.append(acc / l)
        o_ref[q0:q0 + blk, :] = jnp.where(first_head_lanes, outs[0], outs[1]).astype(BF16)


def _attention(q, k, v, *, bsz, seq, blk):
    n = bsz * seq
    pair = 2 * HEAD_PAD
    return pl.pallas_call(
        functools.partial(_attn_kernel, blk=blk),
        grid=(bsz, MLA_HEADS // 2),
        in_specs=[pl.BlockSpec((seq, pair), lambda b, h: (b, h)), pl.BlockSpec((seq, pair), lambda b, h: (b, h)),
                  pl.BlockSpec((seq, 2 * V_HEAD), lambda b, h: (b, h))],
        out_specs=pl.BlockSpec((seq, 2 * V_HEAD), lambda b, h: (b, h)),
        out_shape=jax.ShapeDtypeStruct((n, MLA_HEADS * V_HEAD), BF16),
        compiler_params=_params(("parallel", "parallel")),
        name="attention",
    )(q, k, v)


def _swiglu_chunks(xb, wg_ref, wu_ref, wd_ref):
    dff = wg_ref.shape[1]
    acc = None
    for c0 in range(0, dff, FFN_CHUNK):
        cols = slice(c0, min(c0 + FFN_CHUNK, dff))
        gate = _dot(xb, wg_ref[:, cols])
        up = _dot(xb, wu_ref[:, cols])
        hid = (gate * _sigmoid(gate) * up).astype(BF16)
        part = _dot(hid, wd_ref[cols, :])
        acc = part if acc is None else acc + part
    return acc


def _mixer_ffn_kernel(ya_ref, yb_ref, woa_ref, wob_ref, x_ref, g1_ref, b1_ref, wg_ref, wu_ref, wd_ref, g2_ref, b2_ref,
                      o_ref):
    mix = _dot(ya_ref[...], woa_ref[...]) + _dot(yb_ref[...], wob_ref[...])
    x1 = _layernorm(DEEPNORM_ALPHA * x_ref[...] + mix, g1_ref[...], b1_ref[...])
    ffn = _swiglu_chunks(x1.astype(BF16), wg_ref, wu_ref, wd_ref)
    o_ref[...] = _layernorm(DEEPNORM_ALPHA * x1 + ffn, g2_ref[...], b2_ref[...])


def _mixer_ffn(ya, yb, wo_a, wo_b, x2d, g1, b1, w_gate, w_up, w_down, g2, b2, *, rows):
    n, d = x2d.shape
    once = lambda a: pl.BlockSpec(a.shape, lambda i: (0,) * a.ndim, pipeline_mode=pl.Buffered(1))
    row = lambda w: pl.BlockSpec((rows, w), lambda i: (i, 0))
    vec = lambda v: v.reshape(1, -1)
    args = (ya, yb, wo_a, wo_b, x2d, vec(g1), vec(b1), w_gate, w_up, w_down, vec(g2), vec(b2))
    in_specs = [row(ya.shape[1]), row(yb.shape[1]), once(wo_a), once(wo_b), row(d)] + [once(a) for a in args[5:]]
    return pl.pallas_call(
        _mixer_ffn_kernel,
        grid=(n // rows,),
        in_specs=in_specs,
        out_specs=row(d),
        out_shape=jax.ShapeDtypeStruct((n, d), F32),
        compiler_params=_params(("parallel",)),
        name="mixer_ffn",
    )(*args)


def _s5_kernel(x_ref, perm_ref, win_ref, bd_ref, cd_ref, lam_ref, d_ref, y_hbm, bur_ref, bui_ref, hr_ref, hi_ref,
               st_ref, ys_ref, sem, *, bsz):
    i = pl.program_id(0)
    n_steps = pl.num_programs(0)
    slot = i % 2
    steps = x_ref.shape[1]
    rows = bsz * steps

    def out_copies(at_step, at_slot):
        return [pltpu.make_async_copy(ys_ref.at[at_slot, pl.ds(t * bsz, bsz)], y_hbm.at[:, at_step * steps + t, :],
                                      sem.at[at_slot]) for t in range(steps)]

    @pl.when(i == 0)
    def _():
        st_ref[...] = jnp.zeros_like(st_ref)

    @pl.when(i >= 2)
    def _():
        for cp in out_copies(i - 2, slot):
            cp.wait()

    n_blk, cb, sb2 = bd_ref.shape
    sb = sb2 // 2
    xb = x_ref[...].reshape(rows, x_ref.shape[2]).astype(BF16)
    xb = _dot(perm_ref[...], xb).astype(BF16)
    u = _dot(xb, win_ref[...])
    ub = u.astype(BF16)
    for q in range(n_blk):
        bu = _dot(ub[:, q * cb:(q + 1) * cb], bd_ref[q])
        bur_ref[:, q * sb:(q + 1) * sb] = bu[:, :sb]
        bui_ref[:, q * sb:(q + 1) * sb] = bu[:, sb:]
    for c in range(bur_ref.shape[1] // S5_SCAN_LANES):
        lanes = slice(c * S5_SCAN_LANES, (c + 1) * S5_SCAN_LANES)
        a_re = lam_ref[0:1, lanes]
        a_im = lam_ref[1:2, lanes]
        h_re = st_ref[0, :, lanes]
        h_im = st_ref[1, :, lanes]
        for t in range(steps):
            rws = slice(t * bsz, (t + 1) * bsz)
            n_re = a_re * h_re - a_im * h_im + bur_ref[rws, lanes]
            n_im = a_re * h_im + a_im * h_re + bui_ref[rws, lanes]
            hr_ref[rws, lanes] = n_re.astype(BF16)
            hi_ref[rws, lanes] = n_im.astype(BF16)
            h_re, h_im = n_re, n_im
        st_ref[0, :, lanes] = h_re
        st_ref[1, :, lanes] = h_im
    for q in range(n_blk):
        st = slice(q * sb, (q + 1) * sb)
        ch = slice(q * cb, (q + 1) * cb)
        y = _dot(hr_ref[:, st], cd_ref[q, :sb]) + _dot(hi_ref[:, st], cd_ref[q, sb:])
        ys_ref[slot, :, ch] = y + d_ref[:, ch] * u[:, ch]
    for cp in out_copies(i, slot):
        cp.start()

    @pl.when(i == n_steps - 1)
    def _():
        for cp in out_copies(i, slot):
            cp.wait()

    @pl.when((i == n_steps - 1) & (i >= 1))
    def _():
        for cp in out_copies(i - 1, 1 - slot):
            cp.wait()


def _s5(x2d, w_in, bd, cd, lam, d_skip, *, bsz):
    n, d = x2d.shape
    seq = n // bsz
    rows = S5_STEPS * bsz
    n_state = lam.shape[1]
    full = lambda a: pl.BlockSpec(a.shape, lambda i: (0,) * a.ndim)
    d2 = d_skip.reshape(1, -1)
    r_out = jnp.arange(rows)[:, None]
    r_in = jnp.arange(rows)[None, :]
    perm = (r_in == (r_out % bsz) * S5_STEPS + r_out // bsz).astype(BF16)
    y = pl.pallas_call(
        functools.partial(_s5_kernel, bsz=bsz),
        grid=(seq // S5_STEPS,),
        in_specs=[pl.BlockSpec((bsz, S5_STEPS, d), lambda i: (0, i, 0)), full(perm), full(w_in), full(bd), full(cd),
                  full(lam), full(d2)],
        out_specs=pl.BlockSpec(memory_space=pl.ANY),
        out_shape=jax.ShapeDtypeStruct((bsz, seq, d), F32),
        scratch_shapes=[pltpu.VMEM((rows, n_state), F32), pltpu.VMEM((rows, n_state), F32),
                        pltpu.VMEM((rows, n_state), BF16), pltpu.VMEM((rows, n_state), BF16),
                        pltpu.VMEM((2, bsz, n_state), F32), pltpu.VMEM((2, rows, d), F32),
                        pltpu.SemaphoreType.DMA((2,))],
        compiler_params=_params(("arbitrary",)),
        name="s5",
    )(x2d.reshape(bsz, seq, d), perm, w_in, bd, cd, lam, d2)
    return y.reshape(n, d)


def _s5_weights(a_re, a_im, log_step, b_re, b_im, c_re, c_im):
    n_grp, n_st = a_re.shape
    gpb = S5_GROUPS_PER_BLOCK
    n_blk = n_grp // gpb
    delta = jnp.exp(log_step)[:, None]
    mag = jnp.exp(delta * a_re)
    abar_re = mag * jnp.cos(delta * a_im)
    abar_im = mag * jnp.sin(delta * a_im)
    den = a_re * a_re + a_im * a_im
    coef_re = ((abar_re - 1.0) * a_re + abar_im * a_im) / den
    coef_im = (abar_im * a_re - (abar_re - 1.0) * a_im) / den
    bb_re = coef_re[..., None] * b_re - coef_im[..., None] * b_im
    bb_im = coef_re[..., None] * b_im + coef_im[..., None] * b_re
    same_group = jnp.eye(gpb, dtype=F32)[None, :, None, :, None]

    def expand(w):
        w = jnp.transpose(w.reshape(n_blk, gpb, w.shape[1], w.shape[2]), (0, 1, 3, 2))
        w = w[:, :, :, None, :] * same_group
        return w.reshape(n_blk, gpb * w.shape[2], gpb * w.shape[4])

    bd = jnp.concatenate([expand(bb_re), expand(bb_im)], axis=2).astype(BF16)
    cd = jnp.concatenate([expand(c_re), -expand(c_im)], axis=1).astype(BF16)
    lam = jnp.stack([abar_re.reshape(-1), abar_im.reshape(-1)])
    return bd, cd, lam


def _s5_out_kernel(y_ref, gw_ref, gb_ref, wo_ref, x_ref, g_ref, b_ref, r_ref, o_ref, route_ref, count_ref,
                   carry_ref):
    @pl.when(pl.program_id(0) == 0)
    def _():
        carry_ref[...] = jnp.zeros_like(carry_ref)

    rows = x_ref.shape[0]
    gl = _gelu(y_ref[...])
    z = gl * _sigmoid(_dot(gl.astype(BF16), gw_ref[...]) + gb_ref[...])
    mix = _dot(z.astype(BF16), wo_ref[...])
    xo = _layernorm(DEEPNORM_ALPHA * x_ref[...] + mix, g_ref[...], b_ref[...])
    o_ref[...] = xo
    x_hi = xo.astype(BF16)
    x_lo = (xo - x_hi.astype(F32)).astype(BF16)
    r = r_ref[...]
    r_hi = r.astype(BF16)
    r_lo = (r - r_hi.astype(F32)).astype(BF16)
    logits = _dot(x_hi, r_hi) + (_dot(x_lo, r_hi) + _dot(x_hi, r_lo))
    idx = lax.broadcasted_iota(jnp.int32, logits.shape, 1)
    m1 = jnp.max(logits, axis=-1, keepdims=True)
    i1 = jnp.min(jnp.where(logits == m1, idx, N_EXPERTS), axis=-1, keepdims=True)
    rest = jnp.where(idx == i1, -jnp.inf, logits)
    m2 = jnp.max(rest, axis=-1, keepdims=True)
    i2 = jnp.min(jnp.where(rest == m2, idx, N_EXPERTS), axis=-1, keepdims=True)
    e2 = jnp.exp(m2 - m1)
    g1 = 1.0 / (1.0 + e2)
    g2 = e2 * g1
    sel = jnp.where((idx == i1) | (idx == i2), 1.0, 0.0)
    t_r = lax.broadcasted_iota(jnp.int32, (rows, rows), 0)
    t_c = lax.broadcasted_iota(jnp.int32, (rows, rows), 1)
    earlier = jnp.where(t_c < t_r, 1.0, 0.0).astype(BF16)
    before = _dot(earlier, sel.astype(BF16)) + carry_ref[...]
    r1 = jnp.sum(jnp.where(idx == i1, before, 0.0), axis=-1, keepdims=True)
    r2 = jnp.sum(jnp.where(idx == i2, before, 0.0), axis=-1, keepdims=True)
    total = carry_ref[...] + jnp.sum(sel, axis=0, keepdims=True)
    carry_ref[...] = total
    count_ref[...] = total
    fields = (i1.astype(F32), i2.astype(F32), r1, r2, g1, g2)
    route = jnp.zeros(logits.shape, F32)
    for lane, val in enumerate(fields):
        route = jnp.where(idx == lane, val, route)
    route_ref[...] = route


def _s5_out(y2d, glu_w, glu_b, w_out, x2d, g, b, router, *, rows):
    n, d = x2d.shape
    full = lambda a: pl.BlockSpec(a.shape, lambda i: (0,) * a.ndim)
    row = lambda w: pl.BlockSpec((rows, w), lambda i: (i, 0))
    args = (y2d, glu_w.astype(BF16), glu_b.reshape(1, -1), w_out.astype(BF16), x2d, g.reshape(1, -1),
            b.reshape(1, -1), router)
    in_specs = [row(d), full(args[1]), full(args[2]), full(args[3]), row(d), full(args[5]), full(args[6]),
                full(args[7])]
    return pl.pallas_call(
        _s5_out_kernel,
        grid=(n // rows,),
        in_specs=in_specs,
        out_specs=(row(d), row(N_EXPERTS), pl.BlockSpec((1, N_EXPERTS), lambda i: (0, 0))),
        out_shape=(jax.ShapeDtypeStruct((n, d), F32), jax.ShapeDtypeStruct((n, N_EXPERTS), F32),
                   jax.ShapeDtypeStruct((1, N_EXPERTS), F32)),
        scratch_shapes=[pltpu.VMEM((1, N_EXPERTS), F32)],
        compiler_params=_params(("arbitrary",)),
        name="s5_out",
    )(*args)


def _dispatch_kernel(slot_ref, x_ref, xs_in_ref, xs_ref, sem):
    del xs_in_ref
    rows = x_ref.shape[0]

    def issue(r, carry):
        for k in range(2):
            s = slot_ref[0, 0, 2 * r + k]
            pltpu.make_async_copy(x_ref.at[pl.ds(r, 1)], xs_ref.at[pl.ds(s, 1)], sem).start()
        return carry

    lax.fori_loop(0, rows, issue, 0, unroll=DMA_UNROLL)
    for k in range(2):
        pltpu.make_async_copy(x_ref, xs_ref.at[pl.ds(0, rows)], sem).wait()


def _dispatch(x2d, slots, n_slots, *, rows):
    n, d = x2d.shape
    return pl.pallas_call(
        _dispatch_kernel,
        grid=(n // rows,),
        in_specs=[pl.BlockSpec((1, 1, 2 * rows), lambda i: (i, 0, 0), memory_space=pltpu.SMEM),
                  pl.BlockSpec((rows, d), lambda i: (i, 0)),
                  pl.BlockSpec(memory_space=pl.ANY)],
        out_specs=pl.BlockSpec(memory_space=pl.ANY),
        out_shape=jax.ShapeDtypeStruct((n_slots, d), F32),
        scratch_shapes=[pltpu.SemaphoreType.DMA],
        input_output_aliases={2: 0},
        compiler_params=_params(("arbitrary",)),
        name="moe_dispatch",
    )(slots, x2d, jnp.zeros((n_slots, d), F32))


def _ffn_grouped_kernel(te_ref, na_ref, x_ref, wg_ref, wu_ref, wd_ref, o_ref, xb_ref, acc_ref):
    del te_ref
    i = pl.program_id(0)
    f = pl.program_id(1)
    active = i < na_ref[0]

    @pl.when(active)
    def _():
        @pl.when(f == 0)
        def _():
            xb_ref[...] = x_ref[...].astype(BF16)

        contrib = _swiglu_chunks(xb_ref[...], wg_ref.at[0], wu_ref.at[0], wd_ref.at[0])

        @pl.when(f == 0)
        def _():
            acc_ref[...] = contrib

        @pl.when(f > 0)
        def _():
            acc_ref[...] += contrib

    last = f == pl.num_programs(1) - 1

    @pl.when(last & active)
    def _():
        o_ref[...] = acc_ref[...]

    @pl.when(last & jnp.logical_not(active))
    def _():
        o_ref[...] = jnp.zeros_like(o_ref)


def _ffn_grouped(xs, tile_expert, n_active, w_gate, w_up, w_down, *, rows, fchunk):
    m, d = xs.shape
    n_f = w_gate.shape[2] // fchunk
    tile = lambda i, na: jnp.maximum(jnp.minimum(i, na[0] - 1), 0)
    chunk = lambda i, f, na: jnp.where(i < na[0], f, n_f - 1)
    return pl.pallas_call(
        _ffn_grouped_kernel,
        grid_spec=pltpu.PrefetchScalarGridSpec(
            num_scalar_prefetch=2,
            grid=(m // rows, n_f),
            in_specs=[pl.BlockSpec((rows, d), lambda i, f, te, na: (tile(i, na), 0)),
                      pl.BlockSpec((1, d, fchunk), lambda i, f, te, na: (te[tile(i, na)], 0, chunk(i, f, na))),
                      pl.BlockSpec((1, d, fchunk), lambda i, f, te, na: (te[tile(i, na)], 0, chunk(i, f, na))),
                      pl.BlockSpec((1, fchunk, d), lambda i, f, te, na: (te[tile(i, na)], chunk(i, f, na), 0))],
            out_specs=pl.BlockSpec((rows, d), lambda i, f, te, na: (i, 0)),
            scratch_shapes=[pltpu.VMEM((rows, d), BF16), pltpu.VMEM((rows, d), F32)],
        ),
        out_shape=jax.ShapeDtypeStruct((m, d), F32),
        compiler_params=_params(("arbitrary", "arbitrary")),
        name="ffn_moe",
    )(tile_expert, n_active, xs, w_gate, w_up, w_down)


def _combine_kernel(slot_ref, x_ref, route_ref, ys_ref, g_ref, b_ref, o_ref, y1_ref, y2_ref, sem):
    part = x_ref.shape[0] // COMBINE_PARTS

    for h in range(COMBINE_PARTS):
        def issue(r, carry, h=h):
            for k, dst in enumerate((y1_ref, y2_ref)):
                s = slot_ref[0, 0, 2 * r + k]
                pltpu.make_async_copy(ys_ref.at[pl.ds(s, 1)], dst.at[pl.ds(r, 1)], sem.at[h]).start()
            return carry

        lax.fori_loop(h * part, (h + 1) * part, issue, 0, unroll=DMA_UNROLL)
    for h in range(COMBINE_PARTS):
        rws = pl.ds(h * part, part)
        for dst in (y1_ref, y2_ref):
            pltpu.make_async_copy(ys_ref.at[pl.ds(0, part)], dst.at[rws], sem.at[h]).wait()
        route = route_ref[rws, :]
        moe = route[:, 4:5] * y1_ref[rws, :] + route[:, 5:6] * y2_ref[rws, :]
        o_ref[rws, :] = _layernorm(DEEPNORM_ALPHA * x_ref[rws, :] + moe, g_ref[...], b_ref[...])


def _combine_ln(x2d, route, slots, ys, g, b, *, rows):
    n, d = x2d.shape
    g2, b2 = g.reshape(1, -1), b.reshape(1, -1)
    return pl.pallas_call(
        _combine_kernel,
        grid=(n // rows,),
        in_specs=[pl.BlockSpec((1, 1, 2 * rows), lambda i: (i, 0, 0), memory_space=pltpu.SMEM),
                  pl.BlockSpec((rows, d), lambda i: (i, 0)),
                  pl.BlockSpec((rows, N_EXPERTS), lambda i: (i, 0)),
                  pl.BlockSpec(memory_space=pl.ANY),
                  pl.BlockSpec((1, d), lambda i: (0, 0)),
                  pl.BlockSpec((1, d), lambda i: (0, 0))],
        out_specs=pl.BlockSpec((rows, d), lambda i: (i, 0)),
        out_shape=jax.ShapeDtypeStruct((n, d), F32),
        scratch_shapes=[pltpu.VMEM((rows, d), F32), pltpu.VMEM((rows, d), F32),
                        pltpu.SemaphoreType.DMA((COMBINE_PARTS,))],
        compiler_params=_params(("arbitrary",)),
        name="moe_combine",
    )(slots, x2d, route, ys, g2, b2)


def _moe(x2d, route, counts, w_gate, w_up, w_down, g, b, *, rows, fchunk):
    n, _ = x2d.shape
    counts = counts.reshape(-1).astype(jnp.int32)
    padded = (counts + rows - 1) // rows * rows
    ends = jnp.cumsum(padded)
    offs = ends - padded
    n_tiles = (2 * n) // rows + N_EXPERTS
    starts = jnp.arange(n_tiles, dtype=jnp.int32) * rows
    tile_expert = jnp.minimum(jnp.sum(ends[None, :] <= starts[:, None], axis=1), N_EXPERTS - 1).astype(jnp.int32)
    n_active = (ends[-1:] // rows).astype(jnp.int32)
    expert = route[:, 0:2].astype(jnp.int32)
    rank = route[:, 2:4].astype(jnp.int32)
    base = jnp.sum(jnp.where(expert[..., None] == jnp.arange(N_EXPERTS), offs, 0), axis=-1)
    slots = (base + rank).reshape(n // rows, 1, 2 * rows)
    xs = _dispatch(x2d, slots, n_tiles * rows, rows=rows)
    ys = _ffn_grouped(xs, tile_expert, n_active, w_gate, w_up, w_down, rows=rows, fchunk=fchunk)
    return _combine_ln(x2d, route, slots, ys, g, b, rows=rows)


def _layer_even(x2d, positions, bsz, seq, w_in, gm_ln_g, gm_ln_b, w_s, b_s, q_norm, w_uq, kv_norm, w_ukv, w_out,
                ln_g, ln_b, f_gate, f_up, f_down, f_ln_g, f_ln_b, *, rows, ffn_rows, attn_blk):
    cos_p, sin_p = _rope_tables(positions)
    ya, q, k, v = _mixer_in(x2d, cos_p, sin_p, w_in, gm_ln_g, gm_ln_b, w_s, b_s, q_norm, w_uq, kv_norm, w_ukv,
                            rows=rows)
    yb = _attention(q, k, v, bsz=bsz, seq=seq, blk=attn_blk)
    wo = w_out.astype(BF16)
    return _mixer_ffn(ya, yb, wo[:A_WIDTH], wo[A_WIDTH:], x2d, ln_g, ln_b, f_gate.astype(BF16), f_up.astype(BF16),
                      f_down.astype(BF16), f_ln_g, f_ln_b, rows=ffn_rows)


def _layer_odd(x2d, bsz, w_in, a_re, a_im, log_step, b_re, b_im, c_re, c_im, d_skip, glu_w, glu_b, w_out, ln_g, ln_b,
               router, m_gate, m_up, m_down, m_ln_g, m_ln_b, *, rows, ffn_rows, fchunk):
    bd, cd, lam = _s5_weights(a_re, a_im, log_step, b_re, b_im, c_re, c_im)
    y2d = _s5(x2d, w_in.astype(BF16), bd, cd, lam, d_skip, bsz=bsz)
    x2d, route, counts = _s5_out(y2d, glu_w, glu_b, w_out, x2d, ln_g, ln_b, router, rows=rows)
    return _moe(x2d, route, counts, m_gate.astype(BF16), m_up.astype(BF16), m_down.astype(BF16), m_ln_g, m_ln_b,
                rows=ffn_rows, fchunk=fchunk)


def kernel(x, positions, ab_w_in, gm_ln_g, gm_ln_b, gm_w_s, gm_b_s, mla_q_norm, mla_w_uq, mla_kv_norm, mla_w_ukv, ab_w_out, ab_ln_g, ab_ln_b, ffd_w_gate, ffd_w_up, ffd_w_down, ffd_ln_g, ffd_ln_b, c_w_in, s5_a_re, s5_a_im, s5_log_step, s5_b_re, s5_b_im, s5_c_re, s5_c_im, s5_d, glu_w, glu_b, c_w_out, c_ln_g, c_ln_b, moe_router, moe_w_gate, moe_w_up, moe_w_down, moe_ln_g, moe_ln_b):
    bsz, seq, d = x.shape
    x2d = x.reshape(bsz * seq, d)
    rows = min(ROW_TILE, seq)
    for i in range(DEPTH):
        j = i // 2
        if i % 2 == 0:
            x2d = _layer_even(x2d, positions, bsz, seq, ab_w_in[j], gm_ln_g[j], gm_ln_b[j], gm_w_s[j], gm_b_s[j],
                              mla_q_norm[j], mla_w_uq[j], mla_kv_norm[j], mla_w_ukv[j], ab_w_out[j], ab_ln_g[j],
                              ab_ln_b[j], ffd_w_gate[j], ffd_w_up[j], ffd_w_down[j], ffd_ln_g[j], ffd_ln_b[j],
                              rows=rows, ffn_rows=rows, attn_blk=min(ATTN_BLOCK, seq))
        else:
            x2d = _layer_odd(x2d, bsz, c_w_in[j], s5_a_re[j], s5_a_im[j], s5_log_step[j], s5_b_re[j],
                             s5_b_im[j], s5_c_re[j], s5_c_im[j], s5_d[j], glu_w[j], glu_b[j], c_w_out[j], c_ln_g[j],
                             c_ln_b[j], moe_router[j], moe_w_gate[j], moe_w_up[j], moe_w_down[j], moe_ln_g[j],
                             moe_ln_b[j], rows=rows, ffn_rows=rows, fchunk=moe_w_gate.shape[3] // 2)
    return x2d.reshape(bsz, seq, d)
```

```python
import functools
import math

import jax
import jax.numpy as jnp
from jax import lax
from jax.experimental import pallas as pl
from jax.experimental.pallas import tpu as pltpu

F32 = jnp.float32
BF16 = jnp.bfloat16

A_GROUPS = 4
A_GROUP_DIM = 128
A_WIDTH = A_GROUPS * A_GROUP_DIM
A_CHUNK = 128
MLA_HEADS = 8
QK_NOPE = 64
QK_ROPE = 32
QK_HEAD = QK_NOPE + QK_ROPE
V_HEAD = 64
Q_LORA = 384
KV_LORA = 256
ROPE_THETA = 10000.0
S5_GROUP_DIM = 16
S5_STATE = 64
N_EXPERTS = 8
LN_EPS = 1e-5
RMS_EPS = 1e-6
DEPTH = 2
DEEPNORM_ALPHA = (2.0 * DEPTH) ** 0.25

LANES = 128
HEAD_PAD = LANES
S5_GROUPS_PER_BLOCK = 2 * LANES // S5_GROUP_DIM
S5_STEPS = 16
S5_SCAN_LANES = 4 * LANES
VMEM_LIMIT = 56 * 1024 * 1024
NEG_BIG = -1e30
ROW_TILE = 512
ATTN_BLOCK = 512
COMBINE_PARTS = 4
FFN_CHUNK = 512
DMA_UNROLL = 16


def _params(sem):
    return pltpu.CompilerParams(dimension_semantics=sem, vmem_limit_bytes=VMEM_LIMIT)


def _gelu(x):
    c = math.sqrt(2.0 / math.pi)
    return 0.5 * x * (1.0 + jnp.tanh(c * (x + 0.044715 * (x * x * x))))


def _sigmoid(x):
    return 1.0 / (1.0 + jnp.exp(-x))


def _layernorm(x, g, b):
    mu = jnp.mean(x, axis=-1, keepdims=True)
    xc = x - mu
    var = jnp.mean(xc * xc, axis=-1, keepdims=True)
    return xc * lax.rsqrt(var + LN_EPS) * g + b


def _rmsnorm(x, g):
    ms = jnp.mean(x * x, axis=-1, keepdims=True)
    return x * lax.rsqrt(ms + RMS_EPS) * g


def _dot(a, b):
    return jnp.dot(a, b, preferred_element_type=F32)


def _rope_table_kernel(inv_ref, pos_ref, cos_ref, sin_ref):
    pos = pos_ref[...].astype(F32)
    for j in range(QK_ROPE // 2):
        ang = pos * inv_ref[j]
        cos_ref[j] = jnp.cos(ang)
        sin_ref[j] = jnp.sin(ang)


def _rope_tables(positions):
    n = positions.size
    half = QK_ROPE // 2
    inv_freq = 1.0 / (ROPE_THETA ** (jnp.arange(0, QK_ROPE, 2, dtype=F32) / QK_ROPE))
    pos2d = positions.reshape(n // LANES, LANES)
    cos_t, sin_t = pl.pallas_call(
        _rope_table_kernel,
        out_shape=(jax.ShapeDtypeStruct((half, n // LANES, LANES), F32),) * 2,
        in_specs=[pl.BlockSpec(memory_space=pltpu.SMEM), pl.BlockSpec(memory_space=pltpu.VMEM)],
        out_specs=(pl.BlockSpec(memory_space=pltpu.VMEM),) * 2,
        name="rope_table",
    )(inv_freq, pos2d)
    cos_c = cos_t.reshape(half, n).T
    sin_c = sin_t.reshape(half, n).T
    ones = jnp.ones((n, QK_NOPE), F32)
    zeros_n = jnp.zeros((n, QK_NOPE), F32)
    zeros_p = jnp.zeros((n, HEAD_PAD - QK_HEAD), F32)
    cos_p = jnp.concatenate([ones, cos_c, cos_c, zeros_p], axis=1)
    sin_p = jnp.concatenate([zeros_n, sin_c, sin_c, zeros_p], axis=1)
    return cos_p, sin_p


def _mixer_in_kernel(x_ref, cos_ref, sin_ref, w_in_ref, lng_ref, lnb_ref, ws_ref, bs_ref, qn_ref, wq_ref,
                     kvn_ref, wkv_ref, ya_ref, q_ref, k_ref, v_ref):
    rows = x_ref.shape[0]
    xb = x_ref[...].astype(BF16)
    h = _dot(xb, w_in_ref[...])
    o_q = 2 * A_WIDTH
    o_kv = o_q + Q_LORA
    o_pe = o_kv + KV_LORA
    o_rot = o_pe + HEAD_PAD
    cos_p = cos_ref[...]
    sin_p = sin_ref[...]

    a_u = _gelu(h[:, :A_WIDTH])
    a_v = _gelu(h[:, A_WIDTH:o_q])
    vn = _layernorm(a_v, lng_ref[...], lnb_ref[...]).astype(BF16)
    t_idx = lax.broadcasted_iota(jnp.int32, (A_CHUNK, A_CHUNK), 0)
    s_idx = lax.broadcasted_iota(jnp.int32, (A_CHUNK, A_CHUNK), 1)
    causal = s_idx <= t_idx
    bs = bs_ref[...]
    for g in range(A_GROUPS):
        w_g = jnp.where(causal, ws_ref[g], 0.0).astype(BF16)
        cols = slice(g * A_GROUP_DIM, (g + 1) * A_GROUP_DIM)
        for c in range(rows // A_CHUNK):
            rws = slice(c * A_CHUNK, (c + 1) * A_CHUNK)
            mixed = _dot(w_g, vn[rws, cols]) + bs[:, cols]
            ya_ref[rws, cols] = (a_u[rws, cols] * mixed).astype(BF16)

    cqn = _rmsnorm(h[:, o_q:o_kv], qn_ref[...]).astype(BF16)
    q2 = _dot(cqn, wq_ref[...])
    half = MLA_HEADS * HEAD_PAD
    for hd in range(MLA_HEADS):
        cols = slice(hd * HEAD_PAD, (hd + 1) * HEAD_PAD)
        rot = slice(half + hd * HEAD_PAD, half + (hd + 1) * HEAD_PAD)
        q_ref[:, cols] = (q2[:, cols] * cos_p + q2[:, rot] * sin_p).astype(BF16)

    ckvn = _rmsnorm(h[:, o_kv:o_pe], kvn_ref[...]).astype(BF16)
    kv = _dot(ckvn, wkv_ref[...])
    kpe = h[:, o_pe:o_rot] * cos_p + h[:, o_rot:o_rot + HEAD_PAD] * sin_p
    for hd in range(MLA_HEADS):
        cols = slice(hd * HEAD_PAD, (hd + 1) * HEAD_PAD)
        k_ref[:, cols] = (kv[:, cols] + kpe).astype(BF16)
    v_ref[...] = kv[:, half:].astype(BF16)


def _mixer_in(x2d, cos_p, sin_p, w_in, gm_ln_g, gm_ln_b, w_s, b_s, q_norm, w_uq, kv_norm, w_ukv, *, rows):
    n, d = x2d.shape
    hp = MLA_HEADS * HEAD_PAD
    o_pe = 2 * A_WIDTH + Q_LORA + KV_LORA
    half = QK_ROPE // 2
    w_pe = w_in[:, o_pe:o_pe + QK_ROPE]
    w_pe_rot = jnp.concatenate([-w_pe[:, half:], w_pe[:, :half]], axis=1)
    pad_l = jnp.zeros((d, QK_NOPE), F32)
    pad_r = jnp.zeros((d, HEAD_PAD - QK_HEAD), F32)
    w_in_p = jnp.concatenate([w_in[:, :o_pe], pad_l, w_pe, pad_r, pad_l, w_pe_rot, pad_r], axis=1).astype(BF16)
    wq = w_uq.reshape(Q_LORA, MLA_HEADS, QK_HEAD)
    wq_pe = wq[:, :, QK_NOPE:]
    wq_rot = jnp.concatenate([jnp.zeros((Q_LORA, MLA_HEADS, QK_NOPE), F32), -wq_pe[:, :, half:], wq_pe[:, :, :half]],
                             axis=2)
    padq = ((0, 0), (0, 0), (0, HEAD_PAD - QK_HEAD))
    wq2 = jnp.concatenate([jnp.pad(wq, padq).reshape(Q_LORA, hp), jnp.pad(wq_rot, padq).reshape(Q_LORA, hp)],
                          axis=1).astype(BF16)
    wkv = w_ukv.reshape(KV_LORA, MLA_HEADS, QK_NOPE + V_HEAD)
    wk = jnp.pad(wkv[:, :, :QK_NOPE], ((0, 0), (0, 0), (0, HEAD_PAD - QK_NOPE))).reshape(KV_LORA, hp)
    wv = wkv[:, :, QK_NOPE:].reshape(KV_LORA, MLA_HEADS * V_HEAD)
    wkv2 = jnp.concatenate([wk, wv], axis=1).astype(BF16)
    bs_full = jnp.repeat(b_s.T, A_GROUP_DIM, axis=1)

    full = lambda a: pl.BlockSpec(a.shape, lambda i: (0,) * a.ndim)
    row = lambda w: pl.BlockSpec((rows, w), lambda i: (i, 0))
    args = (x2d, cos_p, sin_p, w_in_p, gm_ln_g.reshape(1, -1), gm_ln_b.reshape(1, -1), w_s, bs_full,
            q_norm.reshape(1, -1), wq2, kv_norm.reshape(1, -1), wkv2)
    in_specs = [row(d), row(HEAD_PAD), row(HEAD_PAD)] + [full(a) for a in args[3:]]
    return pl.pallas_call(
        _mixer_in_kernel,
        grid=(n // rows,),
        in_specs=in_specs,
        out_specs=(row(A_WIDTH), row(hp), row(hp), row(MLA_HEADS * V_HEAD)),
        out_shape=(jax.ShapeDtypeStruct((n, A_WIDTH), BF16), jax.ShapeDtypeStruct((n, hp), BF16),
                   jax.ShapeDtypeStruct((n, hp), BF16), jax.ShapeDtypeStruct((n, MLA_HEADS * V_HEAD), BF16)),
        compiler_params=_params(("parallel",)),
        name="mixer_in",
    )(*args)


def _attn_kernel(q_ref, k_ref, v_ref, o_ref, *, blk):
    seq = q_ref.shape[0]
    scale = QK_HEAD ** -0.5
    row = lax.broadcasted_iota(jnp.int32, (blk, blk), 0)
    col = lax.broadcasted_iota(jnp.int32, (blk, blk), 1)
    diag_mask = col <= row
    first_head_lanes = lax.broadcasted_iota(jnp.int32, (blk, 2 * V_HEAD), 1) < V_HEAD

    qk = lambda a, b: lax.dot_general(a, b, (((1,), (1,)), ((), ())), preferred_element_type=F32)

    for j in range(seq // blk):
        q0 = j * blk
        outs = []
        for hh in range(2):
            cols = slice(hh * HEAD_PAD, (hh + 1) * HEAD_PAD)
            q = q_ref[q0:q0 + blk, cols]
            s_d = jnp.where(diag_mask, qk(q, k_ref[q0:q0 + blk, cols]), NEG_BIG)
            m = jnp.max(s_d, axis=-1, keepdims=True)
            if j > 0:
                s_o = qk(q, k_ref[0:q0, cols])
                m = jnp.maximum(m, jnp.max(s_o, axis=-1, keepdims=True))
            p_d = jnp.exp((s_d - m) * scale)
            l = jnp.sum(p_d, axis=-1, keepdims=True)
            acc = _dot(p_d.astype(BF16), v_ref[q0:q0 + blk, :])
            if j > 0:
                p_o = jnp.exp((s_o - m) * scale)
                l = l + jnp.sum(p_o, axis=-1, keepdims=True)
                acc = acc + _dot(p_o.astype(BF16), v_ref[0:q0, :])
            outs.append(acc / l)
        o_ref[q0:q0 + blk, :] = jnp.where(first_head_lanes, outs[0], outs[1]).astype(BF16)


def _attention(q, k, v, *, bsz, seq, blk):
    n = bsz * seq
    pair = 2 * HEAD_PAD
    return pl.pallas_call(
        functools.partial(_attn_kernel, blk=blk),
        grid=(bsz, MLA_HEADS // 2),
        in_specs=[pl.BlockSpec((seq, pair), lambda b, h: (b, h)), pl.BlockSpec((seq, pair), lambda b, h: (b, h)),
                  pl.BlockSpec((seq, 2 * V_HEAD), lambda b, h: (b, h))],
        out_specs=pl.BlockSpec((seq, 2 * V_HEAD), lambda b, h: (b, h)),
        out_shape=jax.ShapeDtypeStruct((n, MLA_HEADS * V_HEAD), BF16),
        compiler_params=_params(("parallel", "parallel")),
        name="attention",
    )(q, k, v)


def _swiglu_chunks(xb, wg_ref, wu_ref, wd_ref):
    dff = wg_ref.shape[1]
    acc = None
    for c0 in range(0, dff, FFN_CHUNK):
        cols = slice(c0, min(c0 + FFN_CHUNK, dff))
        gate = _dot(xb, wg_ref[:, cols])
        up = _dot(xb, wu_ref[:, cols])
        hid = (gate * _sigmoid(gate) * up).astype(BF16)
        part = _dot(hid, wd_ref[cols, :])
        acc = part if acc is None else acc + part
    return acc


def _mixer_ffn_kernel(ya_ref, yb_ref, woa_ref, wob_ref, x_ref, g1_ref, b1_ref, wg_ref, wu_ref, wd_ref, g2_ref, b2_ref,
                      o_ref):
    mix = _dot(ya_ref[...], woa_ref[...]) + _dot(yb_ref[...], wob_ref[...])
    x1 = _layernorm(DEEPNORM_ALPHA * x_ref[...] + mix, g1_ref[...], b1_ref[...])
    ffn = _swiglu_chunks(x1.astype(BF16), wg_ref, wu_ref, wd_ref)
    o_ref[...] = _layernorm(DEEPNORM_ALPHA * x1 + ffn, g2_ref[...], b2_ref[...])


def _mixer_ffn(ya, yb, wo_a, wo_b, x2d, g1, b1, w_gate, w_up, w_down, g2, b2, *, rows):
    n, d = x2d.shape
    once = lambda a: pl.BlockSpec(a.shape, lambda i: (0,) * a.ndim, pipeline_mode=pl.Buffered(1))
    row = lambda w: pl.BlockSpec((rows, w), lambda i: (i, 0))
    vec = lambda v: v.reshape(1, -1)
    args = (ya, yb, wo_a, wo_b, x2d, vec(g1), vec(b1), w_gate, w_up, w_down, vec(g2), vec(b2))
    in_specs = [row(ya.shape[1]), row(yb.shape[1]), once(wo_a), once(wo_b), row(d)] + [once(a) for a in args[5:]]
    return pl.pallas_call(
        _mixer_ffn_kernel,
        grid=(n // rows,),
        in_specs=in_specs,
        out_specs=row(d),
        out_shape=jax.ShapeDtypeStruct((n, d), F32),
        compiler_params=_params(("parallel",)),
        name="mixer_ffn",
    )(*args)


def _s5_kernel(x_ref, perm_ref, win_ref, bd_ref, cd_ref, lam_ref, d_ref, y_hbm, bur_ref, bui_ref, hr_ref, hi_ref,
               st_ref, ys_ref, sem, *, bsz):
    i = pl.program_id(0)
    n_steps = pl.num_programs(0)
    slot = i % 2
    steps = x_ref.shape[1]
    rows = bsz * steps

    def out_copies(at_step, at_slot):
        return [pltpu.make_async_copy(ys_ref.at[at_slot, pl.ds(t * bsz, bsz)], y_hbm.at[:, at_step * steps + t, :],
                                      sem.at[at_slot]) for t in range(steps)]

    @pl.when(i == 0)
    def _():
        st_ref[...] = jnp.zeros_like(st_ref)

    @pl.when(i >= 2)
    def _():
        for cp in out_copies(i - 2, slot):
            cp.wait()

    n_blk, cb, sb2 = bd_ref.shape
    sb = sb2 // 2
    xb = x_ref[...].reshape(rows, x_ref.shape[2]).astype(BF16)
    xb = _dot(perm_ref[...], xb).astype(BF16)
    u = _dot(xb, win_ref[...])
    ub = u.astype(BF16)
    for q in range(n_blk):
        bu = _dot(ub[:, q * cb:(q + 1) * cb], bd_ref[q])
        bur_ref[:, q * sb:(q + 1) * sb] = bu[:, :sb]
        bui_ref[:, q * sb:(q + 1) * sb] = bu[:, sb:]
    for c in range(bur_ref.shape[1] // S5_SCAN_LANES):
        lanes = slice(c * S5_SCAN_LANES, (c + 1) * S5_SCAN_LANES)
        a_re = lam_ref[0:1, lanes]
        a_im = lam_ref[1:2, lanes]
        h_re = st_ref[0, :, lanes]
        h_im = st_ref[1, :, lanes]
        for t in range(steps):
            rws = slice(t * bsz, (t + 1) * bsz)
            n_re = a_re * h_re - a_im * h_im + bur_ref[rws, lanes]
            n_im = a_re * h_im + a_im * h_re + bui_ref[rws, lanes]
            hr_ref[rws, lanes] = n_re.astype(BF16)
            hi_ref[rws, lanes] = n_im.astype(BF16)
            h_re, h_im = n_re, n_im
        st_ref[0, :, lanes] = h_re
        st_ref[1, :, lanes] = h_im
    for q in range(n_blk):
        st = slice(q * sb, (q + 1) * sb)
        ch = slice(q * cb, (q + 1) * cb)
        y = _dot(hr_ref[:, st], cd_ref[q, :sb]) + _dot(hi_ref[:, st], cd_ref[q, sb:])
        ys_ref[slot, :, ch] = y + d_ref[:, ch] * u[:, ch]
    for cp in out_copies(i, slot):
        cp.start()

    @pl.when(i == n_steps - 1)
    def _():
        for cp in out_copies(i, slot):
            cp.wait()

    @pl.when((i == n_steps - 1) & (i >= 1))
    def _():
        for cp in out_copies(i - 1, 1 - slot):
            cp.wait()


def _s5(x2d, w_in, bd, cd, lam, d_skip, *, bsz):
    n, d = x2d.shape
    seq = n // bsz
    rows = S5_STEPS * bsz
    n_state = lam.shape[1]
    full = lambda a: pl.BlockSpec(a.shape, lambda i: (0,) * a.ndim)
    d2 = d_skip.reshape(1, -1)
    r_out = jnp.arange(rows)[:, None]
    r_in = jnp.arange(rows)[None, :]
    perm = (r_in == (r_out % bsz) * S5_STEPS + r_out // bsz).astype(BF16)
    y = pl.pallas_call(
        functools.partial(_s5_kernel, bsz=bsz),
        grid=(seq // S5_STEPS,),
        in_specs=[pl.BlockSpec((bsz, S5_STEPS, d), lambda i: (0, i, 0)), full(perm), full(w_in), full(bd), full(cd),
                  full(lam), full(d2)],
        out_specs=pl.BlockSpec(memory_space=pl.ANY),
        out_shape=jax.ShapeDtypeStruct((bsz, seq, d), F32),
        scratch_shapes=[pltpu.VMEM((rows, n_state), F32), pltpu.VMEM((rows, n_state), F32),
                        pltpu.VMEM((rows, n_state), BF16), pltpu.VMEM((rows, n_state), BF16),
                        pltpu.VMEM((2, bsz, n_state), F32), pltpu.VMEM((2, rows, d), F32),
                        pltpu.SemaphoreType.DMA((2,))],
        compiler_params=_params(("arbitrary",)),
        name="s5",
    )(x2d.reshape(bsz, seq, d), perm, w_in, bd, cd, lam, d2)
    return y.reshape(n, d)


def _s5_weights(a_re, a_im, log_step, b_re, b_im, c_re, c_im):
    n_grp, n_st = a_re.shape
    gpb = S5_GROUPS_PER_BLOCK
    n_blk = n_grp // gpb
    delta = jnp.exp(log_step)[:, None]
    mag = jnp.exp(delta * a_re)
    abar_re = mag * jnp.cos(delta * a_im)
    abar_im = mag * jnp.sin(delta * a_im)
    den = a_re * a_re + a_im * a_im
    coef_re = ((abar_re - 1.0) * a_re + abar_im * a_im) / den
    coef_im = (abar_im * a_re - (abar_re - 1.0) * a_im) / den
    bb_re = coef_re[..., None] * b_re - coef_im[..., None] * b_im
    bb_im = coef_re[..., None] * b_im + coef_im[..., None] * b_re
    same_group = jnp.eye(gpb, dtype=F32)[None, :, None, :, None]

    def expand(w):
        w = jnp.transpose(w.reshape(n_blk, gpb, w.shape[1], w.shape[2]), (0, 1, 3, 2))
        w = w[:, :, :, None, :] * same_group
        return w.reshape(n_blk, gpb * w.shape[2], gpb * w.shape[4])

    bd = jnp.concatenate([expand(bb_re), expand(bb_im)], axis=2).astype(BF16)
    cd = jnp.concatenate([expand(c_re), -expand(c_im)], axis=1).astype(BF16)
    lam = jnp.stack([abar_re.reshape(-1), abar_im.reshape(-1)])
    return bd, cd, lam


def _s5_out_kernel(y_ref, gw_ref, gb_ref, wo_ref, x_ref, g_ref, b_ref, r_ref, o_ref, route_ref, count_ref,
                   carry_ref):
    @pl.when(pl.program_id(0) == 0)
    def _():
        carry_ref[...] = jnp.zeros_like(carry_ref)

    rows = x_ref.shape[0]
    gl = _gelu(y_ref[...])
    z = gl * _sigmoid(_dot(gl.astype(BF16), gw_ref[...]) + gb_ref[...])
    mix = _dot(z.astype(BF16), wo_ref[...])
    xo = _layernorm(DEEPNORM_ALPHA * x_ref[...] + mix, g_ref[...], b_ref[...])
    o_ref[...] = xo
    x_hi = xo.astype(BF16)
    x_lo = (xo - x_hi.astype(F32)).astype(BF16)
    r = r_ref[...]
    r_hi = r.astype(BF16)
    r_lo = (r - r_hi.astype(F32)).astype(BF16)
    logits = _dot(x_hi, r_hi) + (_dot(x_lo, r_hi) + _dot(x_hi, r_lo))
    idx = lax.broadcasted_iota(jnp.int32, logits.shape, 1)
    m1 = jnp.max(logits, axis=-1, keepdims=True)
    i1 = jnp.min(jnp.where(logits == m1, idx, N_EXPERTS), axis=-1, keepdims=True)
    rest = jnp.where(idx == i1, -jnp.inf, logits)
    m2 = jnp.max(rest, axis=-1, keepdims=True)
    i2 = jnp.min(jnp.where(rest == m2, idx, N_EXPERTS), axis=-1, keepdims=True)
    e2 = jnp.exp(m2 - m1)
    g1 = 1.0 / (1.0 + e2)
    g2 = e2 * g1
    sel = jnp.where((idx == i1) | (idx == i2), 1.0, 0.0)
    t_r = lax.broadcasted_iota(jnp.int32, (rows, rows), 0)
    t_c = lax.broadcasted_iota(jnp.int32, (rows, rows), 1)
    earlier = jnp.where(t_c < t_r, 1.0, 0.0).astype(BF16)
    before = _dot(earlier, sel.astype(BF16)) + carry_ref[...]
    r1 = jnp.sum(jnp.where(idx == i1, before, 0.0), axis=-1, keepdims=True)
    r2 = jnp.sum(jnp.where(idx == i2, before, 0.0), axis=-1, keepdims=True)
    total = carry_ref[...] + jnp.sum(sel, axis=0, keepdims=True)
    carry_ref[...] = total
    count_ref[...] = total
    fields = (i1.astype(F32), i2.astype(F32), r1, r2, g1, g2)
    route = jnp.zeros(logits.shape, F32)
    for lane, val in enumerate(fields):
        route = jnp.where(idx == lane, val, route)
    route_ref[...] = route


def _s5_out(y2d, glu_w, glu_b, w_out, x2d, g, b, router, *, rows):
    n, d = x2d.shape
    full = lambda a: pl.BlockSpec(a.shape, lambda i: (0,) * a.ndim)
    row = lambda w: pl.BlockSpec((rows, w), lambda i: (i, 0))
    args = (y2d, glu_w.astype(BF16), glu_b.reshape(1, -1), w_out.astype(BF16), x2d, g.reshape(1, -1),
            b.reshape(1, -1), router)
    in_specs = [row(d), full(args[1]), full(args[2]), full(args[3]), row(d), full(args[5]), full(args[6]),
                full(args[7])]
    return pl.pallas_call(
        _s5_out_kernel,
        grid=(n // rows,),
        in_specs=in_specs,
        out_specs=(row(d), row(N_EXPERTS), pl.BlockSpec((1, N_EXPERTS), lambda i: (0, 0))),
        out_shape=(jax.ShapeDtypeStruct((n, d), F32), jax.ShapeDtypeStruct((n, N_EXPERTS), F32),
                   jax.ShapeDtypeStruct((1, N_EXPERTS), F32)),
        scratch_shapes=[pltpu.VMEM((1, N_EXPERTS), F32)],
        compiler_params=_params(("arbitrary",)),
        name="s5_out",
    )(*args)


def _dispatch_kernel(fill_ref, slot_ref, x_ref, xs_ref, zero_ref, sem, zsem):
    rows = x_ref.shape[0]

    @pl.when(pl.program_id(0) == 0)
    def _():
        zero_ref[...] = jnp.zeros_like(zero_ref)
        for e in range(fill_ref.shape[0]):
            cp = pltpu.make_async_copy(zero_ref, xs_ref.at[pl.ds(pl.multiple_of(fill_ref[e], 8), rows)], zsem)
            cp.start()
            cp.wait()

    def issue(r, carry):
        for k in range(2):
            s = slot_ref[0, 0, 2 * r + k]
            pltpu.make_async_copy(x_ref.at[pl.ds(r, 1)], xs_ref.at[pl.ds(s, 1)], sem).start()
        return carry

    lax.fori_loop(0, rows, issue, 0, unroll=DMA_UNROLL)
    for k in range(2):
        pltpu.make_async_copy(x_ref, xs_ref.at[pl.ds(0, rows)], sem).wait()


def _dispatch(x2d, slots, fill_start, n_slots, *, rows):
    n, d = x2d.shape
    return pl.pallas_call(
        _dispatch_kernel,
        grid_spec=pltpu.PrefetchScalarGridSpec(
            num_scalar_prefetch=1,
            grid=(n // rows,),
            in_specs=[pl.BlockSpec((1, 1, 2 * rows), lambda i, fill: (i, 0, 0), memory_space=pltpu.SMEM),
                      pl.BlockSpec((rows, d), lambda i, fill: (i, 0))],
            out_specs=pl.BlockSpec(memory_space=pl.ANY),
            scratch_shapes=[pltpu.VMEM((rows, d), F32), pltpu.SemaphoreType.DMA, pltpu.SemaphoreType.DMA],
        ),
        out_shape=jax.ShapeDtypeStruct((n_slots, d), F32),
        compiler_params=_params(("arbitrary",)),
        name="moe_dispatch",
    )(fill_start, slots, x2d)


def _ffn_grouped_kernel(te_ref, na_ref, x_ref, wg_ref, wu_ref, wd_ref, o_ref, xb_ref, acc_ref):
    del te_ref
    i = pl.program_id(0)
    f = pl.program_id(1)
    active = i < na_ref[0]

    @pl.when(active)
    def _():
        @pl.when(f == 0)
        def _():
            xb_ref[...] = x_ref[...].astype(BF16)

        contrib = _swiglu_chunks(xb_ref[...], wg_ref.at[0], wu_ref.at[0], wd_ref.at[0])

        @pl.when(f == 0)
        def _():
            acc_ref[...] = contrib

        @pl.when(f > 0)
        def _():
            acc_ref[...] += contrib

    last = f == pl.num_programs(1) - 1

    @pl.when(last & active)
    def _():
        o_ref[...] = acc_ref[...]

    @pl.when(last & jnp.logical_not(active))
    def _():
        o_ref[...] = jnp.zeros_like(o_ref)


def _ffn_grouped(xs, tile_expert, n_active, w_gate, w_up, w_down, *, rows, fchunk):
    m, d = xs.shape
    n_f = w_gate.shape[2] // fchunk
    tile = lambda i, na: jnp.maximum(jnp.minimum(i, na[0] - 1), 0)
    chunk = lambda i, f, na: jnp.where(i < na[0], f, n_f - 1)
    return pl.pallas_call(
        _ffn_grouped_kernel,
        grid_spec=pltpu.PrefetchScalarGridSpec(
            num_scalar_prefetch=2,
            grid=(m // rows, n_f),
            in_specs=[pl.BlockSpec((rows, d), lambda i, f, te, na: (tile(i, na), 0)),
                      pl.BlockSpec((1, d, fchunk), lambda i, f, te, na: (te[tile(i, na)], 0, chunk(i, f, na))),
                      pl.BlockSpec((1, d, fchunk), lambda i, f, te, na: (te[tile(i, na)], 0, chunk(i, f, na))),
                      pl.BlockSpec((1, fchunk, d), lambda i, f, te, na: (te[tile(i, na)], chunk(i, f, na), 0))],
            out_specs=pl.BlockSpec((rows, d), lambda i, f, te, na: (i, 0)),
            scratch_shapes=[pltpu.VMEM((rows, d), BF16), pltpu.VMEM((rows, d), F32)],
        ),
        out_shape=jax.ShapeDtypeStruct((m, d), F32),
        compiler_params=_params(("arbitrary", "arbitrary")),
        name="ffn_moe",
    )(tile_expert, n_active, xs, w_gate, w_up, w_down)


def _combine_kernel(slot_ref, x_ref, route_ref, ys_ref, g_ref, b_ref, o_ref, y1_ref, y2_ref, sem):
    part = x_ref.shape[0] // COMBINE_PARTS

    for h in range(COMBINE_PARTS):
        def issue(r, carry, h=h):
            for k, dst in enumerate((y1_ref, y2_ref)):
                s = slot_ref[0, 0, 2 * r + k]
                pltpu.make_async_copy(ys_ref.at[pl.ds(s, 1)], dst.at[pl.ds(r, 1)], sem.at[h]).start()
            return carry

        lax.fori_loop(h * part, (h + 1) * part, issue, 0, unroll=DMA_UNROLL)
    for h in range(COMBINE_PARTS):
        rws = pl.ds(h * part, part)
        for dst in (y1_ref, y2_ref):
            pltpu.make_async_copy(ys_ref.at[pl.ds(0, part)], dst.at[rws], sem.at[h]).wait()
        route = route_ref[rws, :]
        moe = route[:, 4:5] * y1_ref[rws, :] + route[:, 5:6] * y2_ref[rws, :]
        o_ref[rws, :] = _layernorm(DEEPNORM_ALPHA * x_ref[rws, :] + moe, g_ref[...], b_ref[...])


def _combine_ln(x2d, route, slots, ys, g, b, *, rows):
    n, d = x2d.shape
    g2, b2 = g.reshape(1, -1), b.reshape(1, -1)
    return pl.pallas_call(
        _combine_kernel,
        grid=(n // rows,),
        in_specs=[pl.BlockSpec((1, 1, 2 * rows), lambda i: (i, 0, 0), memory_space=pltpu.SMEM),
                  pl.BlockSpec((rows, d), lambda i: (i, 0)),
                  pl.BlockSpec((rows, N_EXPERTS), lambda i: (i, 0)),
                  pl.BlockSpec(memory_space=pl.ANY),
                  pl.BlockSpec((1, d), lambda i: (0, 0)),
                  pl.BlockSpec((1, d), lambda i: (0, 0))],
        out_specs=pl.BlockSpec((rows, d), lambda i: (i, 0)),
        out_shape=jax.ShapeDtypeStruct((n, d), F32),
        scratch_shapes=[pltpu.VMEM((rows, d), F32), pltpu.VMEM((rows, d), F32),
                        pltpu.SemaphoreType.DMA((COMBINE_PARTS,))],
        compiler_params=_params(("arbitrary",)),
        name="moe_combine",
    )(slots, x2d, route, ys, g2, b2)


def _moe(x2d, route, counts, w_gate, w_up, w_down, g, b, *, rows, fchunk):
    n, _ = x2d.shape
    counts = counts.reshape(-1).astype(jnp.int32)
    padded = (counts + rows - 1) // rows * rows
    ends = jnp.cumsum(padded)
    offs = ends - padded
    n_tiles = (2 * n) // rows + N_EXPERTS
    starts = jnp.arange(n_tiles, dtype=jnp.int32) * rows
    tile_expert = jnp.minimum(jnp.sum(ends[None, :] <= starts[:, None], axis=1), N_EXPERTS - 1).astype(jnp.int32)
    n_active = (ends[-1:] // rows).astype(jnp.int32)
    expert = route[:, 0:2].astype(jnp.int32)
    rank = route[:, 2:4].astype(jnp.int32)
    base = jnp.sum(jnp.where(expert[..., None] == jnp.arange(N_EXPERTS), offs, 0), axis=-1)
    slots = (base + rank).reshape(n // rows, 1, 2 * rows)
    fill = jnp.concatenate([(offs + counts) // 8 * 8, ends[-1] + jnp.arange(N_EXPERTS, dtype=jnp.int32) * rows])
    fill_start = jnp.minimum(fill, (n_tiles - 1) * rows).astype(jnp.int32)
    xs = _dispatch(x2d, slots, fill_start, n_tiles * rows, rows=rows)
    ys = _ffn_grouped(xs, tile_expert, n_active, w_gate, w_up, w_down, rows=rows, fchunk=fchunk)
    return _combine_ln(x2d, route, slots, ys, g, b, rows=rows)


def _layer_even(x2d, positions, bsz, seq, w_in, gm_ln_g, gm_ln_b, w_s, b_s, q_norm, w_uq, kv_norm, w_ukv, w_out,
                ln_g, ln_b, f_gate, f_up, f_down, f_ln_g, f_ln_b, *, rows, ffn_rows, attn_blk):
    cos_p, sin_p = _rope_tables(positions)
    ya, q, k, v = _mixer_in(x2d, cos_p, sin_p, w_in, gm_ln_g, gm_ln_b, w_s, b_s, q_norm, w_uq, kv_norm, w_ukv,
                            rows=rows)
    yb = _attention(q, k, v, bsz=bsz, seq=seq, blk=attn_blk)
    wo = w_out.astype(BF16)
    return _mixer_ffn(ya, yb, wo[:A_WIDTH], wo[A_WIDTH:], x2d, ln_g, ln_b, f_gate.astype(BF16), f_up.astype(BF16),
                      f_down.astype(BF16), f_ln_g, f_ln_b, rows=ffn_rows)


def _layer_odd(x2d, bsz, w_in, a_re, a_im, log_step, b_re, b_im, c_re, c_im, d_skip, glu_w, glu_b, w_out, ln_g, ln_b,
               router, m_gate, m_up, m_down, m_ln_g, m_ln_b, *, rows, ffn_rows, fchunk):
    bd, cd, lam = _s5_weights(a_re, a_im, log_step, b_re, b_im, c_re, c_im)
    y2d = _s5(x2d, w_in.astype(BF16), bd, cd, lam, d_skip, bsz=bsz)
    x2d, route, counts = _s5_out(y2d, glu_w, glu_b, w_out, x2d, ln_g, ln_b, router, rows=rows)
    return _moe(x2d, route, counts, m_gate.astype(BF16), m_up.astype(BF16), m_down.astype(BF16), m_ln_g, m_ln_b,
                rows=ffn_rows, fchunk=fchunk)


def kernel(x, positions, ab_w_in, gm_ln_g, gm_ln_b, gm_w_s, gm_b_s, mla_q_norm, mla_w_uq, mla_kv_norm, mla_w_ukv, ab_w_out, ab_ln_g, ab_ln_b, ffd_w_gate, ffd_w_up, ffd_w_down, ffd_ln_g, ffd_ln_b, c_w_in, s5_a_re, s5_a_im, s5_log_step, s5_b_re, s5_b_im, s5_c_re, s5_c_im, s5_d, glu_w, glu_b, c_w_out, c_ln_g, c_ln_b, moe_router, moe_w_gate, moe_w_up, moe_w_down, moe_ln_g, moe_ln_b):
    bsz, seq, d = x.shape
    x2d = x.reshape(bsz * seq, d)
    rows = min(ROW_TILE, seq)
    for i in range(DEPTH):
        j = i // 2
        if i % 2 == 0:
            x2d = _layer_even(x2d, positions, bsz, seq, ab_w_in[j], gm_ln_g[j], gm_ln_b[j], gm_w_s[j], gm_b_s[j],
                              mla_q_norm[j], mla_w_uq[j], mla_kv_norm[j], mla_w_ukv[j], ab_w_out[j], ab_ln_g[j],
                              ab_ln_b[j], ffd_w_gate[j], ffd_w_up[j], ffd_w_down[j], ffd_ln_g[j], ffd_ln_b[j],
                              rows=rows, ffn_rows=rows, attn_blk=min(ATTN_BLOCK, seq))
        else:
            x2d = _layer_odd(x2d, bsz, c_w_in[j], s5_a_re[j], s5_a_im[j], s5_log_step[j], s5_b_re[j],
                             s5_b_im[j], s5_c_re[j], s5_c_im[j], s5_d[j], glu_w[j], glu_b[j], c_w_out[j], c_ln_g[j],
                             c_ln_b[j], moe_router[j], moe_w_gate[j], moe_w_up[j], moe_w_down[j], moe_ln_g[j],
                             moe_ln_b[j], rows=rows, ffn_rows=rows, fchunk=moe_w_gate.shape[3] // 2)
    return x2d.reshape(bsz, seq, d)
```

```python
import functools
import math

import jax
import jax.numpy as jnp
from jax import lax
from jax.experimental import pallas as pl
from jax.experimental.pallas import tpu as pltpu

F32 = jnp.float32
BF16 = jnp.bfloat16

A_GROUPS = 4
A_GROUP_DIM = 128
A_WIDTH = A_GROUPS * A_GROUP_DIM
A_CHUNK = 128
MLA_HEADS = 8
QK_NOPE = 64
QK_ROPE = 32
QK_HEAD = QK_NOPE + QK_ROPE
V_HEAD = 64
Q_LORA = 384
KV_LORA = 256
ROPE_THETA = 10000.0
S5_GROUP_DIM = 16
S5_STATE = 64
N_EXPERTS = 8
LN_EPS = 1e-5
RMS_EPS = 1e-6
DEPTH = 2
DEEPNORM_ALPHA = (2.0 * DEPTH) ** 0.25

LANES = 128
HEAD_PAD = LANES
S5_GROUPS_PER_BLOCK = 2 * LANES // S5_GROUP_DIM
S5_STEPS = 16
S5_SCAN_LANES = 4 * LANES
VMEM_LIMIT = 56 * 1024 * 1024
NEG_BIG = -1e30
ROW_TILE = 512
ATTN_BLOCK = 512
COMBINE_PARTS = 4
FFN_CHUNK = 512
DMA_UNROLL = 16


def _params(sem):
    return pltpu.CompilerParams(dimension_semantics=sem, vmem_limit_bytes=VMEM_LIMIT)


def _gelu(x):
    c = math.sqrt(2.0 / math.pi)
    return 0.5 * x * (1.0 + jnp.tanh(c * (x + 0.044715 * (x * x * x))))


def _sigmoid(x):
    return 1.0 / (1.0 + jnp.exp(-x))


def _layernorm(x, g, b):
    mu = jnp.mean(x, axis=-1, keepdims=True)
    xc = x - mu
    var = jnp.mean(xc * xc, axis=-1, keepdims=True)
    return xc * lax.rsqrt(var + LN_EPS) * g + b


def _rmsnorm(x, g):
    ms = jnp.mean(x * x, axis=-1, keepdims=True)
    return x * lax.rsqrt(ms + RMS_EPS) * g


def _dot(a, b):
    return jnp.dot(a, b, preferred_element_type=F32)


def _rope_table_kernel(inv_ref, pos_ref, cos_ref, sin_ref):
    pos = pos_ref[...].astype(F32)
    for j in range(QK_ROPE // 2):
        ang = pos * inv_ref[j]
        cos_ref[j] = jnp.cos(ang)
        sin_ref[j] = jnp.sin(ang)


def _rope_tables(positions):
    n = positions.size
    half = QK_ROPE // 2
    inv_freq = 1.0 / (ROPE_THETA ** (jnp.arange(0, QK_ROPE, 2, dtype=F32) / QK_ROPE))
    pos2d = positions.reshape(n // LANES, LANES)
    cos_t, sin_t = pl.pallas_call(
        _rope_table_kernel,
        out_shape=(jax.ShapeDtypeStruct((half, n // LANES, LANES), F32),) * 2,
        in_specs=[pl.BlockSpec(memory_space=pltpu.SMEM), pl.BlockSpec(memory_space=pltpu.VMEM)],
        out_specs=(pl.BlockSpec(memory_space=pltpu.VMEM),) * 2,
        name="rope_table",
    )(inv_freq, pos2d)
    cos_c = cos_t.reshape(half, n).T
    sin_c = sin_t.reshape(half, n).T
    ones = jnp.ones((n, QK_NOPE), F32)
    zeros_n = jnp.zeros((n, QK_NOPE), F32)
    zeros_p = jnp.zeros((n, HEAD_PAD - QK_HEAD), F32)
    cos_p = jnp.concatenate([ones, cos_c, cos_c, zeros_p], axis=1)
    sin_p = jnp.concatenate([zeros_n, sin_c, sin_c, zeros_p], axis=1)
    return cos_p, sin_p


def _mixer_in_kernel(x_ref, cos_ref, sin_ref, w_in_ref, lng_ref, lnb_ref, ws_ref, bs_ref, qn_ref, wq_ref,
                     kvn_ref, wkv_ref, ya_ref, q_ref, k_ref, v_ref):
    rows = x_ref.shape[0]
    xb = x_ref[...].astype(BF16)
    h = _dot(xb, w_in_ref[...])
    o_q = 2 * A_WIDTH
    o_kv = o_q + Q_LORA
    o_pe = o_kv + KV_LORA
    o_rot = o_pe + HEAD_PAD
    cos_p = cos_ref[...]
    sin_p = sin_ref[...]

    a_u = _gelu(h[:, :A_WIDTH])
    a_v = _gelu(h[:, A_WIDTH:o_q])
    vn = _layernorm(a_v, lng_ref[...], lnb_ref[...]).astype(BF16)
    t_idx = lax.broadcasted_iota(jnp.int32, (A_CHUNK, A_CHUNK), 0)
    s_idx = lax.broadcasted_iota(jnp.int32, (A_CHUNK, A_CHUNK), 1)
    causal = s_idx <= t_idx
    bs = bs_ref[...]
    for g in range(A_GROUPS):
        w_g = jnp.where(causal, ws_ref[g], 0.0).astype(BF16)
        cols = slice(g * A_GROUP_DIM, (g + 1) * A_GROUP_DIM)
        for c in range(rows // A_CHUNK):
            rws = slice(c * A_CHUNK, (c + 1) * A_CHUNK)
            mixed = _dot(w_g, vn[rws, cols]) + bs[:, cols]
            ya_ref[rws, cols] = (a_u[rws, cols] * mixed).astype(BF16)

    cqn = _rmsnorm(h[:, o_q:o_kv], qn_ref[...]).astype(BF16)
    q2 = _dot(cqn, wq_ref[...])
    half = MLA_HEADS * HEAD_PAD
    for hd in range(MLA_HEADS):
        cols = slice(hd * HEAD_PAD, (hd + 1) * HEAD_PAD)
        rot = slice(half + hd * HEAD_PAD, half + (hd + 1) * HEAD_PAD)
        q_ref[:, cols] = (q2[:, cols] * cos_p + q2[:, rot] * sin_p).astype(BF16)

    ckvn = _rmsnorm(h[:, o_kv:o_pe], kvn_ref[...]).astype(BF16)
    kv = _dot(ckvn, wkv_ref[...])
    kpe = h[:, o_pe:o_rot] * cos_p + h[:, o_rot:o_rot + HEAD_PAD] * sin_p
    for hd in range(MLA_HEADS):
        cols = slice(hd * HEAD_PAD, (hd + 1) * HEAD_PAD)
        k_ref[:, cols] = (kv[:, cols] + kpe).astype(BF16)
    v_ref[...] = kv[:, half:].astype(BF16)


def _mixer_in(x2d, cos_p, sin_p, w_in, gm_ln_g, gm_ln_b, w_s, b_s, q_norm, w_uq, kv_norm, w_ukv, *, rows):
    n, d = x2d.shape
    hp = MLA_HEADS * HEAD_PAD
    o_pe = 2 * A_WIDTH + Q_LORA + KV_LORA
    half = QK_ROPE // 2
    w_pe = w_in[:, o_pe:o_pe + QK_ROPE]
    w_pe_rot = jnp.concatenate([-w_pe[:, half:], w_pe[:, :half]], axis=1)
    pad_l = jnp.zeros((d, QK_NOPE), F32)
    pad_r = jnp.zeros((d, HEAD_PAD - QK_HEAD), F32)
    w_in_p = jnp.concatenate([w_in[:, :o_pe], pad_l, w_pe, pad_r, pad_l, w_pe_rot, pad_r], axis=1).astype(BF16)
    wq = w_uq.reshape(Q_LORA, MLA_HEADS, QK_HEAD)
    wq_pe = wq[:, :, QK_NOPE:]
    wq_rot = jnp.concatenate([jnp.zeros((Q_LORA, MLA_HEADS, QK_NOPE), F32), -wq_pe[:, :, half:], wq_pe[:, :, :half]],
                             axis=2)
    padq = ((0, 0), (0, 0), (0, HEAD_PAD - QK_HEAD))
    wq2 = jnp.concatenate([jnp.pad(wq, padq).reshape(Q_LORA, hp), jnp.pad(wq_rot, padq).reshape(Q_LORA, hp)],
                          axis=1).astype(BF16)
    wkv = w_ukv.reshape(KV_LORA, MLA_HEADS, QK_NOPE + V_HEAD)
    wk = jnp.pad(wkv[:, :, :QK_NOPE], ((0, 0), (0, 0), (0, HEAD_PAD - QK_NOPE))).reshape(KV_LORA, hp)
    wv = wkv[:, :, QK_NOPE:].reshape(KV_LORA, MLA_HEADS * V_HEAD)
    wkv2 = jnp.concatenate([wk, wv], axis=1).astype(BF16)
    bs_full = jnp.repeat(b_s.T, A_GROUP_DIM, axis=1)

    full = lambda a: pl.BlockSpec(a.shape, lambda i: (0,) * a.ndim)
    row = lambda w: pl.BlockSpec((rows, w), lambda i: (i, 0))
    args = (x2d, cos_p, sin_p, w_in_p, gm_ln_g.reshape(1, -1), gm_ln_b.reshape(1, -1), w_s, bs_full,
            q_norm.reshape(1, -1), wq2, kv_norm.reshape(1, -1), wkv2)
    in_specs = [row(d), row(HEAD_PAD), row(HEAD_PAD)] + [full(a) for a in args[3:]]
    return pl.pallas_call(
        _mixer_in_kernel,
        grid=(n // rows,),
        in_specs=in_specs,
        out_specs=(row(A_WIDTH), row(hp), row(hp), row(MLA_HEADS * V_HEAD)),
        out_shape=(jax.ShapeDtypeStruct((n, A_WIDTH), BF16), jax.ShapeDtypeStruct((n, hp), BF16),
                   jax.ShapeDtypeStruct((n, hp), BF16), jax.ShapeDtypeStruct((n, MLA_HEADS * V_HEAD), BF16)),
        compiler_params=_params(("parallel",)),
        name="mixer_in",
    )(*args)


def _attn_kernel(q_ref, k_ref, v_ref, o_ref, *, blk):
    seq = q_ref.shape[0]
    scale = QK_HEAD ** -0.5
    row = lax.broadcasted_iota(jnp.int32, (blk, blk), 0)
    col = lax.broadcasted_iota(jnp.int32, (blk, blk), 1)
    diag_mask = col <= row
    first_head_lanes = lax.broadcasted_iota(jnp.int32, (blk, 2 * V_HEAD), 1) < V_HEAD

    qk = lambda a, b: lax.dot_general(a, b, (((1,), (1,)), ((), ())), preferred_element_type=F32)

    for j in range(seq // blk):
        q0 = j * blk
        outs = []
        for hh in range(2):
            cols = slice(hh * HEAD_PAD, (hh + 1) * HEAD_PAD)
            q = q_ref[q0:q0 + blk, cols]
            s_d = jnp.where(diag_mask, qk(q, k_ref[q0:q0 + blk, cols]), NEG_BIG)
            m = jnp.max(s_d, axis=-1, keepdims=True)
            if j > 0:
                s_o = qk(q, k_ref[0:q0, cols])
                m = jnp.maximum(m, jnp.max(s_o, axis=-1, keepdims=True))
            p_d = jnp.exp((s_d - m) * scale)
            l = jnp.sum(p_d, axis=-1, keepdims=True)
            acc = _dot(p_d.astype(BF16), v_ref[q0:q0 + blk, :])
            if j > 0:
                p_o = jnp.exp((s_o - m) * scale)
                l = l + jnp.sum(p_o, axis=-1, keepdims=True)
                acc = acc + _dot(p_o.astype(BF16), v_ref[0:q0, :])
            outs.append(acc / l)
        o_ref[q0:q0 + blk, :] = jnp.where(first_head_lanes, outs[0], outs[1]).astype(BF16)


def _attention(q, k, v, *, bsz, seq, blk):
    n = bsz * seq
    pair = 2 * HEAD_PAD
    return pl.pallas_call(
        functools.partial(_attn_kernel, blk=blk),
        grid=(bsz, MLA_HEADS // 2),
        in_specs=[pl.BlockSpec((seq, pair), lambda b, h: (b, h)), pl.BlockSpec((seq, pair), lambda b, h: (b, h)),
                  pl.BlockSpec((seq, 2 * V_HEAD), lambda b, h: (b, h))],
        out_specs=pl.BlockSpec((seq, 2 * V_HEAD), lambda b, h: (b, h)),
        out_shape=jax.ShapeDtypeStruct((n, MLA_HEADS * V_HEAD), BF16),
        compiler_params=_params(("parallel", "parallel")),
        name="attention",
    )(q, k, v)


def _swiglu_chunks(xb, wg_ref, wu_ref, wd_ref):
    dff = wg_ref.shape[1]
    acc = None
    for c0 in range(0, dff, FFN_CHUNK):
        cols = slice(c0, min(c0 + FFN_CHUNK, dff))
        gate = _dot(xb, wg_ref[:, cols])
        up = _dot(xb, wu_ref[:, cols])
        hid = (gate * _sigmoid(gate) * up).astype(BF16)
        part = _dot(hid, wd_ref[cols, :])
        acc = part if acc is None else acc + part
    return acc


def _mixer_ffn_kernel(ya_ref, yb_ref, woa_ref, wob_ref, x_ref, g1_ref, b1_ref, wg_ref, wu_ref, wd_ref, g2_ref, b2_ref,
                      o_ref):
    mix = _dot(ya_ref[...], woa_ref[...]) + _dot(yb_ref[...], wob_ref[...])
    x1 = _layernorm(DEEPNORM_ALPHA * x_ref[...] + mix, g1_ref[...], b1_ref[...])
    ffn = _swiglu_chunks(x1.astype(BF16), wg_ref, wu_ref, wd_ref)
    o_ref[...] = _layernorm(DEEPNORM_ALPHA * x1 + ffn, g2_ref[...], b2_ref[...])


def _mixer_ffn(ya, yb, wo_a, wo_b, x2d, g1, b1, w_gate, w_up, w_down, g2, b2, *, rows):
    n, d = x2d.shape
    once = lambda a: pl.BlockSpec(a.shape, lambda i: (0,) * a.ndim, pipeline_mode=pl.Buffered(1))
    row = lambda w: pl.BlockSpec((rows, w), lambda i: (i, 0))
    vec = lambda v: v.reshape(1, -1)
    args = (ya, yb, wo_a, wo_b, x2d, vec(g1), vec(b1), w_gate, w_up, w_down, vec(g2), vec(b2))
    in_specs = [row(ya.shape[1]), row(yb.shape[1]), once(wo_a), once(wo_b), row(d)] + [once(a) for a in args[5:]]
    return pl.pallas_call(
        _mixer_ffn_kernel,
        grid=(n // rows,),
        in_specs=in_specs,
        out_specs=row(d),
        out_shape=jax.ShapeDtypeStruct((n, d), F32),
        compiler_params=_params(("parallel",)),
        name="mixer_ffn",
    )(*args)


def _s5_kernel(x_ref, perm_ref, win_ref, bd_ref, cd_ref, lam_ref, d_ref, y_hbm, bur_ref, bui_ref, hr_ref, hi_ref,
               st_ref, ys_ref, sem, *, bsz):
    i = pl.program_id(0)
    n_steps = pl.num_programs(0)
    slot = i % 2
    steps = x_ref.shape[1]
    rows = bsz * steps

    def out_copies(at_step, at_slot):
        return [pltpu.make_async_copy(ys_ref.at[at_slot, pl.ds(t * bsz, bsz)], y_hbm.at[:, at_step * steps + t, :],
                                      sem.at[at_slot]) for t in range(steps)]

    @pl.when(i == 0)
    def _():
        st_ref[...] = jnp.zeros_like(st_ref)

    @pl.when(i >= 2)
    def _():
        for cp in out_copies(i - 2, slot):
            cp.wait()

    n_blk, cb, sb2 = bd_ref.shape
    sb = sb2 // 2
    xb = x_ref[...].reshape(rows, x_ref.shape[2]).astype(BF16)
    xb = _dot(perm_ref[...], xb).astype(BF16)
    u = _dot(xb, win_ref[...])
    ub = u.astype(BF16)
    for q in range(n_blk):
        bu = _dot(ub[:, q * cb:(q + 1) * cb], bd_ref[q])
        bur_ref[:, q * sb:(q + 1) * sb] = bu[:, :sb]
        bui_ref[:, q * sb:(q + 1) * sb] = bu[:, sb:]
    for c in range(bur_ref.shape[1] // S5_SCAN_LANES):
        lanes = slice(c * S5_SCAN_LANES, (c + 1) * S5_SCAN_LANES)
        a_re = lam_ref[0:1, lanes]
        a_im = lam_ref[1:2, lanes]
        h_re = st_ref[0, :, lanes]
        h_im = st_ref[1, :, lanes]
        for t in range(steps):
            rws = slice(t * bsz, (t + 1) * bsz)
            n_re = a_re * h_re - a_im * h_im + bur_ref[rws, lanes]
            n_im = a_re * h_im + a_im * h_re + bui_ref[rws, lanes]
            hr_ref[rws, lanes] = n_re.astype(BF16)
            hi_ref[rws, lanes] = n_im.astype(BF16)
            h_re, h_im = n_re, n_im
        st_ref[0, :, lanes] = h_re
        st_ref[1, :, lanes] = h_im
    for q in range(n_blk):
        st = slice(q * sb, (q + 1) * sb)
        ch = slice(q * cb, (q + 1) * cb)
        y = _dot(hr_ref[:, st], cd_ref[q, :sb]) + _dot(hi_ref[:, st], cd_ref[q, sb:])
        ys_ref[slot, :, ch] = y + d_ref[:, ch] * u[:, ch]
    for cp in out_copies(i, slot):
        cp.start()

    @pl.when(i == n_steps - 1)
    def _():
        for cp in out_copies(i, slot):
            cp.wait()

    @pl.when((i == n_steps - 1) & (i >= 1))
    def _():
        for cp in out_copies(i - 1, 1 - slot):
            cp.wait()


def _s5(x2d, w_in, bd, cd, lam, d_skip, *, bsz):
    n, d = x2d.shape
    seq = n // bsz
    rows = S5_STEPS * bsz
    n_state = lam.shape[1]
    full = lambda a: pl.BlockSpec(a.shape, lambda i: (0,) * a.ndim)
    d2 = d_skip.reshape(1, -1)
    r_out = jnp.arange(rows)[:, None]
    r_in = jnp.arange(rows)[None, :]
    perm = (r_in == (r_out % bsz) * S5_STEPS + r_out // bsz).astype(BF16)
    y = pl.pallas_call(
        functools.partial(_s5_kernel, bsz=bsz),
        grid=(seq // S5_STEPS,),
        in_specs=[pl.BlockSpec((bsz, S5_STEPS, d), lambda i: (0, i, 0)), full(perm), full(w_in), full(bd), full(cd),
                  full(lam), full(d2)],
        out_specs=pl.BlockSpec(memory_space=pl.ANY),
        out_shape=jax.ShapeDtypeStruct((bsz, seq, d), F32),
        scratch_shapes=[pltpu.VMEM((rows, n_state), F32), pltpu.VMEM((rows, n_state), F32),
                        pltpu.VMEM((rows, n_state), BF16), pltpu.VMEM((rows, n_state), BF16),
                        pltpu.VMEM((2, bsz, n_state), F32), pltpu.VMEM((2, rows, d), F32),
                        pltpu.SemaphoreType.DMA((2,))],
        compiler_params=_params(("arbitrary",)),
        name="s5",
    )(x2d.reshape(bsz, seq, d), perm, w_in, bd, cd, lam, d2)
    return y.reshape(n, d)


def _s5_weights(a_re, a_im, log_step, b_re, b_im, c_re, c_im):
    n_grp, n_st = a_re.shape
    gpb = S5_GROUPS_PER_BLOCK
    n_blk = n_grp // gpb
    delta = jnp.exp(log_step)[:, None]
    mag = jnp.exp(delta * a_re)
    abar_re = mag * jnp.cos(delta * a_im)
    abar_im = mag * jnp.sin(delta * a_im)
    den = a_re * a_re + a_im * a_im
    coef_re = ((abar_re - 1.0) * a_re + abar_im * a_im) / den
    coef_im = (abar_im * a_re - (abar_re - 1.0) * a_im) / den
    bb_re = coef_re[..., None] * b_re - coef_im[..., None] * b_im
    bb_im = coef_re[..., None] * b_im + coef_im[..., None] * b_re
    same_group = jnp.eye(gpb, dtype=F32)[None, :, None, :, None]

    def expand(w):
        w = jnp.transpose(w.reshape(n_blk, gpb, w.shape[1], w.shape[2]), (0, 1, 3, 2))
        w = w[:, :, :, None, :] * same_group
        return w.reshape(n_blk, gpb * w.shape[2], gpb * w.shape[4])

    bd = jnp.concatenate([expand(bb_re), expand(bb_im)], axis=2).astype(BF16)
    cd = jnp.concatenate([expand(c_re), -expand(c_im)], axis=1).astype(BF16)
    lam = jnp.stack([abar_re.reshape(-1), abar_im.reshape(-1)])
    return bd, cd, lam


def _s5_out_kernel(y_ref, gw_ref, gb_ref, wo_ref, x_ref, g_ref, b_ref, r_ref, o_ref, route_ref, count_ref,
                   carry_ref):
    @pl.when(pl.program_id(0) == 0)
    def _():
        carry_ref[...] = jnp.zeros_like(carry_ref)

    rows = x_ref.shape[0]
    gl = _gelu(y_ref[...])
    z = gl * _sigmoid(_dot(gl.astype(BF16), gw_ref[...]) + gb_ref[...])
    mix = _dot(z.astype(BF16), wo_ref[...])
    xo = _layernorm(DEEPNORM_ALPHA * x_ref[...] + mix, g_ref[...], b_ref[...])
    o_ref[...] = xo
    x_hi = xo.astype(BF16)
    x_lo = (xo - x_hi.astype(F32)).astype(BF16)
    r = r_ref[...]
    r_hi = r.astype(BF16)
    r_lo = (r - r_hi.astype(F32)).astype(BF16)
    logits = _dot(x_hi, r_hi) + (_dot(x_lo, r_hi) + _dot(x_hi, r_lo))
    idx = lax.broadcasted_iota(jnp.int32, logits.shape, 1)
    m1 = jnp.max(logits, axis=-1, keepdims=True)
    i1 = jnp.min(jnp.where(logits == m1, idx, N_EXPERTS), axis=-1, keepdims=True)
    rest = jnp.where(idx == i1, -jnp.inf, logits)
    m2 = jnp.max(rest, axis=-1, keepdims=True)
    i2 = jnp.min(jnp.where(rest == m2, idx, N_EXPERTS), axis=-1, keepdims=True)
    e2 = jnp.exp(m2 - m1)
    g1 = 1.0 / (1.0 + e2)
    g2 = e2 * g1
    sel = jnp.where((idx == i1) | (idx == i2), 1.0, 0.0)
    t_r = lax.broadcasted_iota(jnp.int32, (rows, rows), 0)
    t_c = lax.broadcasted_iota(jnp.int32, (rows, rows), 1)
    earlier = jnp.where(t_c < t_r, 1.0, 0.0).astype(BF16)
    before = _dot(earlier, sel.astype(BF16)) + carry_ref[...]
    r1 = jnp.sum(jnp.where(idx == i1, before, 0.0), axis=-1, keepdims=True)
    r2 = jnp.sum(jnp.where(idx == i2, before, 0.0), axis=-1, keepdims=True)
    total = carry_ref[...] + jnp.sum(sel, axis=0, keepdims=True)
    carry_ref[...] = total
    count_ref[...] = total
    fields = (i1.astype(F32), i2.astype(F32), r1, r2, g1, g2)
    route = jnp.zeros(logits.shape, F32)
    for lane, val in enumerate(fields):
        route = jnp.where(idx == lane, val, route)
    route_ref[...] = route


def _s5_out(y2d, glu_w, glu_b, w_out, x2d, g, b, router, *, rows):
    n, d = x2d.shape
    full = lambda a: pl.BlockSpec(a.shape, lambda i: (0,) * a.ndim)
    row = lambda w: pl.BlockSpec((rows, w), lambda i: (i, 0))
    args = (y2d, glu_w.astype(BF16), glu_b.reshape(1, -1), w_out.astype(BF16), x2d, g.reshape(1, -1),
            b.reshape(1, -1), router)
    in_specs = [row(d), full(args[1]), full(args[2]), full(args[3]), row(d), full(args[5]), full(args[6]),
                full(args[7])]
    return pl.pallas_call(
        _s5_out_kernel,
        grid=(n // rows,),
        in_specs=in_specs,
        out_specs=(row(d), row(N_EXPERTS), pl.BlockSpec((1, N_EXPERTS), lambda i: (0, 0))),
        out_shape=(jax.ShapeDtypeStruct((n, d), F32), jax.ShapeDtypeStruct((n, N_EXPERTS), F32),
                   jax.ShapeDtypeStruct((1, N_EXPERTS), F32)),
        scratch_shapes=[pltpu.VMEM((1, N_EXPERTS), F32)],
        compiler_params=_params(("arbitrary",)),
        name="s5_out",
    )(*args)


def _dispatch_kernel(fill_ref, slot_ref, x_ref, xs_ref, zero_ref, sem, zsem):
    rows = x_ref.shape[0]

    @pl.when(pl.program_id(0) == 0)
    def _():
        zero_ref[...] = jnp.zeros_like(zero_ref)
        for e in range(fill_ref.shape[0]):
            cp = pltpu.make_async_copy(zero_ref, xs_ref.at[pl.ds(pl.multiple_of(fill_ref[e], 8), rows)], zsem)
            cp.start()
            cp.wait()

    def issue(r, carry):
        for k in range(2):
            s = slot_ref[0, 0, 2 * r + k]
            pltpu.make_async_copy(x_ref.at[pl.ds(r, 1)], xs_ref.at[pl.ds(s, 1)], sem).start()
        return carry

    lax.fori_loop(0, rows, issue, 0, unroll=DMA_UNROLL)
    for k in range(2):
        pltpu.make_async_copy(x_ref, xs_ref.at[pl.ds(0, rows)], sem).wait()


def _dispatch(x2d, slots, fill_start, n_slots, *, rows):
    n, d = x2d.shape
    return pl.pallas_call(
        _dispatch_kernel,
        grid_spec=pltpu.PrefetchScalarGridSpec(
            num_scalar_prefetch=1,
            grid=(n // rows,),
            in_specs=[pl.BlockSpec((1, 1, 2 * rows), lambda i, fill: (i, 0, 0), memory_space=pltpu.SMEM),
                      pl.BlockSpec((rows, d), lambda i, fill: (i, 0))],
            out_specs=pl.BlockSpec(memory_space=pl.ANY),
            scratch_shapes=[pltpu.VMEM((rows, d), F32), pltpu.SemaphoreType.DMA, pltpu.SemaphoreType.DMA],
        ),
        out_shape=jax.ShapeDtypeStruct((n_slots, d), F32),
        compiler_params=_params(("arbitrary",)),
        name="moe_dispatch",
    )(fill_start, slots, x2d)


def _ffn_grouped_kernel(te_ref, na_ref, x_ref, wg_ref, wu_ref, wd_ref, o_ref, xb_ref, acc_ref):
    del te_ref
    i = pl.program_id(0)
    f = pl.program_id(1)
    active = i < na_ref[0]

    @pl.when(active)
    def _():
        @pl.when(f == 0)
        def _():
            xb_ref[...] = x_ref[...].astype(BF16)

        contrib = _swiglu_chunks(xb_ref[...], wg_ref.at[0], wu_ref.at[0], wd_ref.at[0])

        @pl.when(f == 0)
        def _():
            acc_ref[...] = contrib

        @pl.when(f > 0)
        def _():
            acc_ref[...] += contrib

    last = f == pl.num_programs(1) - 1

    @pl.when(last & active)
    def _():
        o_ref[...] = acc_ref[...]

    @pl.when(last & jnp.logical_not(active))
    def _():
        o_ref[...] = jnp.zeros_like(o_ref)


def _ffn_grouped(xs, tile_expert, n_active, w_gate, w_up, w_down, *, rows, fchunk):
    m, d = xs.shape
    n_f = w_gate.shape[2] // fchunk
    tile = lambda i, na: jnp.maximum(jnp.minimum(i, na[0] - 1), 0)
    chunk = lambda i, f, na: jnp.where(i < na[0], f, n_f - 1)
    return pl.pallas_call(
        _ffn_grouped_kernel,
        grid_spec=pltpu.PrefetchScalarGridSpec(
            num_scalar_prefetch=2,
            grid=(m // rows, n_f),
            in_specs=[pl.BlockSpec((rows, d), lambda i, f, te, na: (tile(i, na), 0)),
                      pl.BlockSpec((1, d, fchunk), lambda i, f, te, na: (te[tile(i, na)], 0, chunk(i, f, na)),
                                   pipeline_mode=pl.Buffered(1)),
                      pl.BlockSpec((1, d, fchunk), lambda i, f, te, na: (te[tile(i, na)], 0, chunk(i, f, na)),
                                   pipeline_mode=pl.Buffered(1)),
                      pl.BlockSpec((1, fchunk, d), lambda i, f, te, na: (te[tile(i, na)], chunk(i, f, na), 0),
                                   pipeline_mode=pl.Buffered(1))],
            out_specs=pl.BlockSpec((rows, d), lambda i, f, te, na: (i, 0)),
            scratch_shapes=[pltpu.VMEM((rows, d), BF16), pltpu.VMEM((rows, d), F32)],
        ),
        out_shape=jax.ShapeDtypeStruct((m, d), F32),
        compiler_params=_params(("arbitrary", "arbitrary")),
        name="ffn_moe",
    )(tile_expert, n_active, xs, w_gate, w_up, w_down)


def _combine_kernel(slot_ref, x_ref, route_ref, ys_ref, g_ref, b_ref, o_ref, y1_ref, y2_ref, sem):
    part = x_ref.shape[0] // COMBINE_PARTS

    for h in range(COMBINE_PARTS):
        def issue(r, carry, h=h):
            for k, dst in enumerate((y1_ref, y2_ref)):
                s = slot_ref[0, 0, 2 * r + k]
                pltpu.make_async_copy(ys_ref.at[pl.ds(s, 1)], dst.at[pl.ds(r, 1)], sem.at[h]).start()
            return carry

        lax.fori_loop(h * part, (h + 1) * part, issue, 0, unroll=DMA_UNROLL)
    for h in range(COMBINE_PARTS):
        rws = pl.ds(h * part, part)
        for dst in (y1_ref, y2_ref):
            pltpu.make_async_copy(ys_ref.at[pl.ds(0, part)], dst.at[rws], sem.at[h]).wait()
        route = route_ref[rws, :]
        moe = route[:, 4:5] * y1_ref[rws, :] + route[:, 5:6] * y2_ref[rws, :]
        o_ref[rws, :] = _layernorm(DEEPNORM_ALPHA * x_ref[rws, :] + moe, g_ref[...], b_ref[...])


def _combine_ln(x2d, route, slots, ys, g, b, *, rows):
    n, d = x2d.shape
    g2, b2 = g.reshape(1, -1), b.reshape(1, -1)
    return pl.pallas_call(
        _combine_kernel,
        grid=(n // rows,),
        in_specs=[pl.BlockSpec((1, 1, 2 * rows), lambda i: (i, 0, 0), memory_space=pltpu.SMEM),
                  pl.BlockSpec((rows, d), lambda i: (i, 0)),
                  pl.BlockSpec((rows, N_EXPERTS), lambda i: (i, 0)),
                  pl.BlockSpec(memory_space=pl.ANY),
                  pl.BlockSpec((1, d), lambda i: (0, 0)),
                  pl.BlockSpec((1, d), lambda i: (0, 0))],
        out_specs=pl.BlockSpec((rows, d), lambda i: (i, 0)),
        out_shape=jax.ShapeDtypeStruct((n, d), F32),
        scratch_shapes=[pltpu.VMEM((rows, d), F32), pltpu.VMEM((rows, d), F32),
                        pltpu.SemaphoreType.DMA((COMBINE_PARTS,))],
        compiler_params=_params(("arbitrary",)),
        name="moe_combine",
    )(slots, x2d, route, ys, g2, b2)


def _moe(x2d, route, counts, w_gate, w_up, w_down, g, b, *, rows, fchunk):
    n, _ = x2d.shape
    counts = counts.reshape(-1).astype(jnp.int32)
    padded = (counts + rows - 1) // rows * rows
    ends = jnp.cumsum(padded)
    offs = ends - padded
    n_tiles = (2 * n) // rows + N_EXPERTS
    starts = jnp.arange(n_tiles, dtype=jnp.int32) * rows
    tile_expert = jnp.minimum(jnp.sum(ends[None, :] <= starts[:, None], axis=1), N_EXPERTS - 1).astype(jnp.int32)
    n_active = (ends[-1:] // rows).astype(jnp.int32)
    expert = route[:, 0:2].astype(jnp.int32)
    rank = route[:, 2:4].astype(jnp.int32)
    base = jnp.sum(jnp.where(expert[..., None] == jnp.arange(N_EXPERTS), offs, 0), axis=-1)
    slots = (base + rank).reshape(n // rows, 1, 2 * rows)
    fill = jnp.concatenate([(offs + counts) // 8 * 8, ends[-1] + jnp.arange(N_EXPERTS, dtype=jnp.int32) * rows])
    fill_start = jnp.minimum(fill, (n_tiles - 1) * rows).astype(jnp.int32)
    xs = _dispatch(x2d, slots, fill_start, n_tiles * rows, rows=rows)
    ys = _ffn_grouped(xs, tile_expert, n_active, w_gate, w_up, w_down, rows=rows, fchunk=fchunk)
    return _combine_ln(x2d, route, slots, ys, g, b, rows=rows)


def _layer_even(x2d, positions, bsz, seq, w_in, gm_ln_g, gm_ln_b, w_s, b_s, q_norm, w_uq, kv_norm, w_ukv, w_out,
                ln_g, ln_b, f_gate, f_up, f_down, f_ln_g, f_ln_b, *, rows, ffn_rows, attn_blk):
    cos_p, sin_p = _rope_tables(positions)
    ya, q, k, v = _mixer_in(x2d, cos_p, sin_p, w_in, gm_ln_g, gm_ln_b, w_s, b_s, q_norm, w_uq, kv_norm, w_ukv,
                            rows=rows)
    yb = _attention(q, k, v, bsz=bsz, seq=seq, blk=attn_blk)
    wo = w_out.astype(BF16)
    return _mixer_ffn(ya, yb, wo[:A_WIDTH], wo[A_WIDTH:], x2d, ln_g, ln_b, f_gate.astype(BF16), f_up.astype(BF16),
                      f_down.astype(BF16), f_ln_g, f_ln_b, rows=ffn_rows)


def _layer_odd(x2d, bsz, w_in, a_re, a_im, log_step, b_re, b_im, c_re, c_im, d_skip, glu_w, glu_b, w_out, ln_g, ln_b,
               router, m_gate, m_up, m_down, m_ln_g, m_ln_b, *, rows, ffn_rows, fchunk):
    bd, cd, lam = _s5_weights(a_re, a_im, log_step, b_re, b_im, c_re, c_im)
    y2d = _s5(x2d, w_in.astype(BF16), bd, cd, lam, d_skip, bsz=bsz)
    x2d, route, counts = _s5_out(y2d, glu_w, glu_b, w_out, x2d, ln_g, ln_b, router, rows=rows)
    return _moe(x2d, route, counts, m_gate.astype(BF16), m_up.astype(BF16), m_down.astype(BF16), m_ln_g, m_ln_b,
                rows=ffn_rows, fchunk=fchunk)


def kernel(x, positions, ab_w_in, gm_ln_g, gm_ln_b, gm_w_s, gm_b_s, mla_q_norm, mla_w_uq, mla_kv_norm, mla_w_ukv, ab_w_out, ab_ln_g, ab_ln_b, ffd_w_gate, ffd_w_up, ffd_w_down, ffd_ln_g, ffd_ln_b, c_w_in, s5_a_re, s5_a_im, s5_log_step, s5_b_re, s5_b_im, s5_c_re, s5_c_im, s5_d, glu_w, glu_b, c_w_out, c_ln_g, c_ln_b, moe_router, moe_w_gate, moe_w_up, moe_w_down, moe_ln_g, moe_ln_b):
    bsz, seq, d = x.shape
    x2d = x.reshape(bsz * seq, d)
    rows = min(ROW_TILE, seq)
    for i in range(DEPTH):
        j = i // 2
        if i % 2 == 0:
            x2d = _layer_even(x2d, positions, bsz, seq, ab_w_in[j], gm_ln_g[j], gm_ln_b[j], gm_w_s[j], gm_b_s[j],
                              mla_q_norm[j], mla_w_uq[j], mla_kv_norm[j], mla_w_ukv[j], ab_w_out[j], ab_ln_g[j],
                              ab_ln_b[j], ffd_w_gate[j], ffd_w_up[j], ffd_w_down[j], ffd_ln_g[j], ffd_ln_b[j],
                              rows=rows, ffn_rows=rows, attn_blk=min(ATTN_BLOCK, seq))
        else:
            x2d = _layer_odd(x2d, bsz, c_w_in[j], s5_a_re[j], s5_a_im[j], s5_log_step[j], s5_b_re[j],
                             s5_b_im[j], s5_c_re[j], s5_c_im[j], s5_d[j], glu_w[j], glu_b[j], c_w_out[j], c_ln_g[j],
                             c_ln_b[j], moe_router[j], moe_w_gate[j], moe_w_up[j], moe_w_down[j], moe_ln_g[j],
                             moe_ln_b[j], rows=rows, ffn_rows=rows, fchunk=moe_w_gate.shape[3])
    return x2d.reshape(bsz, seq, d)
```

```python
import functools
import math

import jax
import jax.numpy as jnp
from jax import lax
from jax.experimental import pallas as pl
from jax.experimental.pallas import tpu as pltpu

F32 = jnp.float32
BF16 = jnp.bfloat16

A_GROUPS = 4
A_GROUP_DIM = 128
A_WIDTH = A_GROUPS * A_GROUP_DIM
A_CHUNK = 128
MLA_HEADS = 8
QK_NOPE = 64
QK_ROPE = 32
QK_HEAD = QK_NOPE + QK_ROPE
V_HEAD = 64
Q_LORA = 384
KV_LORA = 256
ROPE_THETA = 10000.0
S5_GROUP_DIM = 16
S5_STATE = 64
N_EXPERTS = 8
LN_EPS = 1e-5
RMS_EPS = 1e-6
DEPTH = 2
DEEPNORM_ALPHA = (2.0 * DEPTH) ** 0.25

LANES = 128
HEAD_PAD = LANES
S5_GROUPS_PER_BLOCK = 2 * LANES // S5_GROUP_DIM
S5_STEPS = 16
S5_SCAN_LANES = 4 * LANES
VMEM_LIMIT = 56 * 1024 * 1024
NEG_BIG = -1e30
ROW_TILE = 512
ATTN_BLOCK = 512
COMBINE_PARTS = 4
FFN_CHUNK = 512
DMA_UNROLL = 16


def _params(sem):
    return pltpu.CompilerParams(dimension_semantics=sem, vmem_limit_bytes=VMEM_LIMIT)


def _gelu(x):
    c = math.sqrt(2.0 / math.pi)
    return 0.5 * x * (1.0 + jnp.tanh(c * (x + 0.044715 * (x * x * x))))


def _sigmoid(x):
    return 1.0 / (1.0 + jnp.exp(-x))


def _layernorm(x, g, b):
    mu = jnp.mean(x, axis=-1, keepdims=True)
    xc = x - mu
    var = jnp.mean(xc * xc, axis=-1, keepdims=True)
    return xc * lax.rsqrt(var + LN_EPS) * g + b


def _rmsnorm(x, g):
    ms = jnp.mean(x * x, axis=-1, keepdims=True)
    return x * lax.rsqrt(ms + RMS_EPS) * g


def _dot(a, b):
    return jnp.dot(a, b, preferred_element_type=F32)


def _rope_table_kernel(inv_ref, pos_ref, cos_ref, sin_ref):
    pos = pos_ref[...].astype(F32)
    for j in range(QK_ROPE // 2):
        ang = pos * inv_ref[j]
        cos_ref[j] = jnp.cos(ang)
        sin_ref[j] = jnp.sin(ang)


def _rope_tables(positions):
    n = positions.size
    half = QK_ROPE // 2
    inv_freq = 1.0 / (ROPE_THETA ** (jnp.arange(0, QK_ROPE, 2, dtype=F32) / QK_ROPE))
    pos2d = positions.reshape(n // LANES, LANES)
    cos_t, sin_t = pl.pallas_call(
        _rope_table_kernel,
        out_shape=(jax.ShapeDtypeStruct((half, n // LANES, LANES), F32),) * 2,
        in_specs=[pl.BlockSpec(memory_space=pltpu.SMEM), pl.BlockSpec(memory_space=pltpu.VMEM)],
        out_specs=(pl.BlockSpec(memory_space=pltpu.VMEM),) * 2,
        name="rope_table",
    )(inv_freq, pos2d)
    cos_c = cos_t.reshape(half, n).T
    sin_c = sin_t.reshape(half, n).T
    ones = jnp.ones((n, QK_NOPE), F32)
    zeros_n = jnp.zeros((n, QK_NOPE), F32)
    zeros_p = jnp.zeros((n, HEAD_PAD - QK_HEAD), F32)
    cos_p = jnp.concatenate([ones, cos_c, cos_c, zeros_p], axis=1)
    sin_p = jnp.concatenate([zeros_n, sin_c, sin_c, zeros_p], axis=1)
    return cos_p, sin_p


def _mixer_in_kernel(x_ref, cos_ref, sin_ref, w_in_ref, lng_ref, lnb_ref, ws_ref, bs_ref, qn_ref, wq_ref,
                     kvn_ref, wkv_ref, ya_ref, q_ref, k_ref, v_ref):
    rows = x_ref.shape[0]
    xb = x_ref[...].astype(BF16)
    h = _dot(xb, w_in_ref[...])
    o_q = 2 * A_WIDTH
    o_kv = o_q + Q_LORA
    o_pe = o_kv + KV_LORA
    o_rot = o_pe + HEAD_PAD
    cos_p = cos_ref[...]
    sin_p = sin_ref[...]

    a_u = _gelu(h[:, :A_WIDTH])
    a_v = _gelu(h[:, A_WIDTH:o_q])
    vn = _layernorm(a_v, lng_ref[...], lnb_ref[...]).astype(BF16)
    t_idx = lax.broadcasted_iota(jnp.int32, (A_CHUNK, A_CHUNK), 0)
    s_idx = lax.broadcasted_iota(jnp.int32, (A_CHUNK, A_CHUNK), 1)
    causal = s_idx <= t_idx
    bs = bs_ref[...]
    for g in range(A_GROUPS):
        w_g = jnp.where(causal, ws_ref[g], 0.0).astype(BF16)
        cols = slice(g * A_GROUP_DIM, (g + 1) * A_GROUP_DIM)
        for c in range(rows // A_CHUNK):
            rws = slice(c * A_CHUNK, (c + 1) * A_CHUNK)
            mixed = _dot(w_g, vn[rws, cols]) + bs[:, cols]
            ya_ref[rws, cols] = (a_u[rws, cols] * mixed).astype(BF16)

    cqn = _rmsnorm(h[:, o_q:o_kv], qn_ref[...]).astype(BF16)
    q2 = _dot(cqn, wq_ref[...])
    half = MLA_HEADS * HEAD_PAD
    for hd in range(MLA_HEADS):
        cols = slice(hd * HEAD_PAD, (hd + 1) * HEAD_PAD)
        rot = slice(half + hd * HEAD_PAD, half + (hd + 1) * HEAD_PAD)
        q_ref[:, cols] = (q2[:, cols] * cos_p + q2[:, rot] * sin_p).astype(BF16)

    ckvn = _rmsnorm(h[:, o_kv:o_pe], kvn_ref[...]).astype(BF16)
    kv = _dot(ckvn, wkv_ref[...])
    kpe = h[:, o_pe:o_rot] * cos_p + h[:, o_rot:o_rot + HEAD_PAD] * sin_p
    for hd in range(MLA_HEADS):
        cols = slice(hd * HEAD_PAD, (hd + 1) * HEAD_PAD)
        k_ref[:, cols] = (kv[:, cols] + kpe).astype(BF16)
    v_ref[...] = kv[:, half:].astype(BF16)


def _mixer_in(x2d, cos_p, sin_p, w_in, gm_ln_g, gm_ln_b, w_s, b_s, q_norm, w_uq, kv_norm, w_ukv, *, rows):
    n, d = x2d.shape
    hp = MLA_HEADS * HEAD_PAD
    o_pe = 2 * A_WIDTH + Q_LORA + KV_LORA
    half = QK_ROPE // 2
    w_pe = w_in[:, o_pe:o_pe + QK_ROPE]
    w_pe_rot = jnp.concatenate([-w_pe[:, half:], w_pe[:, :half]], axis=1)
    pad_l = jnp.zeros((d, QK_NOPE), F32)
    pad_r = jnp.zeros((d, HEAD_PAD - QK_HEAD), F32)
    w_in_p = jnp.concatenate([w_in[:, :o_pe], pad_l, w_pe, pad_r, pad_l, w_pe_rot, pad_r], axis=1).astype(BF16)
    wq = w_uq.reshape(Q_LORA, MLA_HEADS, QK_HEAD)
    wq_pe = wq[:, :, QK_NOPE:]
    wq_rot = jnp.concatenate([jnp.zeros((Q_LORA, MLA_HEADS, QK_NOPE), F32), -wq_pe[:, :, half:], wq_pe[:, :, :half]],
                             axis=2)
    padq = ((0, 0), (0, 0), (0, HEAD_PAD - QK_HEAD))
    wq2 = jnp.concatenate([jnp.pad(wq, padq).reshape(Q_LORA, hp), jnp.pad(wq_rot, padq).reshape(Q_LORA, hp)],
                          axis=1).astype(BF16)
    wkv = w_ukv.reshape(KV_LORA, MLA_HEADS, QK_NOPE + V_HEAD)
    wk = jnp.pad(wkv[:, :, :QK_NOPE], ((0, 0), (0, 0), (0, HEAD_PAD - QK_NOPE))).reshape(KV_LORA, hp)
    wv = wkv[:, :, QK_NOPE:].reshape(KV_LORA, MLA_HEADS * V_HEAD)
    wkv2 = jnp.concatenate([wk, wv], axis=1).astype(BF16)
    bs_full = jnp.repeat(b_s.T, A_GROUP_DIM, axis=1)

    full = lambda a: pl.BlockSpec(a.shape, lambda i: (0,) * a.ndim)
    row = lambda w: pl.BlockSpec((rows, w), lambda i: (i, 0))
    args = (x2d, cos_p, sin_p, w_in_p, gm_ln_g.reshape(1, -1), gm_ln_b.reshape(1, -1), w_s, bs_full,
            q_norm.reshape(1, -1), wq2, kv_norm.reshape(1, -1), wkv2)
    in_specs = [row(d), row(HEAD_PAD), row(HEAD_PAD)] + [full(a) for a in args[3:]]
    return pl.pallas_call(
        _mixer_in_kernel,
        grid=(n // rows,),
        in_specs=in_specs,
        out_specs=(row(A_WIDTH), row(hp), row(hp), row(MLA_HEADS * V_HEAD)),
        out_shape=(jax.ShapeDtypeStruct((n, A_WIDTH), BF16), jax.ShapeDtypeStruct((n, hp), BF16),
                   jax.ShapeDtypeStruct((n, hp), BF16), jax.ShapeDtypeStruct((n, MLA_HEADS * V_HEAD), BF16)),
        compiler_params=_params(("parallel",)),
        name="mixer_in",
    )(*args)


def _attn_kernel(q_ref, k_ref, v_ref, o_ref, *, blk):
    seq = q_ref.shape[0]
    scale = QK_HEAD ** -0.5
    row = lax.broadcasted_iota(jnp.int32, (blk, blk), 0)
    col = lax.broadcasted_iota(jnp.int32, (blk, blk), 1)
    diag_mask = col <= row
    first_head_lanes = lax.broadcasted_iota(jnp.int32, (blk, 2 * V_HEAD), 1) < V_HEAD

    qk = lambda a, b: lax.dot_general(a, b, (((1,), (1,)), ((), ())), preferred_element_type=F32)

    for j in range(seq // blk):
        q0 = j * blk
        outs = []
        for hh in range(2):
            cols = slice(hh * HEAD_PAD, (hh + 1) * HEAD_PAD)
            q = q_ref[q0:q0 + blk, cols]
            s_d = jnp.where(diag_mask, qk(q, k_ref[q0:q0 + blk, cols]), NEG_BIG)
            m = jnp.max(s_d, axis=-1, keepdims=True)
            if j > 0:
                s_o = qk(q, k_ref[0:q0, cols])
                m = jnp.maximum(m, jnp.max(s_o, axis=-1, keepdims=True))
            p_d = jnp.exp((s_d - m) * scale)
            l = jnp.sum(p_d, axis=-1, keepdims=True)
            acc = _dot(p_d.astype(BF16), v_ref[q0:q0 + blk, :])
            if j > 0:
                p_o = jnp.exp((s_o - m) * scale)
                l = l + jnp.sum(p_o, axis=-1, keepdims=True)
                acc = acc + _dot(p_o.astype(BF16), v_ref[0:q0, :])
            outs.append(acc / l)
        o_ref[q0:q0 + blk, :] = jnp.where(first_head_lanes, outs[0], outs[1]).astype(BF16)


def _attention(q, k, v, *, bsz, seq, blk):
    n = bsz * seq
    pair = 2 * HEAD_PAD
    return pl.pallas_call(
        functools.partial(_attn_kernel, blk=blk),
        grid=(bsz, MLA_HEADS // 2),
        in_specs=[pl.BlockSpec((seq, pair), lambda b, h: (b, h)), pl.BlockSpec((seq, pair), lambda b, h: (b, h)),
                  pl.BlockSpec((seq, 2 * V_HEAD), lambda b, h: (b, h))],
        out_specs=pl.BlockSpec((seq, 2 * V_HEAD), lambda b, h: (b, h)),
        out_shape=jax.ShapeDtypeStruct((n, MLA_HEADS * V_HEAD), BF16),
        compiler_params=_params(("parallel", "parallel")),
        name="attention",
    )(q, k, v)


def _swiglu_chunks(xb, wg_ref, wu_ref, wd_ref):
    dff = wg_ref.shape[1]
    acc = None
    for c0 in range(0, dff, FFN_CHUNK):
        cols = slice(c0, min(c0 + FFN_CHUNK, dff))
        gate = _dot(xb, wg_ref[:, cols])
        up = _dot(xb, wu_ref[:, cols])
        hid = (gate * _sigmoid(gate) * up).astype(BF16)
        part = _dot(hid, wd_ref[cols, :])
        acc = part if acc is None else acc + part
    return acc


def _mixer_ffn_kernel(ya_ref, yb_ref, woa_ref, wob_ref, x_ref, g1_ref, b1_ref, wg_ref, wu_ref, wd_ref, g2_ref, b2_ref,
                      o_ref):
    mix = _dot(ya_ref[...], woa_ref[...]) + _dot(yb_ref[...], wob_ref[...])
    x1 = _layernorm(DEEPNORM_ALPHA * x_ref[...] + mix, g1_ref[...], b1_ref[...])
    ffn = _swiglu_chunks(x1.astype(BF16), wg_ref, wu_ref, wd_ref)
    o_ref[...] = _layernorm(DEEPNORM_ALPHA * x1 + ffn, g2_ref[...], b2_ref[...])


def _mixer_ffn(ya, yb, wo_a, wo_b, x2d, g1, b1, w_gate, w_up, w_down, g2, b2, *, rows):
    n, d = x2d.shape
    once = lambda a: pl.BlockSpec(a.shape, lambda i: (0,) * a.ndim, pipeline_mode=pl.Buffered(1))
    row = lambda w: pl.BlockSpec((rows, w), lambda i: (i, 0))
    vec = lambda v: v.reshape(1, -1)
    args = (ya, yb, wo_a, wo_b, x2d, vec(g1), vec(b1), w_gate, w_up, w_down, vec(g2), vec(b2))
    in_specs = [row(ya.shape[1]), row(yb.shape[1]), once(wo_a), once(wo_b), row(d)] + [once(a) for a in args[5:]]
    return pl.pallas_call(
        _mixer_ffn_kernel,
        grid=(n // rows,),
        in_specs=in_specs,
        out_specs=row(d),
        out_shape=jax.ShapeDtypeStruct((n, d), F32),
        compiler_params=_params(("parallel",)),
        name="mixer_ffn",
    )(*args)


def _s5_kernel(x_ref, perm_ref, win_ref, bd_ref, cd_ref, lam_ref, d_ref, y_hbm, bur_ref, bui_ref, hr_ref, hi_ref,
               st_ref, ys_ref, sem, *, bsz):
    i = pl.program_id(0)
    n_steps = pl.num_programs(0)
    slot = i % 2
    steps = x_ref.shape[1]
    rows = bsz * steps

    def out_copies(at_step, at_slot):
        return [pltpu.make_async_copy(ys_ref.at[at_slot, pl.ds(t * bsz, bsz)], y_hbm.at[:, at_step * steps + t, :],
                                      sem.at[at_slot]) for t in range(steps)]

    @pl.when(i == 0)
    def _():
        st_ref[...] = jnp.zeros_like(st_ref)

    @pl.when(i >= 2)
    def _():
        for cp in out_copies(i - 2, slot):
            cp.wait()

    n_blk, cb, sb2 = bd_ref.shape
    sb = sb2 // 2
    xb = x_ref[...].reshape(rows, x_ref.shape[2]).astype(BF16)
    xb = _dot(perm_ref[...], xb).astype(BF16)
    u = _dot(xb, win_ref[...])
    ub = u.astype(BF16)
    for q in range(n_blk):
        bu = _dot(ub[:, q * cb:(q + 1) * cb], bd_ref[q])
        bur_ref[:, q * sb:(q + 1) * sb] = bu[:, :sb]
        bui_ref[:, q * sb:(q + 1) * sb] = bu[:, sb:]
    for c in range(bur_ref.shape[1] // S5_SCAN_LANES):
        lanes = slice(c * S5_SCAN_LANES, (c + 1) * S5_SCAN_LANES)
        a_re = lam_ref[0:1, lanes]
        a_im = lam_ref[1:2, lanes]
        h_re = st_ref[0, :, lanes]
        h_im = st_ref[1, :, lanes]
        for t in range(steps):
            rws = slice(t * bsz, (t + 1) * bsz)
            n_re = a_re * h_re - a_im * h_im + bur_ref[rws, lanes]
            n_im = a_re * h_im + a_im * h_re + bui_ref[rws, lanes]
            hr_ref[rws, lanes] = n_re.astype(BF16)
            hi_ref[rws, lanes] = n_im.astype(BF16)
            h_re, h_im = n_re, n_im
        st_ref[0, :, lanes] = h_re
        st_ref[1, :, lanes] = h_im
    for q in range(n_blk):
        st = slice(q * sb, (q + 1) * sb)
        ch = slice(q * cb, (q + 1) * cb)
        y = _dot(hr_ref[:, st], cd_ref[q, :sb]) + _dot(hi_ref[:, st], cd_ref[q, sb:])
        ys_ref[slot, :, ch] = y + d_ref[:, ch] * u[:, ch]
    for cp in out_copies(i, slot):
        cp.start()

    @pl.when(i == n_steps - 1)
    def _():
        for cp in out_copies(i, slot):
            cp.wait()

    @pl.when((i == n_steps - 1) & (i >= 1))
    def _():
        for cp in out_copies(i - 1, 1 - slot):
            cp.wait()


def _s5(x2d, w_in, bd, cd, lam, d_skip, *, bsz):
    n, d = x2d.shape
    seq = n // bsz
    rows = S5_STEPS * bsz
    n_state = lam.shape[1]
    full = lambda a: pl.BlockSpec(a.shape, lambda i: (0,) * a.ndim)
    d2 = d_skip.reshape(1, -1)
    r_out = jnp.arange(rows)[:, None]
    r_in = jnp.arange(rows)[None, :]
    perm = (r_in == (r_out % bsz) * S5_STEPS + r_out // bsz).astype(BF16)
    y = pl.pallas_call(
        functools.partial(_s5_kernel, bsz=bsz),
        grid=(seq // S5_STEPS,),
        in_specs=[pl.BlockSpec((bsz, S5_STEPS, d), lambda i: (0, i, 0)), full(perm), full(w_in), full(bd), full(cd),
                  full(lam), full(d2)],
        out_specs=pl.BlockSpec(memory_space=pl.ANY),
        out_shape=jax.ShapeDtypeStruct((bsz, seq, d), F32),
        scratch_shapes=[pltpu.VMEM((rows, n_state), F32), pltpu.VMEM((rows, n_state), F32),
                        pltpu.VMEM((rows, n_state), BF16), pltpu.VMEM((rows, n_state), BF16),
                        pltpu.VMEM((2, bsz, n_state), F32), pltpu.VMEM((2, rows, d), F32),
                        pltpu.SemaphoreType.DMA((2,))],
        compiler_params=_params(("arbitrary",)),
        name="s5",
    )(x2d.reshape(bsz, seq, d), perm, w_in, bd, cd, lam, d2)
    return y.reshape(n, d)


def _s5_weights(a_re, a_im, log_step, b_re, b_im, c_re, c_im):
    n_grp, n_st = a_re.shape
    gpb = S5_GROUPS_PER_BLOCK
    n_blk = n_grp // gpb
    delta = jnp.exp(log_step)[:, None]
    mag = jnp.exp(delta * a_re)
    abar_re = mag * jnp.cos(delta * a_im)
    abar_im = mag * jnp.sin(delta * a_im)
    den = a_re * a_re + a_im * a_im
    coef_re = ((abar_re - 1.0) * a_re + abar_im * a_im) / den
    coef_im = (abar_im * a_re - (abar_re - 1.0) * a_im) / den
    bb_re = coef_re[..., None] * b_re - coef_im[..., None] * b_im
    bb_im = coef_re[..., None] * b_im + coef_im[..., None] * b_re
    same_group = jnp.eye(gpb, dtype=F32)[None, :, None, :, None]

    def expand(w):
        w = jnp.transpose(w.reshape(n_blk, gpb, w.shape[1], w.shape[2]), (0, 1, 3, 2))
        w = w[:, :, :, None, :] * same_group
        return w.reshape(n_blk, gpb * w.shape[2], gpb * w.shape[4])

    bd = jnp.concatenate([expand(bb_re), expand(bb_im)], axis=2).astype(BF16)
    cd = jnp.concatenate([expand(c_re), -expand(c_im)], axis=1).astype(BF16)
    lam = jnp.stack([abar_re.reshape(-1), abar_im.reshape(-1)])
    return bd, cd, lam


def _s5_out_kernel(y_ref, gw_ref, gb_ref, wo_ref, x_ref, g_ref, b_ref, r_ref, o_ref, route_ref, count_ref,
                   carry_ref):
    @pl.when(pl.program_id(0) == 0)
    def _():
        carry_ref[...] = jnp.zeros_like(carry_ref)

    rows = x_ref.shape[0]
    gl = _gelu(y_ref[...])
    z = gl * _sigmoid(_dot(gl.astype(BF16), gw_ref[...]) + gb_ref[...])
    mix = _dot(z.astype(BF16), wo_ref[...])
    xo = _layernorm(DEEPNORM_ALPHA * x_ref[...] + mix, g_ref[...], b_ref[...])
    o_ref[...] = xo
    x_hi = xo.astype(BF16)
    x_lo = (xo - x_hi.astype(F32)).astype(BF16)
    r = r_ref[...]
    r_hi = r.astype(BF16)
    r_lo = (r - r_hi.astype(F32)).astype(BF16)
    logits = _dot(x_hi, r_hi) + (_dot(x_lo, r_hi) + _dot(x_hi, r_lo))
    idx = lax.broadcasted_iota(jnp.int32, logits.shape, 1)
    m1 = jnp.max(logits, axis=-1, keepdims=True)
    i1 = jnp.min(jnp.where(logits == m1, idx, N_EXPERTS), axis=-1, keepdims=True)
    rest = jnp.where(idx == i1, -jnp.inf, logits)
    m2 = jnp.max(rest, axis=-1, keepdims=True)
    i2 = jnp.min(jnp.where(rest == m2, idx, N_EXPERTS), axis=-1, keepdims=True)
    e2 = jnp.exp(m2 - m1)
    g1 = 1.0 / (1.0 + e2)
    g2 = e2 * g1
    sel = jnp.where((idx == i1) | (idx == i2), 1.0, 0.0)
    t_r = lax.broadcasted_iota(jnp.int32, (rows, rows), 0)
    t_c = lax.broadcasted_iota(jnp.int32, (rows, rows), 1)
    earlier = jnp.where(t_c < t_r, 1.0, 0.0).astype(BF16)
    before = _dot(earlier, sel.astype(BF16)) + carry_ref[...]
    r1 = jnp.sum(jnp.where(idx == i1, before, 0.0), axis=-1, keepdims=True)
    r2 = jnp.sum(jnp.where(idx == i2, before, 0.0), axis=-1, keepdims=True)
    total = carry_ref[...] + jnp.sum(sel, axis=0, keepdims=True)
    carry_ref[...] = total
    count_ref[...] = total
    fields = (i1.astype(F32), i2.astype(F32), r1, r2, g1, g2)
    route = jnp.zeros(logits.shape, F32)
    for lane, val in enumerate(fields):
        route = jnp.where(idx == lane, val, route)
    route_ref[...] = route


def _s5_out(y2d, glu_w, glu_b, w_out, x2d, g, b, router, *, rows):
    n, d = x2d.shape
    full = lambda a: pl.BlockSpec(a.shape, lambda i: (0,) * a.ndim)
    row = lambda w: pl.BlockSpec((rows, w), lambda i: (i, 0))
    args = (y2d, glu_w.astype(BF16), glu_b.reshape(1, -1), w_out.astype(BF16), x2d, g.reshape(1, -1),
            b.reshape(1, -1), router)
    in_specs = [row(d), full(args[1]), full(args[2]), full(args[3]), row(d), full(args[5]), full(args[6]),
                full(args[7])]
    return pl.pallas_call(
        _s5_out_kernel,
        grid=(n // rows,),
        in_specs=in_specs,
        out_specs=(row(d), row(N_EXPERTS), pl.BlockSpec((1, N_EXPERTS), lambda i: (0, 0))),
        out_shape=(jax.ShapeDtypeStruct((n, d), F32), jax.ShapeDtypeStruct((n, N_EXPERTS), F32),
                   jax.ShapeDtypeStruct((1, N_EXPERTS), F32)),
        scratch_shapes=[pltpu.VMEM((1, N_EXPERTS), F32)],
        compiler_params=_params(("arbitrary",)),
        name="s5_out",
    )(*args)


def _dispatch_kernel(fill_ref, slot_ref, x_ref, xs_ref, zero_ref, sem, zsem):
    rows = x_ref.shape[0]

    @pl.when(pl.program_id(0) == 0)
    def _():
        zero_ref[...] = jnp.zeros_like(zero_ref)
        for e in range(fill_ref.shape[0]):
            cp = pltpu.make_async_copy(zero_ref, xs_ref.at[pl.ds(pl.multiple_of(fill_ref[e], 8), rows)], zsem)
            cp.start()
            cp.wait()

    def issue(r, carry):
        for k in range(2):
            s = slot_ref[0, 0, 2 * r + k]
            pltpu.make_async_copy(x_ref.at[pl.ds(r, 1)], xs_ref.at[pl.ds(s, 1)], sem).start(priority=k)
        return carry

    lax.fori_loop(0, rows, issue, 0, unroll=DMA_UNROLL)
    for k in range(2):
        pltpu.make_async_copy(x_ref, xs_ref.at[pl.ds(0, rows)], sem).wait()


def _dispatch(x2d, slots, fill_start, n_slots, *, rows):
    n, d = x2d.shape
    return pl.pallas_call(
        _dispatch_kernel,
        grid_spec=pltpu.PrefetchScalarGridSpec(
            num_scalar_prefetch=1,
            grid=(n // rows,),
            in_specs=[pl.BlockSpec((1, 1, 2 * rows), lambda i, fill: (i, 0, 0), memory_space=pltpu.SMEM),
                      pl.BlockSpec((rows, d), lambda i, fill: (i, 0))],
            out_specs=pl.BlockSpec(memory_space=pl.ANY),
            scratch_shapes=[pltpu.VMEM((rows, d), F32), pltpu.SemaphoreType.DMA, pltpu.SemaphoreType.DMA],
        ),
        out_shape=jax.ShapeDtypeStruct((n_slots, d), F32),
        compiler_params=_params(("arbitrary",)),
        name="moe_dispatch",
    )(fill_start, slots, x2d)


def _ffn_grouped_kernel(te_ref, na_ref, x_ref, wg_ref, wu_ref, wd_ref, o_ref, xb_ref, acc_ref):
    del te_ref
    i = pl.program_id(0)
    f = pl.program_id(1)
    active = i < na_ref[0]

    @pl.when(active)
    def _():
        @pl.when(f == 0)
        def _():
            xb_ref[...] = x_ref[...].astype(BF16)

        contrib = _swiglu_chunks(xb_ref[...], wg_ref.at[0], wu_ref.at[0], wd_ref.at[0])

        @pl.when(f == 0)
        def _():
            acc_ref[...] = contrib

        @pl.when(f > 0)
        def _():
            acc_ref[...] += contrib

    last = f == pl.num_programs(1) - 1

    @pl.when(last & active)
    def _():
        o_ref[...] = acc_ref[...]

    @pl.when(last & jnp.logical_not(active))
    def _():
        o_ref[...] = jnp.zeros_like(o_ref)


def _ffn_grouped(xs, tile_expert, n_active, w_gate, w_up, w_down, *, rows, fchunk):
    m, d = xs.shape
    n_f = w_gate.shape[2] // fchunk
    tile = lambda i, na: jnp.maximum(jnp.minimum(i, na[0] - 1), 0)
    chunk = lambda i, f, na: jnp.where(i < na[0], f, n_f - 1)
    return pl.pallas_call(
        _ffn_grouped_kernel,
        grid_spec=pltpu.PrefetchScalarGridSpec(
            num_scalar_prefetch=2,
            grid=(m // rows, n_f),
            in_specs=[pl.BlockSpec((rows, d), lambda i, f, te, na: (tile(i, na), 0)),
                      pl.BlockSpec((1, d, fchunk), lambda i, f, te, na: (te[tile(i, na)], 0, chunk(i, f, na)),
                                   pipeline_mode=pl.Buffered(1)),
                      pl.BlockSpec((1, d, fchunk), lambda i, f, te, na: (te[tile(i, na)], 0, chunk(i, f, na)),
                                   pipeline_mode=pl.Buffered(1)),
                      pl.BlockSpec((1, fchunk, d), lambda i, f, te, na: (te[tile(i, na)], chunk(i, f, na), 0),
                                   pipeline_mode=pl.Buffered(1))],
            out_specs=pl.BlockSpec((rows, d), lambda i, f, te, na: (i, 0)),
            scratch_shapes=[pltpu.VMEM((rows, d), BF16), pltpu.VMEM((rows, d), F32)],
        ),
        out_shape=jax.ShapeDtypeStruct((m, d), F32),
        compiler_params=_params(("arbitrary", "arbitrary")),
        name="ffn_moe",
    )(tile_expert, n_active, xs, w_gate, w_up, w_down)


def _combine_kernel(slot_ref, x_ref, route_ref, ys_ref, g_ref, b_ref, o_ref, y1_ref, y2_ref, sem):
    part = x_ref.shape[0] // COMBINE_PARTS

    for h in range(COMBINE_PARTS):
        def issue(r, carry, h=h):
            for k, dst in enumerate((y1_ref, y2_ref)):
                s = slot_ref[0, 0, 2 * r + k]
                pltpu.make_async_copy(ys_ref.at[pl.ds(s, 1)], dst.at[pl.ds(r, 1)], sem.at[h]).start(priority=k)
            return carry

        lax.fori_loop(h * part, (h + 1) * part, issue, 0, unroll=DMA_UNROLL)
    for h in range(COMBINE_PARTS):
        rws = pl.ds(h * part, part)
        for dst in (y1_ref, y2_ref):
            pltpu.make_async_copy(ys_ref.at[pl.ds(0, part)], dst.at[rws], sem.at[h]).wait()
        route = route_ref[rws, :]
        moe = route[:, 4:5] * y1_ref[rws, :] + route[:, 5:6] * y2_ref[rws, :]
        o_ref[rws, :] = _layernorm(DEEPNORM_ALPHA * x_ref[rws, :] + moe, g_ref[...], b_ref[...])


def _combine_ln(x2d, route, slots, ys, g, b, *, rows):
    n, d = x2d.shape
    g2, b2 = g.reshape(1, -1), b.reshape(1, -1)
    return pl.pallas_call(
        _combine_kernel,
        grid=(n // rows,),
        in_specs=[pl.BlockSpec((1, 1, 2 * rows), lambda i: (i, 0, 0), memory_space=pltpu.SMEM),
                  pl.BlockSpec((rows, d), lambda i: (i, 0)),
                  pl.BlockSpec((rows, N_EXPERTS), lambda i: (i, 0)),
                  pl.BlockSpec(memory_space=pl.ANY),
                  pl.BlockSpec((1, d), lambda i: (0, 0)),
                  pl.BlockSpec((1, d), lambda i: (0, 0))],
        out_specs=pl.BlockSpec((rows, d), lambda i: (i, 0)),
        out_shape=jax.ShapeDtypeStruct((n, d), F32),
        scratch_shapes=[pltpu.VMEM((rows, d), F32), pltpu.VMEM((rows, d), F32),
                        pltpu.SemaphoreType.DMA((COMBINE_PARTS,))],
        compiler_params=_params(("arbitrary",)),
        name="moe_combine",
    )(slots, x2d, route, ys, g2, b2)


def _moe(x2d, route, counts, w_gate, w_up, w_down, g, b, *, rows, fchunk):
    n, _ = x2d.shape
    counts = counts.reshape(-1).astype(jnp.int32)
    padded = (counts + rows - 1) // rows * rows
    ends = jnp.cumsum(padded)
    offs = ends - padded
    n_tiles = (2 * n) // rows + N_EXPERTS
    starts = jnp.arange(n_tiles, dtype=jnp.int32) * rows
    tile_expert = jnp.minimum(jnp.sum(ends[None, :] <= starts[:, None], axis=1), N_EXPERTS - 1).astype(jnp.int32)
    n_active = (ends[-1:] // rows).astype(jnp.int32)
    expert = route[:, 0:2].astype(jnp.int32)
    rank = route[:, 2:4].astype(jnp.int32)
    base = jnp.sum(jnp.where(expert[..., None] == jnp.arange(N_EXPERTS), offs, 0), axis=-1)
    slots = (base + rank).reshape(n // rows, 1, 2 * rows)
    fill = jnp.concatenate([(offs + counts) // 8 * 8, ends[-1] + jnp.arange(N_EXPERTS, dtype=jnp.int32) * rows])
    fill_start = jnp.minimum(fill, (n_tiles - 1) * rows).astype(jnp.int32)
    xs = _dispatch(x2d, slots, fill_start, n_tiles * rows, rows=rows)
    ys = _ffn_grouped(xs, tile_expert, n_active, w_gate, w_up, w_down, rows=rows, fchunk=fchunk)
    return _combine_ln(x2d, route, slots, ys, g, b, rows=rows)


def _layer_even(x2d, positions, bsz, seq, w_in, gm_ln_g, gm_ln_b, w_s, b_s, q_norm, w_uq, kv_norm, w_ukv, w_out,
                ln_g, ln_b, f_gate, f_up, f_down, f_ln_g, f_ln_b, *, rows, ffn_rows, attn_blk):
    cos_p, sin_p = _rope_tables(positions)
    ya, q, k, v = _mixer_in(x2d, cos_p, sin_p, w_in, gm_ln_g, gm_ln_b, w_s, b_s, q_norm, w_uq, kv_norm, w_ukv,
                            rows=rows)
    yb = _attention(q, k, v, bsz=bsz, seq=seq, blk=attn_blk)
    wo = w_out.astype(BF16)
    return _mixer_ffn(ya, yb, wo[:A_WIDTH], wo[A_WIDTH:], x2d, ln_g, ln_b, f_gate.astype(BF16), f_up.astype(BF16),
                      f_down.astype(BF16), f_ln_g, f_ln_b, rows=ffn_rows)


def _layer_odd(x2d, bsz, w_in, a_re, a_im, log_step, b_re, b_im, c_re, c_im, d_skip, glu_w, glu_b, w_out, ln_g, ln_b,
               router, m_gate, m_up, m_down, m_ln_g, m_ln_b, *, rows, ffn_rows, fchunk):
    bd, cd, lam = _s5_weights(a_re, a_im, log_step, b_re, b_im, c_re, c_im)
    y2d = _s5(x2d, w_in.astype(BF16), bd, cd, lam, d_skip, bsz=bsz)
    x2d, route, counts = _s5_out(y2d, glu_w, glu_b, w_out, x2d, ln_g, ln_b, router, rows=rows)
    return _moe(x2d, route, counts, m_gate.astype(BF16), m_up.astype(BF16), m_down.astype(BF16), m_ln_g, m_ln_b,
                rows=ffn_rows, fchunk=fchunk)


def kernel(x, positions, ab_w_in, gm_ln_g, gm_ln_b, gm_w_s, gm_b_s, mla_q_norm, mla_w_uq, mla_kv_norm, mla_w_ukv, ab_w_out, ab_ln_g, ab_ln_b, ffd_w_gate, ffd_w_up, ffd_w_down, ffd_ln_g, ffd_ln_b, c_w_in, s5_a_re, s5_a_im, s5_log_step, s5_b_re, s5_b_im, s5_c_re, s5_c_im, s5_d, glu_w, glu_b, c_w_out, c_ln_g, c_ln_b, moe_router, moe_w_gate, moe_w_up, moe_w_down, moe_ln_g, moe_ln_b):
    bsz, seq, d = x.shape
    x2d = x.reshape(bsz * seq, d)
    rows = min(ROW_TILE, seq)
    for i in range(DEPTH):
        j = i // 2
        if i % 2 == 0:
            x2d = _layer_even(x2d, positions, bsz, seq, ab_w_in[j], gm_ln_g[j], gm_ln_b[j], gm_w_s[j], gm_b_s[j],
                              mla_q_norm[j], mla_w_uq[j], mla_kv_norm[j], mla_w_ukv[j], ab_w_out[j], ab_ln_g[j],
                              ab_ln_b[j], ffd_w_gate[j], ffd_w_up[j], ffd_w_down[j], ffd_ln_g[j], ffd_ln_b[j],
                              rows=rows, ffn_rows=rows, attn_blk=min(ATTN_BLOCK, seq))
        else:
            x2d = _layer_odd(x2d, bsz, c_w_in[j], s5_a_re[j], s5_a_im[j], s5_log_step[j], s5_b_re[j],
                             s5_b_im[j], s5_c_re[j], s5_c_im[j], s5_d[j], glu_w[j], glu_b[j], c_w_out[j], c_ln_g[j],
                             c_ln_b[j], moe_router[j], moe_w_gate[j], moe_w_up[j], moe_w_down[j], moe_ln_g[j],
                             moe_ln_b[j], rows=rows, ffn_rows=rows, fchunk=moe_w_gate.shape[3])
    return x2d.reshape(bsz, seq, d)
```

```python
import functools
import math

import jax
import jax.numpy as jnp
from jax import lax
from jax.experimental import pallas as pl
from jax.experimental.pallas import tpu as pltpu

F32 = jnp.float32
BF16 = jnp.bfloat16

A_GROUPS = 4
A_GROUP_DIM = 128
A_WIDTH = A_GROUPS * A_GROUP_DIM
A_CHUNK = 128
MLA_HEADS = 8
QK_NOPE = 64
QK_ROPE = 32
QK_HEAD = QK_NOPE + QK_ROPE
V_HEAD = 64
Q_LORA = 384
KV_LORA = 256
ROPE_THETA = 10000.0
S5_GROUP_DIM = 16
S5_STATE = 64
N_EXPERTS = 8
LN_EPS = 1e-5
RMS_EPS = 1e-6
DEPTH = 2
DEEPNORM_ALPHA = (2.0 * DEPTH) ** 0.25

LANES = 128
HEAD_PAD = LANES
S5_GROUPS_PER_BLOCK = 2 * LANES // S5_GROUP_DIM
S5_STEPS = 16
S5_SCAN_LANES = 4 * LANES
VMEM_LIMIT = 56 * 1024 * 1024
NEG_BIG = -1e30
ROW_TILE = 512
ATTN_BLOCK = 512
COMBINE_PARTS = 4
FFN_CHUNK = 512
DMA_UNROLL = 16


def _params(sem):
    return pltpu.CompilerParams(dimension_semantics=sem, vmem_limit_bytes=VMEM_LIMIT)


def _gelu(x):
    c = math.sqrt(2.0 / math.pi)
    return 0.5 * x * (1.0 + jnp.tanh(c * (x + 0.044715 * (x * x * x))))


def _sigmoid(x):
    return 1.0 / (1.0 + jnp.exp(-x))


def _layernorm(x, g, b):
    mu = jnp.mean(x, axis=-1, keepdims=True)
    xc = x - mu
    var = jnp.mean(xc * xc, axis=-1, keepdims=True)
    return xc * lax.rsqrt(var + LN_EPS) * g + b


def _rmsnorm(x, g):
    ms = jnp.mean(x * x, axis=-1, keepdims=True)
    return x * lax.rsqrt(ms + RMS_EPS) * g


def _dot(a, b):
    return jnp.dot(a, b, preferred_element_type=F32)


def _rope_table_kernel(inv_ref, pos_ref, cos_ref, sin_ref):
    pos = pos_ref[...].astype(F32)
    for j in range(QK_ROPE // 2):
        ang = pos * inv_ref[j]
        cos_ref[j] = jnp.cos(ang)
        sin_ref[j] = jnp.sin(ang)


def _rope_tables(positions):
    n = positions.size
    half = QK_ROPE // 2
    inv_freq = 1.0 / (ROPE_THETA ** (jnp.arange(0, QK_ROPE, 2, dtype=F32) / QK_ROPE))
    pos2d = positions.reshape(n // LANES, LANES)
    cos_t, sin_t = pl.pallas_call(
        _rope_table_kernel,
        out_shape=(jax.ShapeDtypeStruct((half, n // LANES, LANES), F32),) * 2,
        in_specs=[pl.BlockSpec(memory_space=pltpu.SMEM), pl.BlockSpec(memory_space=pltpu.VMEM)],
        out_specs=(pl.BlockSpec(memory_space=pltpu.VMEM),) * 2,
        name="rope_table",
    )(inv_freq, pos2d)
    cos_c = cos_t.reshape(half, n).T
    sin_c = sin_t.reshape(half, n).T
    ones = jnp.ones((n, QK_NOPE), F32)
    zeros_n = jnp.zeros((n, QK_NOPE), F32)
    zeros_p = jnp.zeros((n, HEAD_PAD - QK_HEAD), F32)
    cos_p = jnp.concatenate([ones, cos_c, cos_c, zeros_p], axis=1)
    sin_p = jnp.concatenate([zeros_n, sin_c, sin_c, zeros_p], axis=1)
    return cos_p, sin_p


def _mixer_in_kernel(x_ref, cos_ref, sin_ref, w_in_ref, lng_ref, lnb_ref, ws_ref, bs_ref, qn_ref, wq_ref,
                     kvn_ref, wkv_ref, ya_ref, q_ref, k_ref, v_ref):
    rows = x_ref.shape[0]
    xb = x_ref[...].astype(BF16)
    h = _dot(xb, w_in_ref[...])
    o_q = 2 * A_WIDTH
    o_kv = o_q + Q_LORA
    o_pe = o_kv + KV_LORA
    o_rot = o_pe + HEAD_PAD
    cos_p = cos_ref[...]
    sin_p = sin_ref[...]

    a_u = _gelu(h[:, :A_WIDTH])
    a_v = _gelu(h[:, A_WIDTH:o_q])
    vn = _layernorm(a_v, lng_ref[...], lnb_ref[...]).astype(BF16)
    t_idx = lax.broadcasted_iota(jnp.int32, (A_CHUNK, A_CHUNK), 0)
    s_idx = lax.broadcasted_iota(jnp.int32, (A_CHUNK, A_CHUNK), 1)
    causal = s_idx <= t_idx
    bs = bs_ref[...]
    for g in range(A_GROUPS):
        w_g = jnp.where(causal, ws_ref[g], 0.0).astype(BF16)
        cols = slice(g * A_GROUP_DIM, (g + 1) * A_GROUP_DIM)
        for c in range(rows // A_CHUNK):
            rws = slice(c * A_CHUNK, (c + 1) * A_CHUNK)
            mixed = _dot(w_g, vn[rws, cols]) + bs[:, cols]
            ya_ref[rws, cols] = (a_u[rws, cols] * mixed).astype(BF16)

    cqn = _rmsnorm(h[:, o_q:o_kv], qn_ref[...]).astype(BF16)
    q2 = _dot(cqn, wq_ref[...])
    half = MLA_HEADS * HEAD_PAD
    for hd in range(MLA_HEADS):
        cols = slice(hd * HEAD_PAD, (hd + 1) * HEAD_PAD)
        rot = slice(half + hd * HEAD_PAD, half + (hd + 1) * HEAD_PAD)
        q_ref[:, cols] = (q2[:, cols] * cos_p + q2[:, rot] * sin_p).astype(BF16)

    ckvn = _rmsnorm(h[:, o_kv:o_pe], kvn_ref[...]).astype(BF16)
    kv = _dot(ckvn, wkv_ref[...])
    kpe = h[:, o_pe:o_rot] * cos_p + h[:, o_rot:o_rot + HEAD_PAD] * sin_p
    for hd in range(MLA_HEADS):
        cols = slice(hd * HEAD_PAD, (hd + 1) * HEAD_PAD)
        k_ref[:, cols] = (kv[:, cols] + kpe).astype(BF16)
    v_ref[...] = kv[:, half:].astype(BF16)


def _mixer_in(x2d, cos_p, sin_p, w_in, gm_ln_g, gm_ln_b, w_s, b_s, q_norm, w_uq, kv_norm, w_ukv, *, rows):
    n, d = x2d.shape
    hp = MLA_HEADS * HEAD_PAD
    o_pe = 2 * A_WIDTH + Q_LORA + KV_LORA
    half = QK_ROPE // 2
    w_pe = w_in[:, o_pe:o_pe + QK_ROPE]
    w_pe_rot = jnp.concatenate([-w_pe[:, half:], w_pe[:, :half]], axis=1)
    pad_l = jnp.zeros((d, QK_NOPE), F32)
    pad_r = jnp.zeros((d, HEAD_PAD - QK_HEAD), F32)
    w_in_p = jnp.concatenate([w_in[:, :o_pe], pad_l, w_pe, pad_r, pad_l, w_pe_rot, pad_r], axis=1).astype(BF16)
    wq = w_uq.reshape(Q_LORA, MLA_HEADS, QK_HEAD)
    wq_pe = wq[:, :, QK_NOPE:]
    wq_rot = jnp.concatenate([jnp.zeros((Q_LORA, MLA_HEADS, QK_NOPE), F32), -wq_pe[:, :, half:], wq_pe[:, :, :half]],
                             axis=2)
    padq = ((0, 0), (0, 0), (0, HEAD_PAD - QK_HEAD))
    wq2 = jnp.concatenate([jnp.pad(wq, padq).reshape(Q_LORA, hp), jnp.pad(wq_rot, padq).reshape(Q_LORA, hp)],
                          axis=1).astype(BF16)
    wkv = w_ukv.reshape(KV_LORA, MLA_HEADS, QK_NOPE + V_HEAD)
    wk = jnp.pad(wkv[:, :, :QK_NOPE], ((0, 0), (0, 0), (0, HEAD_PAD - QK_NOPE))).reshape(KV_LORA, hp)
    wv = wkv[:, :, QK_NOPE:].reshape(KV_LORA, MLA_HEADS * V_HEAD)
    wkv2 = jnp.concatenate([wk, wv], axis=1).astype(BF16)
    bs_full = jnp.repeat(b_s.T, A_GROUP_DIM, axis=1)

    full = lambda a: pl.BlockSpec(a.shape, lambda i: (0,) * a.ndim)
    row = lambda w: pl.BlockSpec((rows, w), lambda i: (i, 0))
    args = (x2d, cos_p, sin_p, w_in_p, gm_ln_g.reshape(1, -1), gm_ln_b.reshape(1, -1), w_s, bs_full,
            q_norm.reshape(1, -1), wq2, kv_norm.reshape(1, -1), wkv2)
    in_specs = [row(d), row(HEAD_PAD), row(HEAD_PAD)] + [full(a) for a in args[3:]]
    return pl.pallas_call(
        _mixer_in_kernel,
        grid=(n // rows,),
        in_specs=in_specs,
        out_specs=(row(A_WIDTH), row(hp), row(hp), row(MLA_HEADS * V_HEAD)),
        out_shape=(jax.ShapeDtypeStruct((n, A_WIDTH), BF16), jax.ShapeDtypeStruct((n, hp), BF16),
                   jax.ShapeDtypeStruct((n, hp), BF16), jax.ShapeDtypeStruct((n, MLA_HEADS * V_HEAD), BF16)),
        compiler_params=_params(("parallel",)),
        name="mixer_in",
    )(*args)


def _attn_kernel(q_ref, k_ref, v_ref, o_ref, *, blk):
    seq = q_ref.shape[0]
    scale = QK_HEAD ** -0.5
    row = lax.broadcasted_iota(jnp.int32, (blk, blk), 0)
    col = lax.broadcasted_iota(jnp.int32, (blk, blk), 1)
    diag_mask = col <= row
    first_head_lanes = lax.broadcasted_iota(jnp.int32, (blk, 2 * V_HEAD), 1) < V_HEAD

    qk = lambda a, b: lax.dot_general(a, b, (((1,), (1,)), ((), ())), preferred_element_type=F32)

    for j in range(seq // blk):
        q0 = j * blk
        outs = []
        for hh in range(2):
            cols = slice(hh * HEAD_PAD, (hh + 1) * HEAD_PAD)
            q = q_ref[q0:q0 + blk, cols]
            s_d = jnp.where(diag_mask, qk(q, k_ref[q0:q0 + blk, cols]), NEG_BIG)
            m = jnp.max(s_d, axis=-1, keepdims=True)
            if j > 0:
                s_o = qk(q, k_ref[0:q0, cols])
                m = jnp.maximum(m, jnp.max(s_o, axis=-1, keepdims=True))
            p_d = jnp.exp((s_d - m) * scale)
            l = jnp.sum(p_d, axis=-1, keepdims=True)
            acc = _dot(p_d.astype(BF16), v_ref[q0:q0 + blk, :])
            if j > 0:
                p_o = jnp.exp((s_o - m) * scale)
                l = l + jnp.sum(p_o, axis=-1, keepdims=True)
                acc = acc + _dot(p_o.astype(BF16), v_ref[0:q0, :])
            outs.append(acc / l)
        o_ref[q0:q0 + blk, :] = jnp.where(first_head_lanes, outs[0], outs[1]).astype(BF16)


def _attention(q, k, v, *, bsz, seq, blk):
    n = bsz * seq
    pair = 2 * HEAD_PAD
    return pl.pallas_call(
        functools.partial(_attn_kernel, blk=blk),
        grid=(bsz, MLA_HEADS // 2),
        in_specs=[pl.BlockSpec((seq, pair), lambda b, h: (b, h)), pl.BlockSpec((seq, pair), lambda b, h: (b, h)),
                  pl.BlockSpec((seq, 2 * V_HEAD), lambda b, h: (b, h))],
        out_specs=pl.BlockSpec((seq, 2 * V_HEAD), lambda b, h: (b, h)),
        out_shape=jax.ShapeDtypeStruct((n, MLA_HEADS * V_HEAD), BF16),
        compiler_params=_params(("parallel", "parallel")),
        name="attention",
    )(q, k, v)


def _swiglu_chunks(xb, wg_ref, wu_ref, wd_ref):
    dff = wg_ref.shape[1]
    acc = None
    for c0 in range(0, dff, FFN_CHUNK):
        cols = slice(c0, min(c0 + FFN_CHUNK, dff))
        gate = _dot(xb, wg_ref[:, cols])
        up = _dot(xb, wu_ref[:, cols])
        hid = (gate * _sigmoid(gate) * up).astype(BF16)
        part = _dot(hid, wd_ref[cols, :])
        acc = part if acc is None else acc + part
    return acc


def _mixer_ffn_kernel(ya_ref, yb_ref, woa_ref, wob_ref, x_ref, g1_ref, b1_ref, wg_ref, wu_ref, wd_ref, g2_ref, b2_ref,
                      o_ref):
    mix = _dot(ya_ref[...], woa_ref[...]) + _dot(yb_ref[...], wob_ref[...])
    x1 = _layernorm(DEEPNORM_ALPHA * x_ref[...] + mix, g1_ref[...], b1_ref[...])
    ffn = _swiglu_chunks(x1.astype(BF16), wg_ref, wu_ref, wd_ref)
    o_ref[...] = _layernorm(DEEPNORM_ALPHA * x1 + ffn, g2_ref[...], b2_ref[...])


def _mixer_ffn(ya, yb, wo_a, wo_b, x2d, g1, b1, w_gate, w_up, w_down, g2, b2, *, rows):
    n, d = x2d.shape
    once = lambda a: pl.BlockSpec(a.shape, lambda i: (0,) * a.ndim, pipeline_mode=pl.Buffered(1))
    row = lambda w: pl.BlockSpec((rows, w), lambda i: (i, 0))
    vec = lambda v: v.reshape(1, -1)
    args = (ya, yb, wo_a, wo_b, x2d, vec(g1), vec(b1), w_gate, w_up, w_down, vec(g2), vec(b2))
    in_specs = [row(ya.shape[1]), row(yb.shape[1]), once(wo_a), once(wo_b), row(d)] + [once(a) for a in args[5:]]
    return pl.pallas_call(
        _mixer_ffn_kernel,
        grid=(n // rows,),
        in_specs=in_specs,
        out_specs=row(d),
        out_shape=jax.ShapeDtypeStruct((n, d), F32),
        compiler_params=_params(("parallel",)),
        name="mixer_ffn",
    )(*args)


def _s5_kernel(x_ref, perm_ref, win_ref, bd_ref, cd_ref, lam_ref, d_ref, y_hbm, bur_ref, bui_ref, hr_ref, hi_ref,
               st_ref, ys_ref, sem, *, bsz):
    i = pl.program_id(0)
    n_steps = pl.num_programs(0)
    slot = i % 2
    steps = x_ref.shape[1]
    rows = bsz * steps

    def out_copies(at_step, at_slot):
        return [pltpu.make_async_copy(ys_ref.at[at_slot, pl.ds(t * bsz, bsz)], y_hbm.at[:, at_step * steps + t, :],
                                      sem.at[at_slot]) for t in range(steps)]

    @pl.when(i == 0)
    def _():
        st_ref[...] = jnp.zeros_like(st_ref)

    @pl.when(i >= 2)
    def _():
        for cp in out_copies(i - 2, slot):
            cp.wait()

    n_blk, cb, sb2 = bd_ref.shape
    sb = sb2 // 2
    xb = x_ref[...].reshape(rows, x_ref.shape[2]).astype(BF16)
    xb = _dot(perm_ref[...], xb).astype(BF16)
    u = _dot(xb, win_ref[...])
    ub = u.astype(BF16)
    for q in range(n_blk):
        bu = _dot(ub[:, q * cb:(q + 1) * cb], bd_ref[q])
        bur_ref[:, q * sb:(q + 1) * sb] = bu[:, :sb]
        bui_ref[:, q * sb:(q + 1) * sb] = bu[:, sb:]
    for c in range(bur_ref.shape[1] // S5_SCAN_LANES):
        lanes = slice(c * S5_SCAN_LANES, (c + 1) * S5_SCAN_LANES)
        a_re = lam_ref[0:1, lanes]
        a_im = lam_ref[1:2, lanes]
        h_re = st_ref[0, :, lanes]
        h_im = st_ref[1, :, lanes]
        for t in range(steps):
            rws = slice(t * bsz, (t + 1) * bsz)
            n_re = a_re * h_re - a_im * h_im + bur_ref[rws, lanes]
            n_im = a_re * h_im + a_im * h_re + bui_ref[rws, lanes]
            hr_ref[rws, lanes] = n_re.astype(BF16)
            hi_ref[rws, lanes] = n_im.astype(BF16)
            h_re, h_im = n_re, n_im
        st_ref[0, :, lanes] = h_re
        st_ref[1, :, lanes] = h_im
    for q in range(n_blk):
        st = slice(q * sb, (q + 1) * sb)
        ch = slice(q * cb, (q + 1) * cb)
        y = _dot(hr_ref[:, st], cd_ref[q, :sb]) + _dot(hi_ref[:, st], cd_ref[q, sb:])
        ys_ref[slot, :, ch] = y + d_ref[:, ch] * u[:, ch]
    for cp in out_copies(i, slot):
        cp.start()

    @pl.when(i == n_steps - 1)
    def _():
        for cp in out_copies(i, slot):
            cp.wait()

    @pl.when((i == n_steps - 1) & (i >= 1))
    def _():
        for cp in out_copies(i - 1, 1 - slot):
            cp.wait()


def _s5(x2d, w_in, bd, cd, lam, d_skip, *, bsz):
    n, d = x2d.shape
    seq = n // bsz
    rows = S5_STEPS * bsz
    n_state = lam.shape[1]
    full = lambda a: pl.BlockSpec(a.shape, lambda i: (0,) * a.ndim)
    d2 = d_skip.reshape(1, -1)
    r_out = jnp.arange(rows)[:, None]
    r_in = jnp.arange(rows)[None, :]
    perm = (r_in == (r_out % bsz) * S5_STEPS + r_out // bsz).astype(BF16)
    y = pl.pallas_call(
        functools.partial(_s5_kernel, bsz=bsz),
        grid=(seq // S5_STEPS,),
        in_specs=[pl.BlockSpec((bsz, S5_STEPS, d), lambda i: (0, i, 0)), full(perm), full(w_in), full(bd), full(cd),
                  full(lam), full(d2)],
        out_specs=pl.BlockSpec(memory_space=pl.ANY),
        out_shape=jax.ShapeDtypeStruct((bsz, seq, d), F32),
        scratch_shapes=[pltpu.VMEM((rows, n_state), F32), pltpu.VMEM((rows, n_state), F32),
                        pltpu.VMEM((rows, n_state), BF16), pltpu.VMEM((rows, n_state), BF16),
                        pltpu.VMEM((2, bsz, n_state), F32), pltpu.VMEM((2, rows, d), F32),
                        pltpu.SemaphoreType.DMA((2,))],
        compiler_params=_params(("arbitrary",)),
        name="s5",
    )(x2d.reshape(bsz, seq, d), perm, w_in, bd, cd, lam, d2)
    return y.reshape(n, d)


def _s5_weights(a_re, a_im, log_step, b_re, b_im, c_re, c_im):
    n_grp, n_st = a_re.shape
    gpb = S5_GROUPS_PER_BLOCK
    n_blk = n_grp // gpb
    delta = jnp.exp(log_step)[:, None]
    mag = jnp.exp(delta * a_re)
    abar_re = mag * jnp.cos(delta * a_im)
    abar_im = mag * jnp.sin(delta * a_im)
    den = a_re * a_re + a_im * a_im
    coef_re = ((abar_re - 1.0) * a_re + abar_im * a_im) / den
    coef_im = (abar_im * a_re - (abar_re - 1.0) * a_im) / den
    bb_re = coef_re[..., None] * b_re - coef_im[..., None] * b_im
    bb_im = coef_re[..., None] * b_im + coef_im[..., None] * b_re
    same_group = jnp.eye(gpb, dtype=F32)[None, :, None, :, None]

    def expand(w):
        w = jnp.transpose(w.reshape(n_blk, gpb, w.shape[1], w.shape[2]), (0, 1, 3, 2))
        w = w[:, :, :, None, :] * same_group
        return w.reshape(n_blk, gpb * w.shape[2], gpb * w.shape[4])

    bd = jnp.concatenate([expand(bb_re), expand(bb_im)], axis=2).astype(BF16)
    cd = jnp.concatenate([expand(c_re), -expand(c_im)], axis=1).astype(BF16)
    lam = jnp.stack([abar_re.reshape(-1), abar_im.reshape(-1)])
    return bd, cd, lam


def _s5_out_kernel(y_ref, gw_ref, gb_ref, wo_ref, x_ref, g_ref, b_ref, r_ref, o_ref, route_ref, count_ref,
                   carry_ref):
    @pl.when(pl.program_id(0) == 0)
    def _():
        carry_ref[...] = jnp.zeros_like(carry_ref)

    rows = x_ref.shape[0]
    gl = _gelu(y_ref[...])
    z = gl * _sigmoid(_dot(gl.astype(BF16), gw_ref[...]) + gb_ref[...])
    mix = _dot(z.astype(BF16), wo_ref[...])
    xo = _layernorm(DEEPNORM_ALPHA * x_ref[...] + mix, g_ref[...], b_ref[...])
    o_ref[...] = xo
    x_hi = xo.astype(BF16)
    x_lo = (xo - x_hi.astype(F32)).astype(BF16)
    r = r_ref[...]
    r_hi = r.astype(BF16)
    r_lo = (r - r_hi.astype(F32)).astype(BF16)
    logits = _dot(x_hi, r_hi) + (_dot(x_lo, r_hi) + _dot(x_hi, r_lo))
    idx = lax.broadcasted_iota(jnp.int32, logits.shape, 1)
    m1 = jnp.max(logits, axis=-1, keepdims=True)
    i1 = jnp.min(jnp.where(logits == m1, idx, N_EXPERTS), axis=-1, keepdims=True)
    rest = jnp.where(idx == i1, -jnp.inf, logits)
    m2 = jnp.max(rest, axis=-1, keepdims=True)
    i2 = jnp.min(jnp.where(rest == m2, idx, N_EXPERTS), axis=-1, keepdims=True)
    e2 = jnp.exp(m2 - m1)
    g1 = 1.0 / (1.0 + e2)
    g2 = e2 * g1
    sel = jnp.where((idx == i1) | (idx == i2), 1.0, 0.0)
    t_r = lax.broadcasted_iota(jnp.int32, (rows, rows), 0)
    t_c = lax.broadcasted_iota(jnp.int32, (rows, rows), 1)
    earlier = jnp.where(t_c < t_r, 1.0, 0.0).astype(BF16)
    before = _dot(earlier, sel.astype(BF16)) + carry_ref[...]
    r1 = jnp.sum(jnp.where(idx == i1, before, 0.0), axis=-1, keepdims=True)
    r2 = jnp.sum(jnp.where(idx == i2, before, 0.0), axis=-1, keepdims=True)
    total = carry_ref[...] + jnp.sum(sel, axis=0, keepdims=True)
    carry_ref[...] = total
    count_ref[...] = total
    fields = (i1.astype(F32), i2.astype(F32), r1, r2, g1, g2)
    route = jnp.zeros(logits.shape, F32)
    for lane, val in enumerate(fields):
        route = jnp.where(idx == lane, val, route)
    route_ref[...] = route


def _s5_out(y2d, glu_w, glu_b, w_out, x2d, g, b, router, *, rows):
    n, d = x2d.shape
    full = lambda a: pl.BlockSpec(a.shape, lambda i: (0,) * a.ndim)
    row = lambda w: pl.BlockSpec((rows, w), lambda i: (i, 0))
    args = (y2d, glu_w.astype(BF16), glu_b.reshape(1, -1), w_out.astype(BF16), x2d, g.reshape(1, -1),
            b.reshape(1, -1), router)
    in_specs = [row(d), full(args[1]), full(args[2]), full(args[3]), row(d), full(args[5]), full(args[6]),
                full(args[7])]
    return pl.pallas_call(
        _s5_out_kernel,
        grid=(n // rows,),
        in_specs=in_specs,
        out_specs=(row(d), row(N_EXPERTS), pl.BlockSpec((1, N_EXPERTS), lambda i: (0, 0))),
        out_shape=(jax.ShapeDtypeStruct((n, d), F32), jax.ShapeDtypeStruct((n, N_EXPERTS), F32),
                   jax.ShapeDtypeStruct((1, N_EXPERTS), F32)),
        scratch_shapes=[pltpu.VMEM((1, N_EXPERTS), F32)],
        compiler_params=_params(("arbitrary",)),
        name="s5_out",
    )(*args)


def _dispatch_kernel(fill_ref, slot_ref, x_ref, xs_ref, zero_ref, sem, zsem):
    rows = x_ref.shape[0]

    @pl.when(pl.program_id(0) == 0)
    def _():
        zero_ref[...] = jnp.zeros_like(zero_ref)
        for e in range(fill_ref.shape[0]):
            cp = pltpu.make_async_copy(zero_ref, xs_ref.at[pl.ds(pl.multiple_of(fill_ref[e], 8), rows)], zsem)
            cp.start()
            cp.wait()

    def issue(r, carry):
        for k in range(2):
            s = slot_ref[0, 0, 2 * r + k]
            pltpu.make_async_copy(x_ref.at[pl.ds(r, 1)], xs_ref.at[pl.ds(s, 1)], sem).start()
        return carry

    lax.fori_loop(0, rows, issue, 0, unroll=DMA_UNROLL)
    for k in range(2):
        pltpu.make_async_copy(x_ref, xs_ref.at[pl.ds(0, rows)], sem).wait()


def _dispatch(x2d, slots, fill_start, n_slots, *, rows):
    n, d = x2d.shape
    return pl.pallas_call(
        _dispatch_kernel,
        grid_spec=pltpu.PrefetchScalarGridSpec(
            num_scalar_prefetch=1,
            grid=(n // rows,),
            in_specs=[pl.BlockSpec((1, 1, 2 * rows), lambda i, fill: (i, 0, 0), memory_space=pltpu.SMEM),
                      pl.BlockSpec((rows, d), lambda i, fill: (i, 0))],
            out_specs=pl.BlockSpec(memory_space=pl.ANY),
            scratch_shapes=[pltpu.VMEM((rows, d), F32), pltpu.SemaphoreType.DMA, pltpu.SemaphoreType.DMA],
        ),
        out_shape=jax.ShapeDtypeStruct((n_slots, d), F32),
        compiler_params=_params(("arbitrary",)),
        name="moe_dispatch",
    )(fill_start, slots, x2d)


def _ffn_grouped_kernel(te_ref, na_ref, x_ref, wg_ref, wu_ref, wd_ref, o_ref, xb_ref, acc_ref):
    del te_ref
    i = pl.program_id(0)
    f = pl.program_id(1)
    active = i < na_ref[0]

    @pl.when(active)
    def _():
        @pl.when(f == 0)
        def _():
            xb_ref[...] = x_ref[...].astype(BF16)

        contrib = _swiglu_chunks(xb_ref[...], wg_ref.at[0], wu_ref.at[0], wd_ref.at[0])

        @pl.when(f == 0)
        def _():
            acc_ref[...] = contrib

        @pl.when(f > 0)
        def _():
            acc_ref[...] += contrib

    last = f == pl.num_programs(1) - 1

    @pl.when(last & active)
    def _():
        o_ref[...] = acc_ref[...]

    @pl.when(last & jnp.logical_not(active))
    def _():
        o_ref[...] = jnp.zeros_like(o_ref)


def _ffn_grouped(xs, tile_expert, n_active, w_gate, w_up, w_down, *, rows, fchunk):
    m, d = xs.shape
    n_f = w_gate.shape[2] // fchunk
    tile = lambda i, na: jnp.maximum(jnp.minimum(i, na[0] - 1), 0)
    chunk = lambda i, f, na: jnp.where(i < na[0], f, n_f - 1)
    return pl.pallas_call(
        _ffn_grouped_kernel,
        grid_spec=pltpu.PrefetchScalarGridSpec(
            num_scalar_prefetch=2,
            grid=(m // rows, n_f),
            in_specs=[pl.BlockSpec((rows, d), lambda i, f, te, na: (tile(i, na), 0)),
                      pl.BlockSpec((1, d, fchunk), lambda i, f, te, na: (te[tile(i, na)], 0, chunk(i, f, na)),
                                   pipeline_mode=pl.Buffered(2)),
                      pl.BlockSpec((1, d, fchunk), lambda i, f, te, na: (te[tile(i, na)], 0, chunk(i, f, na)),
                                   pipeline_mode=pl.Buffered(1)),
                      pl.BlockSpec((1, fchunk, d), lambda i, f, te, na: (te[tile(i, na)], chunk(i, f, na), 0),
                                   pipeline_mode=pl.Buffered(1))],
            out_specs=pl.BlockSpec((rows, d), lambda i, f, te, na: (i, 0)),
            scratch_shapes=[pltpu.VMEM((rows, d), BF16), pltpu.VMEM((rows, d), F32)],
        ),
        out_shape=jax.ShapeDtypeStruct((m, d), F32),
        compiler_params=_params(("arbitrary", "arbitrary")),
        name="ffn_moe",
    )(tile_expert, n_active, xs, w_gate, w_up, w_down)


def _combine_kernel(slot_ref, x_ref, route_ref, ys_ref, g_ref, b_ref, o_ref, y1_ref, y2_ref, sem):
    part = x_ref.shape[0] // COMBINE_PARTS

    for h in range(COMBINE_PARTS):
        def issue(r, carry, h=h):
            for k, dst in enumerate((y1_ref, y2_ref)):
                s = slot_ref[0, 0, 2 * r + k]
                pltpu.make_async_copy(ys_ref.at[pl.ds(s, 1)], dst.at[pl.ds(r, 1)], sem.at[h]).start()
            return carry

        lax.fori_loop(h * part, (h + 1) * part, issue, 0, unroll=DMA_UNROLL)
    for h in range(COMBINE_PARTS):
        rws = pl.ds(h * part, part)
        for dst in (y1_ref, y2_ref):
            pltpu.make_async_copy(ys_ref.at[pl.ds(0, part)], dst.at[rws], sem.at[h]).wait()
        route = route_ref[rws, :]
        moe = route[:, 4:5] * y1_ref[rws, :] + route[:, 5:6] * y2_ref[rws, :]
        o_ref[rws, :] = _layernorm(DEEPNORM_ALPHA * x_ref[rws, :] + moe, g_ref[...], b_ref[...])


def _combine_ln(x2d, route, slots, ys, g, b, *, rows):
    n, d = x2d.shape
    g2, b2 = g.reshape(1, -1), b.reshape(1, -1)
    return pl.pallas_call(
        _combine_kernel,
        grid=(n // rows,),
        in_specs=[pl.BlockSpec((1, 1, 2 * rows), lambda i: (i, 0, 0), memory_space=pltpu.SMEM),
                  pl.BlockSpec((rows, d), lambda i: (i, 0)),
                  pl.BlockSpec((rows, N_EXPERTS), lambda i: (i, 0)),
                  pl.BlockSpec(memory_space=pl.ANY),
                  pl.BlockSpec((1, d), lambda i: (0, 0)),
                  pl.BlockSpec((1, d), lambda i: (0, 0))],
        out_specs=pl.BlockSpec((rows, d), lambda i: (i, 0)),
        out_shape=jax.ShapeDtypeStruct((n, d), F32),
        scratch_shapes=[pltpu.VMEM((rows, d), F32), pltpu.VMEM((rows, d), F32),
                        pltpu.SemaphoreType.DMA((COMBINE_PARTS,))],
        compiler_params=_params(("arbitrary",)),
        name="moe_combine",
    )(slots, x2d, route, ys, g2, b2)


def _moe(x2d, route, counts, w_gate, w_up, w_down, g, b, *, rows, fchunk):
    n, _ = x2d.shape
    counts = counts.reshape(-1).astype(jnp.int32)
    padded = (counts + rows - 1) // rows * rows
    ends = jnp.cumsum(padded)
    offs = ends - padded
    n_tiles = (2 * n) // rows + N_EXPERTS
    starts = jnp.arange(n_tiles, dtype=jnp.int32) * rows
    tile_expert = jnp.minimum(jnp.sum(ends[None, :] <= starts[:, None], axis=1), N_EXPERTS - 1).astype(jnp.int32)
    n_active = (ends[-1:] // rows).astype(jnp.int32)
    expert = route[:, 0:2].astype(jnp.int32)
    rank = route[:, 2:4].astype(jnp.int32)
    base = jnp.sum(jnp.where(expert[..., None] == jnp.arange(N_EXPERTS), offs, 0), axis=-1)
    slots = (base + rank).reshape(n // rows, 1, 2 * rows)
    fill = jnp.concatenate([(offs + counts) // 8 * 8, ends[-1] + jnp.arange(N_EXPERTS, dtype=jnp.int32) * rows])
    fill_start = jnp.minimum(fill, (n_tiles - 1) * rows).astype(jnp.int32)
    xs = _dispatch(x2d, slots, fill_start, n_tiles * rows, rows=rows)
    ys = _ffn_grouped(xs, tile_expert, n_active, w_gate, w_up, w_down, rows=rows, fchunk=fchunk)
    return _combine_ln(x2d, route, slots, ys, g, b, rows=rows)


def _layer_even(x2d, positions, bsz, seq, w_in, gm_ln_g, gm_ln_b, w_s, b_s, q_norm, w_uq, kv_norm, w_ukv, w_out,
                ln_g, ln_b, f_gate, f_up, f_down, f_ln_g, f_ln_b, *, rows, ffn_rows, attn_blk):
    cos_p, sin_p = _rope_tables(positions)
    ya, q, k, v = _mixer_in(x2d, cos_p, sin_p, w_in, gm_ln_g, gm_ln_b, w_s, b_s, q_norm, w_uq, kv_norm, w_ukv,
                            rows=rows)
    yb = _attention(q, k, v, bsz=bsz, seq=seq, blk=attn_blk)
    wo = w_out.astype(BF16)
    return _mixer_ffn(ya, yb, wo[:A_WIDTH], wo[A_WIDTH:], x2d, ln_g, ln_b, f_gate.astype(BF16), f_up.astype(BF16),
                      f_down.astype(BF16), f_ln_g, f_ln_b, rows=ffn_rows)


def _layer_odd(x2d, bsz, w_in, a_re, a_im, log_step, b_re, b_im, c_re, c_im, d_skip, glu_w, glu_b, w_out, ln_g, ln_b,
               router, m_gate, m_up, m_down, m_ln_g, m_ln_b, *, rows, ffn_rows, fchunk):
    bd, cd, lam = _s5_weights(a_re, a_im, log_step, b_re, b_im, c_re, c_im)
    y2d = _s5(x2d, w_in.astype(BF16), bd, cd, lam, d_skip, bsz=bsz)
    x2d, route, counts = _s5_out(y2d, glu_w, glu_b, w_out, x2d, ln_g, ln_b, router, rows=rows)
    return _moe(x2d, route, counts, m_gate.astype(BF16), m_up.astype(BF16), m_down.astype(BF16), m_ln_g, m_ln_b,
                rows=ffn_rows, fchunk=fchunk)


def kernel(x, positions, ab_w_in, gm_ln_g, gm_ln_b, gm_w_s, gm_b_s, mla_q_norm, mla_w_uq, mla_kv_norm, mla_w_ukv, ab_w_out, ab_ln_g, ab_ln_b, ffd_w_gate, ffd_w_up, ffd_w_down, ffd_ln_g, ffd_ln_b, c_w_in, s5_a_re, s5_a_im, s5_log_step, s5_b_re, s5_b_im, s5_c_re, s5_c_im, s5_d, glu_w, glu_b, c_w_out, c_ln_g, c_ln_b, moe_router, moe_w_gate, moe_w_up, moe_w_down, moe_ln_g, moe_ln_b):
    bsz, seq, d = x.shape
    x2d = x.reshape(bsz * seq, d)
    rows = min(ROW_TILE, seq)
    for i in range(DEPTH):
        j = i // 2
        if i % 2 == 0:
            x2d = _layer_even(x2d, positions, bsz, seq, ab_w_in[j], gm_ln_g[j], gm_ln_b[j], gm_w_s[j], gm_b_s[j],
                              mla_q_norm[j], mla_w_uq[j], mla_kv_norm[j], mla_w_ukv[j], ab_w_out[j], ab_ln_g[j],
                              ab_ln_b[j], ffd_w_gate[j], ffd_w_up[j], ffd_w_down[j], ffd_ln_g[j], ffd_ln_b[j],
                              rows=rows, ffn_rows=rows, attn_blk=min(ATTN_BLOCK, seq))
        else:
            x2d = _layer_odd(x2d, bsz, c_w_in[j], s5_a_re[j], s5_a_im[j], s5_log_step[j], s5_b_re[j],
                             s5_b_im[j], s5_c_re[j], s5_c_im[j], s5_d[j], glu_w[j], glu_b[j], c_w_out[j], c_ln_g[j],
                             c_ln_b[j], moe_router[j], moe_w_gate[j], moe_w_up[j], moe_w_down[j], moe_ln_g[j],
                             moe_ln_b[j], rows=rows, ffn_rows=rows, fchunk=moe_w_gate.shape[3])
    return x2d.reshape(bsz, seq, d)
```
